```python
import math
import jax, jax.numpy as jnp
from jax import lax
import numpy as np

D_MODEL = 4096
BATCH = 32
SEQ = 256
DEPTH = 2
DEC_BATCH = 8
DEC_SEQ = 2048
PAST_LEN = 256

GRID_W = 64
N_EVEN = (DEPTH + 1) // 2
N_ODD = DEPTH // 2
N_MOD = 9
D_FF = 11008
EPS = 1e-6
ROPE_THETA = 10000.0
Q_BLOCK = 128
D_RNN = D_MODEL // 2
H_A = 16
BW_A = D_RNN // H_A
CONV_W = 4
CONV_PAD_L = 2
CONV_PAD_R = 1
LRU_C = 8.0
H_B = 16
DK_B = 64
DV_B = 2 * DK_B
D_ATT_B = H_B * DV_B
EV_SPLITS = [D_RNN, 2 * D_RNN, 2 * D_RNN + H_B * 2 * DK_B, 2 * D_RNN + 2 * H_B * 2 * DK_B]
EV_IN = 2 * D_RNN + 2 * H_B * 2 * DK_B + D_ATT_B
EV_MIX = D_RNN + D_ATT_B
H_C = 32
Q_LORA = 1024
KV_LORA = 512
NOPE_C = 128
ROPE_C = 64
V_C = 128
OD_IN = Q_LORA + KV_LORA + ROPE_C

kernel_name = "hybrid_rglru_diffattn_mla_prefix_dit_step"


def rms_norm(x, g):
    xf = x.astype(jnp.float32)
    y = xf * lax.rsqrt(jnp.mean(xf * xf, axis=-1, keepdims=True) + EPS)
    return (y * g.astype(jnp.float32)).astype(x.dtype)


def grid_positions(n_tok):
    rows = n_tok // GRID_W
    t = jnp.arange(rows * GRID_W, dtype=jnp.int32)
    return t // GRID_W, t % GRID_W


def _rotate(x, pos):
    n = x.shape[-1] // 2
    inv = ROPE_THETA ** (-jnp.arange(n, dtype=jnp.float32) / n)
    ang = pos.astype(jnp.float32)[:, None] * inv
    shp = (1, ang.shape[0]) + (1,) * (x.ndim - 3) + (n,)
    cos = jnp.cos(ang).reshape(shp)
    sin = jnp.sin(ang).reshape(shp)
    xf = x.astype(jnp.float32)
    x1, x2 = xf[..., :n], xf[..., n:]
    return jnp.concatenate([x1 * cos - x2 * sin, x1 * sin + x2 * cos], axis=-1).astype(x.dtype)


def axial_rope(x, row, col):
    half = x.shape[-1] // 2
    return jnp.concatenate([_rotate(x[..., :half], row), _rotate(x[..., half:], col)], axis=-1)


def _sweep(fn, qs):
    B, S = qs[0].shape[:2]
    nb = S // Q_BLOCK
    qb = tuple(jnp.moveaxis(q.reshape((B, nb, Q_BLOCK) + q.shape[2:]), 1, 0) for q in qs)
    out = lax.map(lambda t: fn(*t), qb)
    return jnp.moveaxis(out, 0, 1).reshape((B, S) + out.shape[3:])


def diff_core(q1, q2, k1, k2, v, lam):
    scale = DK_B ** -0.5

    def blk(q1b, q2b):
        s1 = jnp.einsum('bqhd,bkhd->bhqk', q1b, k1, preferred_element_type=jnp.float32) * scale
        s2 = jnp.einsum('bqhd,bkhd->bhqk', q2b, k2, preferred_element_type=jnp.float32) * scale
        a = jax.nn.softmax(s1, axis=-1) - lam * jax.nn.softmax(s2, axis=-1)
        return jnp.einsum('bhqk,bkhe->bqhe', a.astype(v.dtype), v)

    return _sweep(blk, (q1, q2))


def mha_core(q, k, v):
    scale = q.shape[-1] ** -0.5

    def blk(qb):
        s = jnp.einsum('bqhd,bkhd->bhqk', qb, k, preferred_element_type=jnp.float32) * scale
        p = jax.nn.softmax(s, axis=-1)
        return jnp.einsum('bhqk,bkhe->bqhe', p.astype(v.dtype), v)

    return _sweep(blk, (q,))


def centred_conv(x, w, b):
    S = x.shape[1]
    xp = jnp.pad(x, ((0, 0), (CONV_PAD_L, CONV_PAD_R), (0, 0)))
    out = b
    for j in range(CONV_W):
        out = out + xp[:, j:j + S] * w[j]
    return out


def linear_scan(a, b, h0):
    b = b.at[:, 0].add(a[:, 0] * h0)

    def comb(l, r):
        al, bl = l
        ar, br = r
        return al * ar, ar * bl + br

    _, h = lax.associative_scan(comb, (a, b), axis=1)
    return h


def rglru_dir(xc, wa, ba, wx, bx, lam, h0):
    B, S, _ = xc.shape
    xb = xc.reshape(B, S, H_A, BW_A)
    gr = jnp.einsum('bshi,hij->bshj', xb, wa).reshape(B, S, D_RNN) + ba
    gi = jnp.einsum('bshi,hij->bshj', xb, wx).reshape(B, S, D_RNN) + bx
    r = jax.nn.sigmoid(gr.astype(jnp.float32))
    i = jax.nn.sigmoid(gi.astype(jnp.float32))
    log_a = -LRU_C * r * jax.nn.softplus(-lam.astype(jnp.float32))
    a = jnp.exp(log_a)
    mult = jnp.sqrt(-jnp.expm1(2.0 * log_a))
    return linear_scan(a, mult * i * xc.astype(jnp.float32), h0)


def rglru_bidir(xc, wa, ba, wx, bx, lam, h0):
    hf = rglru_dir(xc, wa[0], ba[0], wx[0], bx[0], lam[0], h0[:, 0])
    hb_rev = rglru_dir(jnp.flip(xc, 1), wa[1], ba[1], wx[1], bx[1], lam[1], h0[:, 1])
    y = hf + jnp.flip(hb_rev, 1)
    h_fin = jnp.stack([hf[:, -1], hb_rev[:, -1]], axis=1)
    return y, h_fin


def even_mixer(h, W, e, lambda_init, ctx):
    B, S, _ = h.shape
    z = h @ W['ev_w_in'][e]
    xa, ga, q, k, v = jnp.split(z, EV_SPLITS, axis=-1)
    xc = centred_conv(xa, W['ev_conv_w'][e], W['ev_conv_b'][e])
    h0 = jnp.zeros((B, 2, D_RNN), jnp.float32) if ctx is None else ctx[0].astype(jnp.float32)
    y_rec, h_fin = rglru_bidir(xc, W['ev_wa'][e], W['ev_ba'][e], W['ev_wx'][e], W['ev_bx'][e], W['ev_lam'][e], h0)
    y_a = jax.nn.gelu(ga) * y_rec.astype(h.dtype)
    q = q.reshape(B, S, H_B, 2, DK_B)
    k = k.reshape(B, S, H_B, 2, DK_B)
    v = v.reshape(B, S, H_B, DV_B)
    if ctx is None:
        k_all, v_all = k, v
    else:
        row, col = ctx[3], ctx[4]
        q = axial_rope(q, row, col)
        k_lat = axial_rope(k, row, col)
        P = ctx[1].shape[1]
        k_all = jnp.concatenate([ctx[1].reshape(B, P, H_B, 2, DK_B).astype(k.dtype), k_lat], axis=1)
        v_all = jnp.concatenate([ctx[2].astype(v.dtype), v], axis=1)
    lq = W['ev_lq'][e].astype(jnp.float32)
    lk = W['ev_lk'][e].astype(jnp.float32)
    lam = jnp.exp(jnp.sum(lq[0] * lk[0])) - jnp.exp(jnp.sum(lq[1] * lk[1])) + lambda_init
    o = diff_core(q[..., 0, :], q[..., 1, :], k_all[..., 0, :], k_all[..., 1, :], v_all, lam)
    o = rms_norm(o, W['ev_subln_g'][e]) * (1.0 - lambda_init)
    y = jnp.concatenate([y_a, o.reshape(B, S, D_ATT_B)], axis=-1) @ W['ev_w_out'][e]
    new = (h_fin, k.reshape(B, S, H_B, 2 * DK_B), v) if ctx is None else None
    return y, new


def odd_mixer(h, W, o_idx, ctx):
    B, S, _ = h.shape
    z = h @ W['od_w_in'][o_idx]
    cq, ckv, krope = jnp.split(z, [Q_LORA, Q_LORA + KV_LORA], axis=-1)
    q = (rms_norm(cq, W['od_qnorm_g'][o_idx]) @ W['od_w_uq'][o_idx]).reshape(B, S, H_C, NOPE_C + ROPE_C)
    q_nope, q_rope = q[..., :NOPE_C], q[..., NOPE_C:]
    ckv = rms_norm(ckv, W['od_kvnorm_g'][o_idx])
    if ctx is None:
        ckv_all, kr_all = ckv, krope
    else:
        row, col = ctx[2], ctx[3]
        q_rope = axial_rope(q_rope, row, col)
        kr_lat = axial_rope(krope, row, col)
        ckv_all = jnp.concatenate([ctx[0].astype(ckv.dtype), ckv], axis=1)
        kr_all = jnp.concatenate([ctx[1].astype(krope.dtype), kr_lat], axis=1)
    T = ckv_all.shape[1]
    kv = (ckv_all @ W['od_w_ukv'][o_idx]).reshape(B, T, H_C, NOPE_C + V_C)
    k_nope, v = kv[..., :NOPE_C], kv[..., NOPE_C:]
    k = jnp.concatenate([k_nope, jnp.broadcast_to(kr_all[:, :, None, :], (B, T, H_C, ROPE_C))], axis=-1)
    qf = jnp.concatenate([q_nope, q_rope], axis=-1)
    o = mha_core(qf, k, v)
    y = o.reshape(B, S, H_C * V_C) @ W['od_w_out'][o_idx]
    new = (ckv, krope) if ctx is None else None
    return y, new


def swiglu(h, wg, wu, wd):
    return (jax.nn.silu(h @ wg) * (h @ wu)) @ wd


def modulation(cond, w, b):
    m = jax.nn.silu(cond) @ w + b
    m = m.reshape(m.shape[:-1] + (N_MOD, D_MODEL))
    if m.ndim == 3:
        m = m[:, None]
    return [m[..., j, :] for j in range(N_MOD)]


def run_stream(x, cond, W, caches):
    pos = None if caches is None else grid_positions(x.shape[1])
    new = {'rec': [], 'dk': [], 'dv': [], 'ckv': [], 'kr': []}
    for l in range(DEPTH):
        m = modulation(cond, W['mod_w'][l], W['mod_b'][l])
        hh = rms_norm(x, W['norm_g'][l, 0]) * (1 + m[1]) + m[0]
        x = x + 0.5 * m[2] * swiglu(hh, W['ffn_wg'][l, 0], W['ffn_wu'][l, 0], W['ffn_wd'][l, 0])
        hh = rms_norm(x, W['norm_g'][l, 1]) * (1 + m[4]) + m[3]
        if l % 2 == 0:
            e = l // 2
            ctx = None if caches is None else (caches['state_rglru'][:, e], caches['cache_dk'][:, e],
                                               caches['cache_dv'][:, e], pos[0], pos[1])
            y, nw = even_mixer(hh, W, e, 0.8 - 0.6 * math.exp(-0.3 * l), ctx)
            if nw is not None:
                new['rec'].append(nw[0]); new['dk'].append(nw[1]); new['dv'].append(nw[2])
        else:
            o_idx = l // 2
            ctx = None if caches is None else (caches['cache_ckv'][:, o_idx], caches['cache_krope'][:, o_idx],
                                               pos[0], pos[1])
            y, nw = odd_mixer(hh, W, o_idx, ctx)
            if nw is not None:
                new['ckv'].append(nw[0]); new['kr'].append(nw[1])
        x = x + m[5] * y
        hh = rms_norm(x, W['norm_g'][l, 2]) * (1 + m[7]) + m[6]
        x = x + 0.5 * m[8] * swiglu(hh, W['ffn_wg'][l, 1], W['ffn_wu'][l, 1], W['ffn_wd'][l, 1])
    return rms_norm(x, W['final_g']), new


def setup_inputs(seed: int = 0) -> dict:
    key = jax.random.key(seed)
    ks = iter(jax.random.split(key, 40))
    f32 = jnp.float32

    def nrm(shape, scale):
        return jax.random.normal(next(ks), shape, f32) * scale

    a8 = jax.random.uniform(next(ks), (N_EVEN, 2, D_RNN), f32, 0.9, 0.999)
    a_base = a8 ** (1.0 / LRU_C)
    ev_lam = jnp.log(a_base) - jnp.log1p(-a_base)
    dm = D_MODEL ** -0.5
    return {
        'x_prompt': nrm((BATCH, SEQ, D_MODEL), 1.0),
        'x_sample': nrm((DEC_BATCH, DEC_SEQ, D_MODEL), 1.0),
        'c': nrm((DEC_BATCH, D_MODEL), 1.0),
        'c_ctx': nrm((D_MODEL,), 1.0),
        'state_rglru': nrm((DEC_BATCH, N_EVEN, 2, D_RNN), 0.5),
        'cache_dk': nrm((DEC_BATCH, N_EVEN, PAST_LEN, H_B, 2 * DK_B), 1.0),
        'cache_dv': nrm((DEC_BATCH, N_EVEN, PAST_LEN, H_B, DV_B), 1.0),
        'cache_ckv': nrm((DEC_BATCH, N_ODD, PAST_LEN, KV_LORA), 1.0),
        'cache_krope': nrm((DEC_BATCH, N_ODD, PAST_LEN, ROPE_C), 1.0),
        'mod_w': nrm((DEPTH, D_MODEL, N_MOD * D_MODEL), 0.5 * dm),
        'mod_b': nrm((DEPTH, N_MOD * D_MODEL), 0.02),
        'norm_g': 1.0 + nrm((DEPTH, 3, D_MODEL), 0.02),
        'final_g': 1.0 + nrm((D_MODEL,), 0.02),
        'ffn_wg': nrm((DEPTH, 2, D_MODEL, D_FF), dm),
        'ffn_wu': nrm((DEPTH, 2, D_MODEL, D_FF), dm),
        'ffn_wd': nrm((DEPTH, 2, D_FF, D_MODEL), D_FF ** -0.5),
        'ev_w_in': nrm((N_EVEN, D_MODEL, EV_IN), dm),
        'ev_conv_w': nrm((N_EVEN, CONV_W, D_RNN), CONV_W ** -0.5),
        'ev_conv_b': nrm((N_EVEN, D_RNN), 0.02),
        'ev_wa': nrm((N_EVEN, 2, H_A, BW_A, BW_A), BW_A ** -0.5),
        'ev_ba': nrm((N_EVEN, 2, D_RNN), 0.02),
        'ev_wx': nrm((N_EVEN, 2, H_A, BW_A, BW_A), BW_A ** -0.5),
        'ev_bx': nrm((N_EVEN, 2, D_RNN), 0.02),
        'ev_lam': ev_lam,
        'ev_lq': nrm((N_EVEN, 2, DK_B), 0.1),
        'ev_lk': nrm((N_EVEN, 2, DK_B), 0.1),
        'ev_subln_g': 1.0 + nrm((N_EVEN, DV_B), 0.02),
        'ev_w_out': nrm((N_EVEN, EV_MIX, D_MODEL), EV_MIX ** -0.5),
        'od_w_in': nrm((N_ODD, D_MODEL, OD_IN), dm),
        'od_qnorm_g': 1.0 + nrm((N_ODD, Q_LORA), 0.02),
        'od_w_uq': nrm((N_ODD, Q_LORA, H_C * (NOPE_C + ROPE_C)), Q_LORA ** -0.5),
        'od_kvnorm_g': 1.0 + nrm((N_ODD, KV_LORA), 0.02),
        'od_w_ukv': nrm((N_ODD, KV_LORA, H_C * (NOPE_C + V_C)), KV_LORA ** -0.5),
        'od_w_out': nrm((N_ODD, H_C * V_C, D_MODEL), (H_C * V_C) ** -0.5),
    }


def reference(x_prompt, x_sample, c, c_ctx, state_rglru, cache_dk, cache_dv, cache_ckv, cache_krope,
              mod_w, mod_b, norm_g, final_g, ffn_wg, ffn_wu, ffn_wd,
              ev_w_in, ev_conv_w, ev_conv_b, ev_wa, ev_ba, ev_wx, ev_bx, ev_lam, ev_lq, ev_lk,
              ev_subln_g, ev_w_out, od_w_in, od_qnorm_g, od_w_uq, od_kvnorm_g, od_w_ukv, od_w_out):
    W = dict(mod_w=mod_w, mod_b=mod_b, norm_g=norm_g, final_g=final_g,
             ffn_wg=ffn_wg, ffn_wu=ffn_wu, ffn_wd=ffn_wd,
             ev_w_in=ev_w_in, ev_conv_w=ev_conv_w, ev_conv_b=ev_conv_b, ev_wa=ev_wa, ev_ba=ev_ba,
             ev_wx=ev_wx, ev_bx=ev_bx, ev_lam=ev_lam, ev_lq=ev_lq, ev_lk=ev_lk,
             ev_subln_g=ev_subln_g, ev_w_out=ev_w_out,
             od_w_in=od_w_in, od_qnorm_g=od_qnorm_g, od_w_uq=od_w_uq, od_kvnorm_g=od_kvnorm_g,
             od_w_ukv=od_w_ukv, od_w_out=od_w_out)
    y_prompt, new = run_stream(x_prompt, c_ctx, W, None)
    new_state_rglru = jnp.stack(new['rec'], axis=1)
    new_cache_dk = jnp.stack(new['dk'], axis=1)
    new_cache_dv = jnp.stack(new['dv'], axis=1)
    new_cache_ckv = jnp.stack(new['ckv'], axis=1)
    new_cache_krope = jnp.stack(new['kr'], axis=1)
    caches = dict(state_rglru=state_rglru, cache_dk=cache_dk, cache_dv=cache_dv,
                  cache_ckv=cache_ckv, cache_krope=cache_krope)
    y_sample, _ = run_stream(x_sample, c, W, caches)
    return (y_prompt, y_sample, new_state_rglru, new_cache_dk, new_cache_dv, new_cache_ckv, new_cache_krope)
```

```python
import functools
import math

import jax
import jax.numpy as jnp
from jax import lax
from jax.experimental import pallas as pl
from jax.experimental.pallas import tpu as pltpu

BF = jnp.bfloat16
F32 = jnp.float32

GRID_W = 64
EPS = 1e-6
ROPE_THETA = 10000.0
N_MOD = 9
H_A = 16
CONV_W = 4
LRU_C = 8.0
H_B = 16
DK_B = 64
H_C = 32
NOPE_C = 128
ROPE_C = 64
V_C = 128

LANES = 128
SUBLANES = 8
VMEM_LIMIT = 56 * 1024 * 1024
FF_ALIGN = 1024


def _cparams(*sem):
    return pltpu.CompilerParams(dimension_semantics=sem, vmem_limit_bytes=VMEM_LIMIT)


def _pick(n, pref, mult):
    best = None
    d = mult
    while d <= min(n, pref):
        if n % d == 0:
            best = d
        d += mult
    return n if best is None else best


def _sigmoid(x):
    return 1.0 / (1.0 + jnp.exp(-x))


def _rms(x):
    return x * lax.rsqrt(jnp.mean(x * x, axis=-1, keepdims=True) + EPS)


def _mod_kernel(c_ref, w_ref, b_ref, o_ref):
    k = pl.program_id(2)
    c = c_ref[...]
    s = (c * _sigmoid(c)).astype(BF)
    part = jnp.dot(s, w_ref[...].astype(BF), preferred_element_type=F32)

    @pl.when(k == 0)
    def _():
        o_ref[...] = part + b_ref[...]

    @pl.when(k > 0)
    def _():
        o_ref[...] += part


def _modulation(cond, mod_w, mod_b):
    R, D = cond.shape
    L, _, N = mod_w.shape
    tn = _pick(N, 2048, LANES)
    tk = _pick(D, 1024, LANES)
    return pl.pallas_call(
        _mod_kernel,
        grid=(L, N // tn, D // tk),
        in_specs=[
            pl.BlockSpec((R, tk), lambda l, n, k: (0, k)),
            pl.BlockSpec((None, tk, tn), lambda l, n, k: (l, k, n)),
            pl.BlockSpec((None, 1, tn), lambda l, n, k: (l, 0, n)),
        ],
        out_specs=pl.BlockSpec((None, R, tn), lambda l, n, k: (l, 0, n)),
        out_shape=jax.ShapeDtypeStruct((L, R, N), F32),
        compiler_params=_cparams("parallel", "parallel", "arbitrary"),
        name="modulation",
    )(cond, mod_w, mod_b.reshape(L, 1, N))


class _Tokens:
    def __init__(self, mp, ds, nb, n_rows):
        self.mp, self.ds, self.nb, self.n_rows = mp, ds, nb, n_rows
        self.m = mp + ds * nb

    def tile(self, pref):
        return _pick(math.gcd(self.mp, self.ds), pref, SUBLANES)

    def mod_index(self, layer, chunk, tm):
        mp, ds, n_rows = self.mp, self.ds, self.n_rows

        def f(i):
            r = jnp.where(i * tm < mp, 0, 1 + (i * tm - mp) // ds)
            return (layer * n_rows + r) * N_MOD + chunk
        return f


def _prenorm_kernel(x_ref, g_ref, sh_ref, sc_ref, o_ref):
    y = _rms(x_ref[...]) * g_ref[...]
    o_ref[...] = (y * (1.0 + sc_ref[...]) + sh_ref[...]).astype(o_ref.dtype)


def _prenorm(x, normt, modt, tok, layer, sub):
    M, D = x.shape
    tm = tok.tile(256)
    sh = tok.mod_index(layer, 3 * sub, tm)
    sc = tok.mod_index(layer, 3 * sub + 1, tm)
    return pl.pallas_call(
        _prenorm_kernel,
        grid=(M // tm,),
        in_specs=[
            pl.BlockSpec((tm, D), lambda i: (i, 0)),
            pl.BlockSpec((None, 1, D), lambda i: (layer * 3 + sub, 0, 0)),
            pl.BlockSpec((None, 1, D), lambda i: (sh(i), 0, 0)),
            pl.BlockSpec((None, 1, D), lambda i: (sc(i), 0, 0)),
        ],
        out_specs=pl.BlockSpec((tm, D), lambda i: (i, 0)),
        out_shape=jax.ShapeDtypeStruct((M, D), BF),
        compiler_params=_cparams("parallel"),
        name="prenorm",
    )(x, normt, modt, modt)


def _final_norm_kernel(x_ref, g_ref, o_ref):
    o_ref[...] = _rms(x_ref[...]) * g_ref[...]


def _final_norm(x, g, row0, rows):
    D = x.shape[1]
    tm = _pick(math.gcd(row0, rows) if row0 else rows, 256, SUBLANES)
    off = row0 // tm
    return pl.pallas_call(
        _final_norm_kernel,
        grid=(rows // tm,),
        in_specs=[
            pl.BlockSpec((tm, D), lambda i: (off + i, 0)),
            pl.BlockSpec((1, D), lambda i: (0, 0)),
        ],
        out_specs=pl.BlockSpec((tm, D), lambda i: (i, 0)),
        out_shape=jax.ShapeDtypeStruct((rows, D), F32),
        compiler_params=_cparams("parallel"),
        name="final_norm",
    )(x, g.reshape(1, D))


def _mm_kernel(*refs, nk, residual, coef):
    if residual:
        a_ref, w_ref, r_ref, g_ref, o_ref = refs[:5]
    else:
        a_ref, w_ref, o_ref = refs[:3]
    acc_ref = refs[-1] if nk > 1 else None

    def finish(acc):
        if residual:
            o_ref[...] = r_ref[...] + (coef * g_ref[...]) * acc
        else:
            o_ref[...] = acc.astype(o_ref.dtype)

    part = jnp.dot(a_ref[...].astype(BF), w_ref[...], preferred_element_type=F32)
    if nk == 1:
        finish(part)
        return
    k = pl.program_id(2)

    @pl.when(k == 0)
    def _():
        acc_ref[...] = part

    @pl.when(k > 0)
    def _():
        acc_ref[...] += part

    @pl.when(k == nk - 1)
    def _():
        finish(acc_ref[...])


def _mm(a, w, out_dtype, *, tm, tn=None, tk=None, row0=0, rows=None, name="mm"):
    K, N = w.shape
    tn = _pick(N, 1024, LANES) if tn is None else tn
    rows = a.shape[0] if rows is None else rows
    tk = K if tk is None else tk
    nk = K // tk
    off = row0 // tm
    return pl.pallas_call(
        functools.partial(_mm_kernel, nk=nk, residual=False, coef=None),
        grid=(rows // tm, N // tn, nk),
        in_specs=[
            pl.BlockSpec((tm, tk), lambda i, j, k: (off + i, k)),
            pl.BlockSpec((tk, tn), lambda i, j, k: (k, j)),
        ],
        out_specs=pl.BlockSpec((tm, tn), lambda i, j, k: (i, j)),
        out_shape=jax.ShapeDtypeStruct((rows, N), out_dtype),
        scratch_shapes=[pltpu.VMEM((tm, tn), F32)] if nk > 1 else [],
        compiler_params=_cparams("parallel", "parallel", "arbitrary"),
        name=name,
    )(a, w)


def _mm_residual(a, w, x, modt, tok, layer, chunk, coef, *, tm, tn, tk=None, name="mm_res"):
    K, N = w.shape
    M = a.shape[0]
    tk = K if tk is None else tk
    nk = K // tk
    gate = tok.mod_index(layer, chunk, tm)
    return pl.pallas_call(
        functools.partial(_mm_kernel, nk=nk, residual=True, coef=coef),
        grid=(M // tm, N // tn, nk),
        in_specs=[
            pl.BlockSpec((tm, tk), lambda i, j, k: (i, k)),
            pl.BlockSpec((tk, tn), lambda i, j, k: (k, j)),
            pl.BlockSpec((tm, tn), lambda i, j, k: (i, j)),
            pl.BlockSpec((None, 1, tn), lambda i, j, k: (gate(i), 0, j)),
        ],
        out_specs=pl.BlockSpec((tm, tn), lambda i, j, k: (i, j)),
        out_shape=jax.ShapeDtypeStruct((M, N), F32),
        scratch_shapes=[pltpu.VMEM((tm, tn), F32)] if nk > 1 else [],
        compiler_params=_cparams("parallel", "parallel", "arbitrary"),
        name=name,
    )(a, w, x, modt)


def _ffn_up_kernel(h_ref, wg_ref, wu_ref, o_ref):
    h = h_ref[...]
    g = jnp.dot(h, wg_ref[...], preferred_element_type=F32)
    u = jnp.dot(h, wu_ref[...], preferred_element_type=F32)
    o_ref[...] = ((g * _sigmoid(g)) * u).astype(o_ref.dtype)


def _ffn_up(h, wg, wu, *, tm, tf):
    M, D = h.shape
    FP = wg.shape[1]
    return pl.pallas_call(
        _ffn_up_kernel,
        grid=(M // tm, FP // tf),
        in_specs=[
            pl.BlockSpec((tm, D), lambda i, j: (i, 0)),
            pl.BlockSpec((D, tf), lambda i, j: (0, j)),
            pl.BlockSpec((D, tf), lambda i, j: (0, j)),
        ],
        out_specs=pl.BlockSpec((tm, tf), lambda i, j: (i, j)),
        out_shape=jax.ShapeDtypeStruct((M, FP), BF),
        compiler_params=_cparams("parallel", "parallel"),
        name="ffn_up",
    )(h, wg, wu)


def _ffn(x, normt, modt, tok, layer, sub, wg, wu, wd):
    h = _prenorm(x, normt, modt, tok, layer, sub)
    FP = wg.shape[1]
    tm = tok.tile(1024)
    a = _ffn_up(h, wg, wu, tm=tm, tf=_pick(FP, 512, LANES))
    tk = _pick(FP, max(FP // 4, LANES), LANES)
    return _mm_residual(a, wd, x, modt, tok, layer, 3 * sub + 2, 0.5,
                        tm=tm, tn=_pick(x.shape[1], 1024, LANES), tk=tk, name="ffn_down")


def _gelu_tanh(x):
    return x * (0.5 * (1.0 + jnp.tanh(math.sqrt(2.0 / math.pi) * (x + 0.044715 * (x * x * x)))))


def _softplus(x):
    return jnp.maximum(x, 0.0) + jnp.log1p(jnp.exp(-jnp.abs(x)))


def _rglru_kernel(xa_ref, ga_ref, cw_ref, cb_ref, w_ref, b_ref, lam_ref, h0_ref,
                  y_ref, hfin_ref, pad_ref, af_ref, bf_ref, ab_ref, bb_ref, *, S):
    C = LANES
    P0 = SUBLANES
    pad_ref[0:P0, :] = jnp.zeros((P0, C), F32)
    pad_ref[P0 + S:P0 + S + P0, :] = jnp.zeros((P0, C), F32)
    pad_ref[P0:P0 + S, :] = xa_ref[...]
    cw = cw_ref[...]
    xc = jnp.broadcast_to(cb_ref[...], (S, C))
    for j in range(CONV_W):
        xc = xc + pad_ref[pl.ds(P0 - 2 + j, S), :] * cw[j:j + 1, :]

    gates = jnp.dot(xc.astype(BF), w_ref[...], preferred_element_type=F32) + b_ref[...]
    lam = lam_ref[...]
    for d, (a_ref, b_ref_) in enumerate(((af_ref, bf_ref), (ab_ref, bb_ref))):
        r = _sigmoid(gates[:, (2 * d) * C:(2 * d + 1) * C])
        i = _sigmoid(gates[:, (2 * d + 1) * C:(2 * d + 2) * C])
        log_a = (-LRU_C * r) * _softplus(-lam[d:d + 1, :])
        a_ref[...] = jnp.exp(log_a)
        th = jnp.tanh(log_a)
        one_minus_a2 = (-2.0 * th) / (1.0 - th)
        b_ref_[...] = (jnp.sqrt(one_minus_a2) * i) * xc

    rows = lax.broadcasted_iota(jnp.int32, (SUBLANES, C), 0)

    def tile_scan(a, b, down):
        for k in (1, 2, 4):
            if down:
                keep = rows >= k
                shift = k
            else:
                keep = rows < SUBLANES - k
                shift = SUBLANES - k
            a1 = jnp.where(keep, pltpu.roll(a, shift, 0), 1.0)
            b1 = jnp.where(keep, pltpu.roll(b, shift, 0), 0.0)
            b = a * b1 + b
            a = a * a1
        return a, b

    nt = S // SUBLANES

    def body(t, carry):
        hf, hb = carry
        r0 = pl.multiple_of(t * SUBLANES, SUBLANES)
        a, b = tile_scan(af_ref[pl.ds(r0, SUBLANES), :], bf_ref[pl.ds(r0, SUBLANES), :], True)
        h = a * hf + b
        bf_ref[pl.ds(r0, SUBLANES), :] = h
        hf = jnp.broadcast_to(h[SUBLANES - 1:SUBLANES, :], (SUBLANES, C))
        r1 = pl.multiple_of((nt - 1 - t) * SUBLANES, SUBLANES)
        a, b = tile_scan(ab_ref[pl.ds(r1, SUBLANES), :], bb_ref[pl.ds(r1, SUBLANES), :], False)
        g = a * hb + b
        bb_ref[pl.ds(r1, SUBLANES), :] = g
        hb = jnp.broadcast_to(g[0:1, :], (SUBLANES, C))
        return hf, hb

    h0 = h0_ref[...]
    hf, hb = lax.fori_loop(0, nt, body, (jnp.broadcast_to(h0[0:1, :], (SUBLANES, C)),
                                         jnp.broadcast_to(h0[1:2, :], (SUBLANES, C))))
    hfin_ref[0:1, :] = hf[0:1, :]
    hfin_ref[1:2, :] = hb[0:1, :]
    y_ref[...] = (_gelu_tanh(ga_ref[...]) * (bf_ref[...] + bb_ref[...])).astype(y_ref.dtype)


def _rglru(z, row0, nb, S, d_rnn, conv_w, conv_b, wcat, bcat, lam, h0):
    C = LANES
    ncb = d_rnn // C
    roff = row0 // S
    return pl.pallas_call(
        functools.partial(_rglru_kernel, S=S),
        grid=(nb, ncb),
        in_specs=[
            pl.BlockSpec((S, C), lambda b, c: (roff + b, c)),
            pl.BlockSpec((S, C), lambda b, c: (roff + b, ncb + c)),
            pl.BlockSpec((CONV_W, C), lambda b, c: (0, c)),
            pl.BlockSpec((1, C), lambda b, c: (0, c)),
            pl.BlockSpec((None, C, 4 * C), lambda b, c: (c, 0, 0)),
            pl.BlockSpec((None, 1, 4 * C), lambda b, c: (c, 0, 0)),
            pl.BlockSpec((2, C), lambda b, c: (0, c)),
            pl.BlockSpec((None, 2, C), lambda b, c: (b, 0, c)),
        ],
        out_specs=[
            pl.BlockSpec((S, C), lambda b, c: (b, c)),
            pl.BlockSpec((None, 2, C), lambda b, c: (b, 0, c)),
        ],
        out_shape=[
            jax.ShapeDtypeStruct((nb * S, d_rnn), BF),
            jax.ShapeDtypeStruct((nb, 2, d_rnn), F32),
        ],
        scratch_shapes=[pltpu.VMEM((S + 2 * SUBLANES, C), F32)] + [pltpu.VMEM((S, C), F32)] * 4,
        compiler_params=_cparams("parallel", "parallel"),
        name="rglru",
    )(z, z, conv_w, conv_b.reshape(1, d_rnn), wcat, bcat, lam, h0)


def _rope_tables(n_tok):
    t = jnp.arange(n_tok, dtype=jnp.int32)
    row, col = t // GRID_W, t % GRID_W
    n = 16
    inv = ROPE_THETA ** (-jnp.arange(n, dtype=F32) / n)
    ang_r = row.astype(F32)[:, None] * inv
    ang_c = col.astype(F32)[:, None] * inv

    def grp(ang):
        c, s = jnp.cos(ang), jnp.sin(ang)
        return jnp.concatenate([c, c], -1), jnp.concatenate([-s, s], -1)
    cr, sr = grp(ang_r)
    cc, sc = grp(ang_c)
    return jnp.concatenate([cr, cc, cr, cc], -1), jnp.concatenate([sr, sc, sr, sc], -1)


def _rope_kernel(x_ref, cos_ref, sin_ref, o_ref):
    cos = cos_ref[...]
    sin = sin_ref[...]
    first = (lax.broadcasted_iota(jnp.int32, cos.shape, 1) % 32) < 16
    for c in range(x_ref.shape[1] // LANES):
        x = x_ref[:, c * LANES:(c + 1) * LANES].astype(F32)
        partner = jnp.where(first, pltpu.roll(x, LANES - 16, 1), pltpu.roll(x, 16, 1))
        o_ref[:, c * LANES:(c + 1) * LANES] = (x * cos + partner * sin).astype(o_ref.dtype)


def _rope(x, cos, sin, row0, rows, col0, width, ds):
    tm = _pick(ds, 256, SUBLANES)
    tw = _pick(width, 1024, LANES)
    roff, coff, per = row0 // tm, col0 // tw, ds // tm
    return pl.pallas_call(
        _rope_kernel,
        grid=(rows // tm, width // tw),
        in_specs=[
            pl.BlockSpec((tm, tw), lambda i, j: (roff + i, coff + j)),
            pl.BlockSpec((tm, LANES), lambda i, j: (i % per, 0)),
            pl.BlockSpec((tm, LANES), lambda i, j: (i % per, 0)),
        ],
        out_specs=pl.BlockSpec((tm, tw), lambda i, j: (i, j)),
        out_shape=jax.ShapeDtypeStruct((rows, width), BF),
        compiler_params=_cparams("parallel", "parallel"),
        name="rope",
    )(x, cos, sin)


def _softmax_rows(s):
    e = jnp.exp(s - jnp.max(s, axis=-1, keepdims=True))
    return e / jnp.sum(e, axis=-1, keepdims=True)


def _dattn_kernel(*refs, P, S, tq, lam_init):
    if P:
        q_ref, kc_ref, vc_ref, kl_ref, vl_ref, lq_ref, lk_ref, g_ref, o_ref, kk, vv = refs
    else:
        q_ref, kl_ref, vl_ref, lq_ref, lk_ref, g_ref, o_ref, kk, vv = refs

    @pl.when(pl.program_id(2) == 0)
    def _():
        if P:
            kk[0:P, :] = kc_ref[...].astype(BF)
            vv[0:P, :] = vc_ref[...].astype(BF)
        kk[P:P + S, :] = kl_ref[...].astype(BF)
        vv[P:P + S, :] = vl_ref[...].astype(BF)

    q = q_ref[...].astype(F32)
    lane = lax.broadcasted_iota(jnp.int32, q.shape, 1)
    qq = jnp.concatenate([jnp.where(lane < DK_B, q, 0.0), jnp.where(lane >= DK_B, q, 0.0)], axis=0).astype(BF)
    s = lax.dot_general(qq, kk[...], (((1,), (1,)), ((), ())), preferred_element_type=F32) * (DK_B ** -0.5)
    p = _softmax_rows(s)
    e = jnp.exp(jnp.sum(lq_ref[...] * lk_ref[...], axis=-1, keepdims=True))
    lam = e[0:1, :] - e[1:2, :] + lam_init
    a = p[:tq, :] - lam * p[tq:, :]
    o = jnp.dot(a.astype(BF), vv[...], preferred_element_type=F32)
    o_ref[...] = ((_rms(o) * g_ref[...]) * (1.0 - lam_init)).astype(o_ref.dtype)


def _dattn(q, qrow0, qcol0, k_lat, krow0, kcol0, v_lat, vrow0, vcol0, cache, nb, S, lq, lk, g, lam_init):
    C = 2 * DK_B
    tq = _pick(S, 256, SUBLANES)
    nq = S // tq
    P = cache[0].shape[1] if cache is not None else 0
    qr, kr, vr = qrow0 // tq, krow0 // S, vrow0 // S
    qc, kc, vc = qcol0 // C, kcol0 // C, vcol0 // C
    in_specs = [pl.BlockSpec((tq, C), lambda b, h, i: (qr + b * nq + i, qc + h))]
    args = [q]
    if P:
        in_specs += [pl.BlockSpec((None, P, C), lambda b, h, i: (b, 0, h)),
                     pl.BlockSpec((None, P, C), lambda b, h, i: (b, 0, h))]
        args += [cache[0], cache[1]]
    in_specs += [
        pl.BlockSpec((S, C), lambda b, h, i: (kr + b, kc + h)),
        pl.BlockSpec((S, C), lambda b, h, i: (vr + b, vc + h)),
        pl.BlockSpec((2, DK_B), lambda b, h, i: (0, 0)),
        pl.BlockSpec((2, DK_B), lambda b, h, i: (0, 0)),
        pl.BlockSpec((1, C), lambda b, h, i: (0, 0)),
    ]
    args += [k_lat, v_lat, lq, lk, g.reshape(1, C)]
    return pl.pallas_call(
        functools.partial(_dattn_kernel, P=P, S=S, tq=tq, lam_init=lam_init),
        grid=(nb, H_B, nq),
        in_specs=in_specs,
        out_specs=pl.BlockSpec((tq, C), lambda b, h, i: (b * nq + i, h)),
        out_shape=jax.ShapeDtypeStruct((nb * S, H_B * C), BF),
        scratch_shapes=[pltpu.VMEM((P + S, C), BF), pltpu.VMEM((P + S, C), BF)],
        compiler_params=_cparams("parallel", "parallel", "arbitrary"),
        name="diff_attn",
    )(*args)


def _mla_in_kernel(h_ref, w_ref, gq_ref, gkv_ref, cq_ref, ckv_ref, kr_ref, *, ql, kvl):
    z = jnp.dot(h_ref[...], w_ref[...], preferred_element_type=F32)
    cq_ref[...] = (_rms(z[:, :ql]) * gq_ref[...]).astype(cq_ref.dtype)
    ckv_ref[...] = _rms(z[:, ql:ql + kvl]) * gkv_ref[...]
    kr_ref[...] = z[:, ql + kvl:]


def _mla_in(h, w, gq, gkv, ql, kvl, tm):
    M, D = h.shape
    N = w.shape[1]
    return pl.pallas_call(
        functools.partial(_mla_in_kernel, ql=ql, kvl=kvl),
        grid=(M // tm,),
        in_specs=[
            pl.BlockSpec((tm, D), lambda i: (i, 0)),
            pl.BlockSpec((D, N), lambda i: (0, 0)),
            pl.BlockSpec((1, ql), lambda i: (0, 0)),
            pl.BlockSpec((1, kvl), lambda i: (0, 0)),
        ],
        out_specs=[
            pl.BlockSpec((tm, ql), lambda i: (i, 0)),
            pl.BlockSpec((tm, kvl), lambda i: (i, 0)),
            pl.BlockSpec((tm, LANES), lambda i: (i, 0)),
        ],
        out_shape=[
            jax.ShapeDtypeStruct((M, ql), BF),
            jax.ShapeDtypeStruct((M, kvl), F32),
            jax.ShapeDtypeStruct((M, LANES), F32),
        ],
        compiler_params=_cparams("parallel"),
        name="mla_in",
    )(h, w, gq.reshape(1, ql), gkv.reshape(1, kvl))


def _mla_kernel(*refs, P, S, tq):
    if P:
        qn_ref, qr_ref, kvc_ref, krc_ref, kvl_ref, krl_ref, o_ref, k0, k1, vv = refs
        segs = ((0, P, kvc_ref, krc_ref), (P, S, kvl_ref, krl_ref))
    else:
        qn_ref, qr_ref, kvl_ref, krl_ref, o_ref, k0, k1, vv = refs
        segs = ((0, S, kvl_ref, krl_ref),)
    C = LANES

    @pl.when(pl.program_id(2) == 0)
    def _():
        for r0, n, kv_ref, kr_ref in segs:
            kr = kr_ref[...].astype(F32)
            kr2 = (kr + pltpu.roll(kr, ROPE_C, 1)).astype(BF)
            k0[r0:r0 + n, 0:C] = kv_ref[:, 0:C]
            k0[r0:r0 + n, C:2 * C] = kr2
            k1[r0:r0 + n, 0:C] = kv_ref[:, 2 * C:3 * C]
            k1[r0:r0 + n, C:2 * C] = kr2
            vv[r0:r0 + n, 0:C] = kv_ref[:, C:2 * C]
            vv[r0:r0 + n, C:2 * C] = kv_ref[:, 3 * C:4 * C]

    qr = qr_ref[...].astype(F32)
    lane = lax.broadcasted_iota(jnp.int32, qr.shape, 1)
    scale = (NOPE_C + ROPE_C) ** -0.5
    for hh, k_ref in enumerate((k0, k1)):
        keep = (lane < ROPE_C) if hh == 0 else (lane >= ROPE_C)
        qf = jnp.concatenate([qn_ref[:, hh * C:(hh + 1) * C], jnp.where(keep, qr, 0.0).astype(BF)], axis=1)
        s = lax.dot_general(qf, k_ref[...], (((1,), (1,)), ((), ())), preferred_element_type=F32) * scale
        p = _softmax_rows(s)
        o = jnp.dot(p.astype(BF), vv[:, hh * C:(hh + 1) * C], preferred_element_type=F32)
        o_ref[:, hh * C:(hh + 1) * C] = o.astype(o_ref.dtype)


def _mla_attn(qn, qr, qr_row0, kv_lat, kr_lat, kr_row0, row0, cache, nb, S):
    C = LANES
    tq = _pick(S, 256, SUBLANES)
    nq = S // tq
    P = cache[0].shape[0] // nb if cache is not None else 0
    r_q, r_qr, r_kv, r_kr = row0 // tq, qr_row0 // tq, row0 // S, kr_row0 // S
    in_specs = [pl.BlockSpec((tq, 2 * C), lambda b, h, i: (r_q + b * nq + i, h)),
                pl.BlockSpec((tq, C), lambda b, h, i: (r_qr + b * nq + i, h))]
    args = [qn, qr]
    if P:
        in_specs += [pl.BlockSpec((P, 4 * C), lambda b, h, i: (b, h)),
                     pl.BlockSpec((P, C), lambda b, h, i: (b, 0))]
        args += [cache[0], cache[1]]
    in_specs += [pl.BlockSpec((S, 4 * C), lambda b, h, i: (r_kv + b, h)),
                 pl.BlockSpec((S, C), lambda b, h, i: (r_kr + b, 0))]
    args += [kv_lat, kr_lat]
    return pl.pallas_call(
        functools.partial(_mla_kernel, P=P, S=S, tq=tq),
        grid=(nb, H_C // 2, nq),
        in_specs=in_specs,
        out_specs=pl.BlockSpec((tq, 2 * C), lambda b, h, i: (b * nq + i, h)),
        out_shape=jax.ShapeDtypeStruct((nb * S, H_C * V_C), BF),
        scratch_shapes=[pltpu.VMEM((P + S, 2 * C), BF)] * 3,
        compiler_params=_cparams("parallel", "parallel", "arbitrary"),
        name="mla_attn",
    )(*args)


def kernel(x_prompt, x_sample, c, c_ctx, state_rglru, cache_dk, cache_dv, cache_ckv, cache_krope,
           mod_w, mod_b, norm_g, final_g, ffn_wg, ffn_wu, ffn_wd,
           ev_w_in, ev_conv_w, ev_conv_b, ev_wa, ev_ba, ev_wx, ev_bx, ev_lam, ev_lq, ev_lk,
           ev_subln_g, ev_w_out, od_w_in, od_qnorm_g, od_w_uq, od_kvnorm_g, od_w_ukv, od_w_out):
    assert NOPE_C == LANES and V_C == LANES and 2 * DK_B == LANES and 2 * ROPE_C == LANES
    NBP, SP, D = x_prompt.shape
    NB, DS, _ = x_sample.shape
    PAST = cache_dk.shape[2]
    L = mod_w.shape[0]
    MP, MS = NBP * SP, NB * DS
    tok = _Tokens(MP, DS, NB, 1 + NB)
    M = tok.m
    d_rnn = ev_lam.shape[-1]
    d_ff = ffn_wg.shape[-1]
    ql, kvl = od_qnorm_g.shape[-1], od_kvnorm_g.shape[-1]
    assert d_rnn // H_A == LANES

    n_cond = -(-(1 + NB) // SUBLANES) * SUBLANES
    cond = jnp.concatenate([c_ctx[None, :], c, jnp.zeros((n_cond - 1 - NB, D), F32)], axis=0)
    mod = _modulation(cond, mod_w, mod_b)
    modt = mod[:, :1 + NB].reshape(L * (1 + NB) * N_MOD, 1, D)
    normt = norm_g.reshape(L * 3, 1, D)

    fpad = -(-d_ff // FF_ALIGN) * FF_ALIGN - d_ff
    wg = jnp.pad(ffn_wg.astype(BF), ((0, 0), (0, 0), (0, 0), (0, fpad)))
    wu = jnp.pad(ffn_wu.astype(BF), ((0, 0), (0, 0), (0, 0), (0, fpad)))
    wd = jnp.pad(ffn_wd.astype(BF), ((0, 0), (0, 0), (0, fpad), (0, 0)))

    x = jnp.concatenate([x_prompt.reshape(MP, D), x_sample.reshape(MS, D)], axis=0)
    cos, sin = _rope_tables(DS)
    tm = tok.tile(1024)
    new = {}

    for l in range(L):
        x = _ffn(x, normt, modt, tok, l, 0, wg[l, 0], wu[l, 0], wd[l, 0])
        h = _prenorm(x, normt, modt, tok, l, 1)
        if l % 2 == 0:
            e = l // 2
            lam_init = 0.8 - 0.6 * math.exp(-0.3 * l)
            z = _mm(h, ev_w_in[e].astype(BF), F32, tm=tm, name="ev_in")
            q0, k0_, v0 = 2 * d_rnn, 2 * d_rnn + H_B * LANES, 2 * d_rnn + 2 * H_B * LANES
            wcat = jnp.concatenate([ev_wa[e, 0], ev_wx[e, 0], ev_wa[e, 1], ev_wx[e, 1]], axis=-1).astype(BF)
            bcat = jnp.concatenate([t.reshape(H_A, 1, LANES) for t in
                                    (ev_ba[e, 0], ev_bx[e, 0], ev_ba[e, 1], ev_bx[e, 1])], axis=-1)
            rg = functools.partial(_rglru, z, d_rnn=d_rnn, conv_w=ev_conv_w[e], conv_b=ev_conv_b[e],
                                   wcat=wcat, bcat=bcat, lam=ev_lam[e])
            ya_p, hfin = rg(row0=0, nb=NBP, S=SP, h0=jnp.zeros((NBP, 2, d_rnn), F32))
            ya_s, _ = rg(row0=MP, nb=NB, S=DS, h0=state_rglru[:, e])
            o_p = _dattn(z, 0, q0, z, 0, k0_, z, 0, v0, None, NBP, SP,
                         ev_lq[e], ev_lk[e], ev_subln_g[e], lam_init)
            qk = _rope(z, cos, sin, MP, MS, q0, 2 * H_B * LANES, DS)
            cache = (cache_dk[:, e].reshape(NB, PAST, H_B * LANES), cache_dv[:, e].reshape(NB, PAST, H_B * LANES))
            o_s = _dattn(qk, 0, 0, qk, 0, H_B * LANES, z, MP, v0, cache, NB, DS,
                         ev_lq[e], ev_lk[e], ev_subln_g[e], lam_init)
            y_in = jnp.concatenate([jnp.concatenate([ya_p, o_p], axis=1),
                                    jnp.concatenate([ya_s, o_s], axis=1)], axis=0)
            w_out = ev_w_out[e].astype(BF)
            new.setdefault('rec', []).append(hfin)
            new.setdefault('dk', []).append(z[:MP, k0_:v0].reshape(NBP, SP, H_B, 2 * DK_B))
            new.setdefault('dv', []).append(z[:MP, v0:].reshape(NBP, SP, H_B, 2 * DK_B))
        else:
            o = l // 2
            w_in = jnp.pad(od_w_in[o].astype(BF), ((0, 0), (0, LANES - ROPE_C)))
            cqn, ckvn, kr = _mla_in(h, w_in, od_qnorm_g[o], od_kvnorm_g[o], ql, kvl, tok.tile(512))
            w_uq = od_w_uq[o].astype(BF).reshape(ql, H_C, NOPE_C + ROPE_C)
            qn = _mm(cqn, w_uq[:, :, :NOPE_C].reshape(ql, H_C * NOPE_C), BF, tm=tm, name="uq_nope")
            qr = _mm(cqn, w_uq[:, :, NOPE_C:].reshape(ql, H_C * ROPE_C), F32, tm=tm, name="uq_rope")
            w_ukv = od_w_ukv[o].astype(BF)
            kv = _mm(ckvn, w_ukv, BF, tm=tm, name="ukv")
            o_p = _mla_attn(qn, qr, 0, kv, kr, 0, 0, None, NBP, SP)
            qr_s = _rope(qr, cos, sin, MP, MS, 0, H_C * ROPE_C, DS)
            kr_s = _rope(kr, cos, sin, MP, MS, 0, LANES, DS)
            ckv_c = cache_ckv[:, o].reshape(NB * PAST, kvl)
            kv_c = _mm(ckv_c, w_ukv, BF, tm=_pick(NB * PAST, 1024, SUBLANES), name="ukv_ctx")
            kr_c = jnp.pad(cache_krope[:, o].reshape(NB * PAST, ROPE_C), ((0, 0), (0, LANES - ROPE_C)))
            o_s = _mla_attn(qn, qr_s, 0, kv, kr_s, 0, MP, (kv_c, kr_c), NB, DS)
            y_in = jnp.concatenate([o_p, o_s], axis=0)
            w_out = od_w_out[o].astype(BF)
            new.setdefault('ckv', []).append(ckvn[:MP].reshape(NBP, SP, kvl))
            new.setdefault('kr', []).append(kr[:MP, :ROPE_C].reshape(NBP, SP, ROPE_C))
        x = _mm_residual(y_in, w_out, x, modt, tok, l, 5, 1.0, tm=tm, tn=_pick(D, 1024, LANES), name="mix_out")
        x = _ffn(x, normt, modt, tok, l, 2, wg[l, 1], wu[l, 1], wd[l, 1])

    y_prompt = _final_norm(x, final_g, 0, MP).reshape(NBP, SP, D)
    y_sample = _final_norm(x, final_g, MP, MS).reshape(NB, DS, D)
    return (y_prompt, y_sample, jnp.stack(new['rec'], axis=1), jnp.stack(new['dk'], axis=1),
            jnp.stack(new['dv'], axis=1), jnp.stack(new['ckv'], axis=1), jnp.stack(new['kr'], axis=1))
```

```python
import functools
import math

import jax
import jax.numpy as jnp
from jax import lax
from jax.experimental import pallas as pl
from jax.experimental.pallas import tpu as pltpu

BF = jnp.bfloat16
F32 = jnp.float32

GRID_W = 64
EPS = 1e-6
ROPE_THETA = 10000.0
N_MOD = 9
H_A = 16
CONV_W = 4
LRU_C = 8.0
H_B = 16
DK_B = 64
H_C = 32
NOPE_C = 128
ROPE_C = 64
V_C = 128

LANES = 128
SUBLANES = 8
VMEM_LIMIT = 56 * 1024 * 1024
FF_ALIGN = 1024
LOG2E = math.log2(math.e)
ATT_ROWS = 256
DATT_ROWS = 128


def _cparams(*sem):
    return pltpu.CompilerParams(dimension_semantics=sem, vmem_limit_bytes=VMEM_LIMIT)


def _pick(n, pref, mult):
    best = None
    d = mult
    while d <= min(n, pref):
        if n % d == 0:
            best = d
        d += mult
    return n if best is None else best


def _sigmoid(x):
    return 1.0 / (1.0 + jnp.exp(-x))


def _rms(x):
    return x * lax.rsqrt(jnp.mean(x * x, axis=-1, keepdims=True) + EPS)


def _mod_kernel(c_ref, w_ref, b_ref, o_ref):
    k = pl.program_id(2)
    c = c_ref[...]
    s = (c * _sigmoid(c)).astype(BF)
    part = jnp.dot(s, w_ref[...].astype(BF), preferred_element_type=F32)

    @pl.when(k == 0)
    def _():
        o_ref[...] = part + b_ref[...]

    @pl.when(k > 0)
    def _():
        o_ref[...] += part


def _modulation(cond, mod_w, mod_b):
    R, D = cond.shape
    L, _, N = mod_w.shape
    tn = _pick(N, 2048, LANES)
    tk = _pick(D, 1024, LANES)
    return pl.pallas_call(
        _mod_kernel,
        grid=(L, N // tn, D // tk),
        in_specs=[
            pl.BlockSpec((R, tk), lambda l, n, k: (0, k)),
            pl.BlockSpec((None, tk, tn), lambda l, n, k: (l, k, n)),
            pl.BlockSpec((None, 1, tn), lambda l, n, k: (l, 0, n)),
        ],
        out_specs=pl.BlockSpec((None, R, tn), lambda l, n, k: (l, 0, n)),
        out_shape=jax.ShapeDtypeStruct((L, R, N), F32),
        compiler_params=_cparams("parallel", "parallel", "arbitrary"),
        name="modulation",
    )(cond, mod_w, mod_b.reshape(L, 1, N))


class _Tokens:
    def __init__(self, mp, ds, nb, n_rows):
        self.mp, self.ds, self.nb, self.n_rows = mp, ds, nb, n_rows
        self.m = mp + ds * nb

    def tile(self, pref):
        return _pick(math.gcd(self.mp, self.ds), pref, SUBLANES)

    def mod_index(self, layer, chunk, tm):
        mp, ds, n_rows = self.mp, self.ds, self.n_rows

        def f(i):
            r = jnp.where(i * tm < mp, 0, 1 + (i * tm - mp) // ds)
            return (layer * n_rows + r) * N_MOD + chunk
        return f


def _prenorm_kernel(x_ref, g_ref, sh_ref, sc_ref, o_ref):
    y = _rms(x_ref[...]) * g_ref[...]
    o_ref[...] = (y * (1.0 + sc_ref[...]) + sh_ref[...]).astype(o_ref.dtype)


def _prenorm(x, normt, modt, tok, layer, sub):
    M, D = x.shape
    tm = tok.tile(256)
    sh = tok.mod_index(layer, 3 * sub, tm)
    sc = tok.mod_index(layer, 3 * sub + 1, tm)
    return pl.pallas_call(
        _prenorm_kernel,
        grid=(M // tm,),
        in_specs=[
            pl.BlockSpec((tm, D), lambda i: (i, 0)),
            pl.BlockSpec((None, 1, D), lambda i: (layer * 3 + sub, 0, 0)),
            pl.BlockSpec((None, 1, D), lambda i: (sh(i), 0, 0)),
            pl.BlockSpec((None, 1, D), lambda i: (sc(i), 0, 0)),
        ],
        out_specs=pl.BlockSpec((tm, D), lambda i: (i, 0)),
        out_shape=jax.ShapeDtypeStruct((M, D), BF),
        compiler_params=_cparams("parallel"),
        name="prenorm",
    )(x, normt, modt, modt)


def _final_norm_kernel(x_ref, g_ref, o_ref):
    o_ref[...] = _rms(x_ref[...]) * g_ref[...]


def _final_norm(x, g, row0, rows):
    D = x.shape[1]
    tm = _pick(math.gcd(row0, rows) if row0 else rows, 256, SUBLANES)
    off = row0 // tm
    return pl.pallas_call(
        _final_norm_kernel,
        grid=(rows // tm,),
        in_specs=[
            pl.BlockSpec((tm, D), lambda i: (off + i, 0)),
            pl.BlockSpec((1, D), lambda i: (0, 0)),
        ],
        out_specs=pl.BlockSpec((tm, D), lambda i: (i, 0)),
        out_shape=jax.ShapeDtypeStruct((rows, D), F32),
        compiler_params=_cparams("parallel"),
        name="final_norm",
    )(x, g.reshape(1, D))


def _mm_kernel(*refs, nk, residual, coef):
    if residual:
        a_ref, w_ref, r_ref, g_ref, o_ref = refs[:5]
    else:
        a_ref, w_ref, o_ref = refs[:3]
    acc_ref = refs[-1] if nk > 1 else None

    def finish(acc):
        if residual:
            o_ref[...] = r_ref[...] + (coef * g_ref[...]) * acc
        else:
            o_ref[...] = acc.astype(o_ref.dtype)

    if nk == 1:
        finish(jnp.dot(a_ref[...].astype(BF), w_ref[...], preferred_element_type=F32))
        return
    k = pl.program_id(2)

    @pl.when(k == 0)
    def _():
        acc_ref[...] = jnp.zeros_like(acc_ref)

    acc_ref[...] += jnp.dot(a_ref[...].astype(BF), w_ref[...], preferred_element_type=F32)

    @pl.when(k == nk - 1)
    def _():
        finish(acc_ref[...])


def _mm(a, w, out_dtype, *, tm, tn=None, tk=None, row0=0, rows=None, name="mm"):
    K, N = w.shape
    tn = _pick(N, 1024, LANES) if tn is None else tn
    rows = a.shape[0] if rows is None else rows
    tk = K if tk is None else tk
    nk = K // tk
    off = row0 // tm
    return pl.pallas_call(
        functools.partial(_mm_kernel, nk=nk, residual=False, coef=None),
        grid=(rows // tm, N // tn, nk),
        in_specs=[
            pl.BlockSpec((tm, tk), lambda i, j, k: (off + i, k)),
            pl.BlockSpec((tk, tn), lambda i, j, k: (k, j)),
        ],
        out_specs=pl.BlockSpec((tm, tn), lambda i, j, k: (i, j)),
        out_shape=jax.ShapeDtypeStruct((rows, N), out_dtype),
        scratch_shapes=[pltpu.VMEM((tm, tn), F32)] if nk > 1 else [],
        compiler_params=_cparams("parallel", "parallel", "arbitrary"),
        name=name,
    )(a, w)


def _mm_residual(a, w, x, modt, tok, layer, chunk, coef, *, tm, tn, tk=None, name="mm_res"):
    K, N = w.shape
    M = a.shape[0]
    tk = K if tk is None else tk
    nk = K // tk
    gate = tok.mod_index(layer, chunk, tm)
    return pl.pallas_call(
        functools.partial(_mm_kernel, nk=nk, residual=True, coef=coef),
        grid=(M // tm, N // tn, nk),
        in_specs=[
            pl.BlockSpec((tm, tk), lambda i, j, k: (i, k)),
            pl.BlockSpec((tk, tn), lambda i, j, k: (k, j)),
            pl.BlockSpec((tm, tn), lambda i, j, k: (i, j)),
            pl.BlockSpec((None, 1, tn), lambda i, j, k: (gate(i), 0, j)),
        ],
        out_specs=pl.BlockSpec((tm, tn), lambda i, j, k: (i, j)),
        out_shape=jax.ShapeDtypeStruct((M, N), F32),
        scratch_shapes=[pltpu.VMEM((tm, tn), F32)] if nk > 1 else [],
        compiler_params=_cparams("parallel", "parallel", "arbitrary"),
        name=name,
    )(a, w, x, modt)


def _ffn_up_kernel(h_ref, wg_ref, wu_ref, o_ref):
    h = h_ref[...]
    g = jnp.dot(h, wg_ref[...], preferred_element_type=F32)
    u = jnp.dot(h, wu_ref[...], preferred_element_type=F32)
    o_ref[...] = ((g * _sigmoid(g)) * u).astype(o_ref.dtype)


def _ffn_up(h, wg, wu, *, tm, tf):
    M, D = h.shape
    FP = wg.shape[1]
    return pl.pallas_call(
        _ffn_up_kernel,
        grid=(M // tm, FP // tf),
        in_specs=[
            pl.BlockSpec((tm, D), lambda i, j: (i, 0)),
            pl.BlockSpec((D, tf), lambda i, j: (0, j)),
            pl.BlockSpec((D, tf), lambda i, j: (0, j)),
        ],
        out_specs=pl.BlockSpec((tm, tf), lambda i, j: (i, j)),
        out_shape=jax.ShapeDtypeStruct((M, FP), BF),
        compiler_params=_cparams("parallel", "parallel"),
        name="ffn_up",
    )(h, wg, wu)


def _ffn(x, normt, modt, tok, layer, sub, wg, wu, wd):
    h = _prenorm(x, normt, modt, tok, layer, sub)
    FP = wg.shape[1]
    tm = tok.tile(1024)
    a = _ffn_up(h, wg, wu, tm=tm, tf=_pick(FP, 512, LANES))
    tk = _pick(FP, max(FP // 4, LANES), LANES)
    return _mm_residual(a, wd, x, modt, tok, layer, 3 * sub + 2, 0.5,
                        tm=tm, tn=_pick(x.shape[1], 1024, LANES), tk=tk, name="ffn_down")


def _gelu_tanh(x):
    return x * (0.5 * (1.0 + jnp.tanh(math.sqrt(2.0 / math.pi) * (x + 0.044715 * (x * x * x)))))


def _softplus(x):
    return jnp.maximum(x, 0.0) + jnp.log1p(jnp.exp(-jnp.abs(x)))


def _rglru_kernel(xa_ref, ga_ref, cw_ref, cb_ref, w_ref, b_ref, lam_ref, h0_ref,
                  y_ref, hfin_ref, pad_ref, af_ref, bf_ref, ab_ref, bb_ref, *, S):
    C = LANES
    P0 = SUBLANES
    pad_ref[0:P0, :] = jnp.zeros((P0, C), F32)
    pad_ref[P0 + S:P0 + S + P0, :] = jnp.zeros((P0, C), F32)
    pad_ref[P0:P0 + S, :] = xa_ref[...]
    cw = cw_ref[...]
    xc = jnp.broadcast_to(cb_ref[...], (S, C))
    for j in range(CONV_W):
        xc = xc + pad_ref[pl.ds(P0 - 2 + j, S), :] * cw[j:j + 1, :]

    gates = jnp.dot(xc.astype(BF), w_ref[...], preferred_element_type=F32) + b_ref[...]
    lam = lam_ref[...]
    for d, (a_ref, b_ref_) in enumerate(((af_ref, bf_ref), (ab_ref, bb_ref))):
        r = _sigmoid(gates[:, (2 * d) * C:(2 * d + 1) * C])
        i = _sigmoid(gates[:, (2 * d + 1) * C:(2 * d + 2) * C])
        log_a = (-LRU_C * r) * _softplus(-lam[d:d + 1, :])
        a_ref[...] = jnp.exp(log_a)
        th = jnp.tanh(log_a)
        one_minus_a2 = (-2.0 * th) / (1.0 - th)
        b_ref_[...] = (jnp.sqrt(one_minus_a2) * i) * xc

    rows = lax.broadcasted_iota(jnp.int32, (SUBLANES, C), 0)

    def tile_scan(a, b, down):
        for k in (1, 2, 4):
            if down:
                keep = rows >= k
                shift = k
            else:
                keep = rows < SUBLANES - k
                shift = SUBLANES - k
            a1 = jnp.where(keep, pltpu.roll(a, shift, 0), 1.0)
            b1 = jnp.where(keep, pltpu.roll(b, shift, 0), 0.0)
            b = a * b1 + b
            a = a * a1
        return a, b

    nt = S // SUBLANES

    def body(t, carry):
        hf, hb = carry
        r0 = pl.multiple_of(t * SUBLANES, SUBLANES)
        a, b = tile_scan(af_ref[pl.ds(r0, SUBLANES), :], bf_ref[pl.ds(r0, SUBLANES), :], True)
        h = a * hf + b
        bf_ref[pl.ds(r0, SUBLANES), :] = h
        hf = jnp.broadcast_to(h[SUBLANES - 1:SUBLANES, :], (SUBLANES, C))
        r1 = pl.multiple_of((nt - 1 - t) * SUBLANES, SUBLANES)
        a, b = tile_scan(ab_ref[pl.ds(r1, SUBLANES), :], bb_ref[pl.ds(r1, SUBLANES), :], False)
        g = a * hb + b
        bb_ref[pl.ds(r1, SUBLANES), :] = g
        hb = jnp.broadcast_to(g[0:1, :], (SUBLANES, C))
        return hf, hb

    h0 = h0_ref[...]
    hf, hb = lax.fori_loop(0, nt, body, (jnp.broadcast_to(h0[0:1, :], (SUBLANES, C)),
                                         jnp.broadcast_to(h0[1:2, :], (SUBLANES, C))))
    hfin_ref[0:1, :] = hf[0:1, :]
    hfin_ref[1:2, :] = hb[0:1, :]
    y_ref[...] = (_gelu_tanh(ga_ref[...]) * (bf_ref[...] + bb_ref[...])).astype(y_ref.dtype)


def _rglru(z, row0, nb, S, d_rnn, conv_w, conv_b, wcat, bcat, lam, h0):
    C = LANES
    ncb = d_rnn // C
    roff = row0 // S
    return pl.pallas_call(
        functools.partial(_rglru_kernel, S=S),
        grid=(nb, ncb),
        in_specs=[
            pl.BlockSpec((S, C), lambda b, c: (roff + b, c)),
            pl.BlockSpec((S, C), lambda b, c: (roff + b, ncb + c)),
            pl.BlockSpec((CONV_W, C), lambda b, c: (0, c)),
            pl.BlockSpec((1, C), lambda b, c: (0, c)),
            pl.BlockSpec((None, C, 4 * C), lambda b, c: (c, 0, 0)),
            pl.BlockSpec((None, 1, 4 * C), lambda b, c: (c, 0, 0)),
            pl.BlockSpec((2, C), lambda b, c: (0, c)),
            pl.BlockSpec((None, 2, C), lambda b, c: (b, 0, c)),
        ],
        out_specs=[
            pl.BlockSpec((S, C), lambda b, c: (b, c)),
            pl.BlockSpec((None, 2, C), lambda b, c: (b, 0, c)),
        ],
        out_shape=[
            jax.ShapeDtypeStruct((nb * S, d_rnn), BF),
            jax.ShapeDtypeStruct((nb, 2, d_rnn), F32),
        ],
        scratch_shapes=[pltpu.VMEM((S + 2 * SUBLANES, C), F32)] + [pltpu.VMEM((S, C), F32)] * 4,
        compiler_params=_cparams("parallel", "parallel"),
        name="rglru",
    )(z, z, conv_w, conv_b.reshape(1, d_rnn), wcat, bcat, lam, h0)


def _rope_tables(n_tok):
    t = jnp.arange(n_tok, dtype=jnp.int32)
    row, col = t // GRID_W, t % GRID_W
    n = 16
    inv = ROPE_THETA ** (-jnp.arange(n, dtype=F32) / n)
    ang_r = row.astype(F32)[:, None] * inv
    ang_c = col.astype(F32)[:, None] * inv

    def grp(ang):
        c, s = jnp.cos(ang), jnp.sin(ang)
        return jnp.concatenate([c, c], -1), jnp.concatenate([-s, s], -1)
    cr, sr = grp(ang_r)
    cc, sc = grp(ang_c)
    return jnp.concatenate([cr, cc, cr, cc], -1), jnp.concatenate([sr, sc, sr, sc], -1)


def _rope_kernel(x_ref, cos_ref, sin_ref, o_ref):
    cos = cos_ref[...]
    sin = sin_ref[...]
    first = (lax.broadcasted_iota(jnp.int32, cos.shape, 1) % 32) < 16
    for c in range(x_ref.shape[1] // LANES):
        x = x_ref[:, c * LANES:(c + 1) * LANES].astype(F32)
        partner = jnp.where(first, pltpu.roll(x, LANES - 16, 1), pltpu.roll(x, 16, 1))
        o_ref[:, c * LANES:(c + 1) * LANES] = (x * cos + partner * sin).astype(o_ref.dtype)


def _rope(x, cos, sin, row0, rows, col0, width, ds):
    tm = _pick(ds, 256, SUBLANES)
    tw = _pick(width, 1024, LANES)
    roff, coff, per = row0 // tm, col0 // tw, ds // tm
    return pl.pallas_call(
        _rope_kernel,
        grid=(rows // tm, width // tw),
        in_specs=[
            pl.BlockSpec((tm, tw), lambda i, j: (roff + i, coff + j)),
            pl.BlockSpec((tm, LANES), lambda i, j: (i % per, 0)),
            pl.BlockSpec((tm, LANES), lambda i, j: (i % per, 0)),
        ],
        out_specs=pl.BlockSpec((tm, tw), lambda i, j: (i, j)),
        out_shape=jax.ShapeDtypeStruct((rows, width), BF),
        compiler_params=_cparams("parallel", "parallel"),
        name="rope",
    )(x, cos, sin)


def _dattn_kernel(*refs, P, S, tq, lam_init):
    if P:
        q_ref, kc_ref, vc_ref, kl_ref, vl_ref, lq_ref, lk_ref, g_ref, o_ref, kk, vv = refs
    else:
        q_ref, kl_ref, vl_ref, lq_ref, lk_ref, g_ref, o_ref, kk, vv = refs

    @pl.when(pl.program_id(2) == 0)
    def _():
        if P:
            kk[0:P, :] = kc_ref[...].astype(BF)
            vv[0:P, :] = vc_ref[...].astype(BF)
        kk[P:P + S, :] = kl_ref[...].astype(BF)
        vv[P:P + S, :] = vl_ref[...].astype(BF)

    el = jnp.exp(jnp.sum(lq_ref[...] * lk_ref[...], axis=-1, keepdims=True))
    lam = el[0:1, :] - el[1:2, :] + lam_init
    sub = min(DATT_ROWS, tq)
    chains = []
    for r0 in range(0, tq, sub):
        q = q_ref[r0:r0 + sub, :].astype(F32)
        lane = lax.broadcasted_iota(jnp.int32, q.shape, 1)
        qq = jnp.concatenate([jnp.where(lane < DK_B, q, 0.0), jnp.where(lane >= DK_B, q, 0.0)], axis=0).astype(BF)
        chains.append((r0, lax.dot_general(qq, kk[...], (((1,), (1,)), ((), ())), preferred_element_type=F32)))
    for r0, s in chains:
        e = jnp.exp2((s - jnp.max(s, axis=-1, keepdims=True)) * ((DK_B ** -0.5) * LOG2E))
        l = jnp.sum(e, axis=-1, keepdims=True)
        a = e[:sub, :] - e[sub:, :] * (lam * l[:sub, :] / l[sub:, :])
        o = jnp.dot(a.astype(BF), vv[...], preferred_element_type=F32) / l[:sub, :]
        o_ref[r0:r0 + sub, :] = ((_rms(o) * g_ref[...]) * (1.0 - lam_init)).astype(o_ref.dtype)


def _dattn(q, qrow0, qcol0, k_lat, krow0, kcol0, v_lat, vrow0, vcol0, cache, nb, S, lq, lk, g, lam_init):
    C = 2 * DK_B
    tq = _pick(S, 512, SUBLANES)
    nq = S // tq
    P = cache[0].shape[1] if cache is not None else 0
    qr, kr, vr = qrow0 // tq, krow0 // S, vrow0 // S
    qc, kc, vc = qcol0 // C, kcol0 // C, vcol0 // C
    in_specs = [pl.BlockSpec((tq, C), lambda b, h, i: (qr + b * nq + i, qc + h))]
    args = [q]
    if P:
        in_specs += [pl.BlockSpec((None, P, C), lambda b, h, i: (b, 0, h)),
                     pl.BlockSpec((None, P, C), lambda b, h, i: (b, 0, h))]
        args += [cache[0], cache[1]]
    in_specs += [
        pl.BlockSpec((S, C), lambda b, h, i: (kr + b, kc + h)),
        pl.BlockSpec((S, C), lambda b, h, i: (vr + b, vc + h)),
        pl.BlockSpec((2, DK_B), lambda b, h, i: (0, 0)),
        pl.BlockSpec((2, DK_B), lambda b, h, i: (0, 0)),
        pl.BlockSpec((1, C), lambda b, h, i: (0, 0)),
    ]
    args += [k_lat, v_lat, lq, lk, g.reshape(1, C)]
    return pl.pallas_call(
        functools.partial(_dattn_kernel, P=P, S=S, tq=tq, lam_init=lam_init),
        grid=(nb, H_B, nq),
        in_specs=in_specs,
        out_specs=pl.BlockSpec((tq, C), lambda b, h, i: (b * nq + i, h)),
        out_shape=jax.ShapeDtypeStruct((nb * S, H_B * C), BF),
        scratch_shapes=[pltpu.VMEM((P + S, C), BF), pltpu.VMEM((P + S, C), BF)],
        compiler_params=_cparams("parallel", "parallel", "arbitrary"),
        name="diff_attn",
    )(*args)


def _mla_in_kernel(h_ref, w_ref, gq_ref, gkv_ref, cq_ref, ckv_ref, kr_ref, *, ql, kvl):
    z = jnp.dot(h_ref[...], w_ref[...], preferred_element_type=F32)
    cq_ref[...] = (_rms(z[:, :ql]) * gq_ref[...]).astype(cq_ref.dtype)
    ckv_ref[...] = _rms(z[:, ql:ql + kvl]) * gkv_ref[...]
    kr_ref[...] = z[:, ql + kvl:]


def _mla_in(h, w, gq, gkv, ql, kvl, tm):
    M, D = h.shape
    N = w.shape[1]
    return pl.pallas_call(
        functools.partial(_mla_in_kernel, ql=ql, kvl=kvl),
        grid=(M // tm,),
        in_specs=[
            pl.BlockSpec((tm, D), lambda i: (i, 0)),
            pl.BlockSpec((D, N), lambda i: (0, 0)),
            pl.BlockSpec((1, ql), lambda i: (0, 0)),
            pl.BlockSpec((1, kvl), lambda i: (0, 0)),
        ],
        out_specs=[
            pl.BlockSpec((tm, ql), lambda i: (i, 0)),
            pl.BlockSpec((tm, kvl), lambda i: (i, 0)),
            pl.BlockSpec((tm, LANES), lambda i: (i, 0)),
        ],
        out_shape=[
            jax.ShapeDtypeStruct((M, ql), BF),
            jax.ShapeDtypeStruct((M, kvl), F32),
            jax.ShapeDtypeStruct((M, LANES), F32),
        ],
        compiler_params=_cparams("parallel"),
        name="mla_in",
    )(h, w, gq.reshape(1, ql), gkv.reshape(1, kvl))


def _mla_kernel(*refs, P, S, tq):
    if P:
        qn_ref, qr_ref, kvc_ref, krc_ref, kvl_ref, krl_ref, o_ref, k0, k1, v0, v1 = refs
        segs = ((0, P, kvc_ref, krc_ref), (P, S, kvl_ref, krl_ref))
    else:
        qn_ref, qr_ref, kvl_ref, krl_ref, o_ref, k0, k1, v0, v1 = refs
        segs = ((0, S, kvl_ref, krl_ref),)
    C = LANES

    @pl.when(pl.program_id(2) == 0)
    def _():
        for r0, n, kv_ref, kr_ref in segs:
            kr = kr_ref[...].astype(F32)
            kr2 = (kr + pltpu.roll(kr, ROPE_C, 1)).astype(BF)
            k0[r0:r0 + n, 0:C] = kv_ref[:, 0:C]
            k0[r0:r0 + n, C:2 * C] = kr2
            k1[r0:r0 + n, 0:C] = kv_ref[:, 2 * C:3 * C]
            k1[r0:r0 + n, C:2 * C] = kr2
            v0[r0:r0 + n, 0:C] = kv_ref[:, C:2 * C]
            v0[r0:r0 + n, C:2 * C] = jnp.ones((n, C), BF)
            v1[r0:r0 + n, 0:C] = kv_ref[:, 3 * C:4 * C]
            v1[r0:r0 + n, C:2 * C] = jnp.ones((n, C), BF)

    c = ((NOPE_C + ROPE_C) ** -0.5) * LOG2E
    chains = []
    sub = min(ATT_ROWS, tq)
    for r0 in range(0, tq, sub):
        rows = slice(r0, r0 + sub)
        qr = qr_ref[rows, :].astype(F32)
        lane = lax.broadcasted_iota(jnp.int32, qr.shape, 1)
        for hh, (k_ref, v_ref) in enumerate(((k0, v0), (k1, v1))):
            keep = (lane < ROPE_C) if hh == 0 else (lane >= ROPE_C)
            qf = jnp.concatenate([qn_ref[rows, hh * C:(hh + 1) * C], jnp.where(keep, qr, 0.0).astype(BF)], axis=1)
            s = lax.dot_general(qf, k_ref[...], (((1,), (1,)), ((), ())), preferred_element_type=F32)
            chains.append((rows, hh, v_ref, s))
    for rows, hh, v_ref, s in chains:
        e = jnp.exp2((s - jnp.max(s, axis=-1, keepdims=True)) * c).astype(BF)
        oe = jnp.dot(e, v_ref[...], preferred_element_type=F32)
        o_ref[rows, hh * C:(hh + 1) * C] = (oe[:, 0:C] / oe[:, C:2 * C]).astype(o_ref.dtype)


def _mla_attn(qn, qr, qr_row0, kv_lat, kr_lat, kr_row0, row0, cache, nb, S):
    C = LANES
    tq = _pick(S, 512, SUBLANES)
    nq = S // tq
    P = cache[0].shape[0] // nb if cache is not None else 0
    r_q, r_qr, r_kv, r_kr = row0 // tq, qr_row0 // tq, row0 // S, kr_row0 // S
    in_specs = [pl.BlockSpec((tq, 2 * C), lambda b, h, i: (r_q + b * nq + i, h)),
                pl.BlockSpec((tq, C), lambda b, h, i: (r_qr + b * nq + i, h))]
    args = [qn, qr]
    if P:
        in_specs += [pl.BlockSpec((P, 4 * C), lambda b, h, i: (b, h)),
                     pl.BlockSpec((P, C), lambda b, h, i: (b, 0))]
        args += [cache[0], cache[1]]
    in_specs += [pl.BlockSpec((S, 4 * C), lambda b, h, i: (r_kv + b, h)),
                 pl.BlockSpec((S, C), lambda b, h, i: (r_kr + b, 0))]
    args += [kv_lat, kr_lat]
    return pl.pallas_call(
        functools.partial(_mla_kernel, P=P, S=S, tq=tq),
        grid=(nb, H_C // 2, nq),
        in_specs=in_specs,
        out_specs=pl.BlockSpec((tq, 2 * C), lambda b, h, i: (b * nq + i, h)),
        out_shape=jax.ShapeDtypeStruct((nb * S, H_C * V_C), BF),
        scratch_shapes=[pltpu.VMEM((P + S, 2 * C), BF)] * 4,
        compiler_params=_cparams("parallel", "parallel", "arbitrary"),
        name="mla_attn",
    )(*args)


def kernel(x_prompt, x_sample, c, c_ctx, state_rglru, cache_dk, cache_dv, cache_ckv, cache_krope,
           mod_w, mod_b, norm_g, final_g, ffn_wg, ffn_wu, ffn_wd,
           ev_w_in, ev_conv_w, ev_conv_b, ev_wa, ev_ba, ev_wx, ev_bx, ev_lam, ev_lq, ev_lk,
           ev_subln_g, ev_w_out, od_w_in, od_qnorm_g, od_w_uq, od_kvnorm_g, od_w_ukv, od_w_out):
    assert NOPE_C == LANES and V_C == LANES and 2 * DK_B == LANES and 2 * ROPE_C == LANES
    NBP, SP, D = x_prompt.shape
    NB, DS, _ = x_sample.shape
    PAST = cache_dk.shape[2]
    L = mod_w.shape[0]
    MP, MS = NBP * SP, NB * DS
    tok = _Tokens(MP, DS, NB, 1 + NB)
    M = tok.m
    d_rnn = ev_lam.shape[-1]
    d_ff = ffn_wg.shape[-1]
    ql, kvl = od_qnorm_g.shape[-1], od_kvnorm_g.shape[-1]
    assert d_rnn // H_A == LANES

    n_cond = -(-(1 + NB) // SUBLANES) * SUBLANES
    cond = jnp.concatenate([c_ctx[None, :], c, jnp.zeros((n_cond - 1 - NB, D), F32)], axis=0)
    mod = _modulation(cond, mod_w, mod_b)
    modt = mod[:, :1 + NB].reshape(L * (1 + NB) * N_MOD, 1, D)
    normt = norm_g.reshape(L * 3, 1, D)

    fpad = -(-d_ff // FF_ALIGN) * FF_ALIGN - d_ff
    wg = jnp.pad(ffn_wg.astype(BF), ((0, 0), (0, 0), (0, 0), (0, fpad)))
    wu = jnp.pad(ffn_wu.astype(BF), ((0, 0), (0, 0), (0, 0), (0, fpad)))
    wd = jnp.pad(ffn_wd.astype(BF), ((0, 0), (0, 0), (0, fpad), (0, 0)))

    x = jnp.concatenate([x_prompt.reshape(MP, D), x_sample.reshape(MS, D)], axis=0)
    cos, sin = _rope_tables(DS)
    tm = tok.tile(1024)
    new = {}

    for l in range(L):
        x = _ffn(x, normt, modt, tok, l, 0, wg[l, 0], wu[l, 0], wd[l, 0])
        h = _prenorm(x, normt, modt, tok, l, 1)
        if l % 2 == 0:
            e = l // 2
            lam_init = 0.8 - 0.6 * math.exp(-0.3 * l)
            z = _mm(h, ev_w_in[e].astype(BF), F32, tm=tm, name="ev_in")
            q0, k0_, v0 = 2 * d_rnn, 2 * d_rnn + H_B * LANES, 2 * d_rnn + 2 * H_B * LANES
            wcat = jnp.concatenate([ev_wa[e, 0], ev_wx[e, 0], ev_wa[e, 1], ev_wx[e, 1]], axis=-1).astype(BF)
            bcat = jnp.concatenate([t.reshape(H_A, 1, LANES) for t in
                                    (ev_ba[e, 0], ev_bx[e, 0], ev_ba[e, 1], ev_bx[e, 1])], axis=-1)
            rg = functools.partial(_rglru, z, d_rnn=d_rnn, conv_w=ev_conv_w[e], conv_b=ev_conv_b[e],
                                   wcat=wcat, bcat=bcat, lam=ev_lam[e])
            ya_p, hfin = rg(row0=0, nb=NBP, S=SP, h0=jnp.zeros((NBP, 2, d_rnn), F32))
            ya_s, _ = rg(row0=MP, nb=NB, S=DS, h0=state_rglru[:, e])
            o_p = _dattn(z, 0, q0, z, 0, k0_, z, 0, v0, None, NBP, SP,
                         ev_lq[e], ev_lk[e], ev_subln_g[e], lam_init)
            qk = _rope(z, cos, sin, MP, MS, q0, 2 * H_B * LANES, DS)
            cache = (cache_dk[:, e].reshape(NB, PAST, H_B * LANES), cache_dv[:, e].reshape(NB, PAST, H_B * LANES))
            o_s = _dattn(qk, 0, 0, qk, 0, H_B * LANES, z, MP, v0, cache, NB, DS,
                         ev_lq[e], ev_lk[e], ev_subln_g[e], lam_init)
            y_in = jnp.concatenate([jnp.concatenate([ya_p, o_p], axis=1),
                                    jnp.concatenate([ya_s, o_s], axis=1)], axis=0)
            w_out = ev_w_out[e].astype(BF)
            new.setdefault('rec', []).append(hfin)
            new.setdefault('dk', []).append(z[:MP, k0_:v0].reshape(NBP, SP, H_B, 2 * DK_B))
            new.setdefault('dv', []).append(z[:MP, v0:].reshape(NBP, SP, H_B, 2 * DK_B))
        else:
            o = l // 2
            w_in = jnp.pad(od_w_in[o].astype(BF), ((0, 0), (0, LANES - ROPE_C)))
            cqn, ckvn, kr = _mla_in(h, w_in, od_qnorm_g[o], od_kvnorm_g[o], ql, kvl, tok.tile(512))
            w_uq = od_w_uq[o].astype(BF).reshape(ql, H_C, NOPE_C + ROPE_C)
            qn = _mm(cqn, w_uq[:, :, :NOPE_C].reshape(ql, H_C * NOPE_C), BF, tm=tm, name="uq_nope")
            qr = _mm(cqn, w_uq[:, :, NOPE_C:].reshape(ql, H_C * ROPE_C), F32, tm=tm, name="uq_rope")
            w_ukv = od_w_ukv[o].astype(BF)
            kv = _mm(ckvn, w_ukv, BF, tm=tm, name="ukv")
            o_p = _mla_attn(qn, qr, 0, kv, kr, 0, 0, None, NBP, SP)
            qr_s = _rope(qr, cos, sin, MP, MS, 0, H_C * ROPE_C, DS)
            kr_s = _rope(kr, cos, sin, MP, MS, 0, LANES, DS)
            ckv_c = cache_ckv[:, o].reshape(NB * PAST, kvl)
            kv_c = _mm(ckv_c, w_ukv, BF, tm=_pick(NB * PAST, 1024, SUBLANES), name="ukv_ctx")
            kr_c = jnp.pad(cache_krope[:, o].reshape(NB * PAST, ROPE_C), ((0, 0), (0, LANES - ROPE_C)))
            o_s = _mla_attn(qn, qr_s, 0, kv, kr_s, 0, MP, (kv_c, kr_c), NB, DS)
            y_in = jnp.concatenate([o_p, o_s], axis=0)
            w_out = od_w_out[o].astype(BF)
            new.setdefault('ckv', []).append(ckvn[:MP].reshape(NBP, SP, kvl))
            new.setdefault('kr', []).append(kr[:MP, :ROPE_C].reshape(NBP, SP, ROPE_C))
        x = _mm_residual(y_in, w_out, x, modt, tok, l, 5, 1.0, tm=tm, tn=_pick(D, 1024, LANES), name="mix_out")
        x = _ffn(x, normt, modt, tok, l, 2, wg[l, 1], wu[l, 1], wd[l, 1])

    y_prompt = _final_norm(x, final_g, 0, MP).reshape(NBP, SP, D)
    y_sample = _final_norm(x, final_g, MP, MS).reshape(NB, DS, D)
    return (y_prompt, y_sample, jnp.stack(new['rec'], axis=1), jnp.stack(new['dk'], axis=1),
            jnp.stack(new['dv'], axis=1), jnp.stack(new['ckv'], axis=1), jnp.stack(new['kr'], axis=1))
```

```python
import functools
import math

import jax
import jax.numpy as jnp
from jax import lax
from jax.experimental import pallas as pl
from jax.experimental.pallas import tpu as pltpu

BF = jnp.bfloat16
F32 = jnp.float32

GRID_W = 64
EPS = 1e-6
ROPE_THETA = 10000.0
N_MOD = 9
H_A = 16
CONV_W = 4
LRU_C = 8.0
H_B = 16
DK_B = 64
H_C = 32
NOPE_C = 128
ROPE_C = 64
V_C = 128

LANES = 128
SUBLANES = 8
VMEM_LIMIT = 56 * 1024 * 1024
FF_ALIGN = 1024
LOG2E = math.log2(math.e)
ATT_ROWS = 256
DATT_ROWS = 128
SHORT_SEQ_HEADS = 4
SCAN_UNROLL = 4


def _cparams(*sem):
    return pltpu.CompilerParams(dimension_semantics=sem, vmem_limit_bytes=VMEM_LIMIT)


def _pick(n, pref, mult):
    best = None
    d = mult
    while d <= min(n, pref):
        if n % d == 0:
            best = d
        d += mult
    return n if best is None else best


def _sigmoid(x):
    return 1.0 / (1.0 + jnp.exp2(x * (-LOG2E)))


def _rms(x):
    return x * lax.rsqrt(jnp.mean(x * x, axis=-1, keepdims=True) + EPS)


def _in_place(in_specs, args, out_buf):
    if out_buf is None:
        return {}
    in_specs.append(pl.BlockSpec(memory_space=pl.ANY))
    args.append(out_buf)
    return {len(args) - 1: 0}


def _mod_kernel(c_ref, w_ref, b_ref, o_ref):
    k = pl.program_id(2)
    c = c_ref[...]
    s = (c * _sigmoid(c)).astype(BF)
    part = jnp.dot(s, w_ref[...].astype(BF), preferred_element_type=F32)

    @pl.when(k == 0)
    def _():
        o_ref[...] = part + b_ref[...]

    @pl.when(k > 0)
    def _():
        o_ref[...] += part


def _modulation(cond, mod_w, mod_b):
    R, D = cond.shape
    L, _, N = mod_w.shape
    tn = _pick(N, 2048, LANES)
    tk = _pick(D, 1024, LANES)
    return pl.pallas_call(
        _mod_kernel,
        grid=(L, N // tn, D // tk),
        in_specs=[
            pl.BlockSpec((R, tk), lambda l, n, k: (0, k)),
            pl.BlockSpec((None, tk, tn), lambda l, n, k: (l, k, n)),
            pl.BlockSpec((None, 1, tn), lambda l, n, k: (l, 0, n)),
        ],
        out_specs=pl.BlockSpec((None, R, tn), lambda l, n, k: (l, 0, n)),
        out_shape=jax.ShapeDtypeStruct((L, R, N), F32),
        compiler_params=_cparams("parallel", "parallel", "arbitrary"),
        name="modulation",
    )(cond, mod_w, mod_b.reshape(L, 1, N))


class _Tokens:
    def __init__(self, mp, ds, nb, n_rows):
        self.mp, self.ds, self.nb, self.n_rows = mp, ds, nb, n_rows
        self.m = mp + ds * nb

    def tile(self, pref):
        return _pick(math.gcd(self.mp, self.ds), pref, SUBLANES)

    def mod_index(self, layer, chunk, tm):
        mp, ds, n_rows = self.mp, self.ds, self.n_rows

        def f(i):
            r = jnp.where(i * tm < mp, 0, 1 + (i * tm - mp) // ds)
            return (layer * n_rows + r) * N_MOD + chunk
        return f


def _prenorm_kernel(x_ref, g_ref, sh_ref, sc_ref, *rest):
    o_ref = rest[-1]
    y = _rms(x_ref[...]) * g_ref[...]
    o_ref[...] = (y * (1.0 + sc_ref[...]) + sh_ref[...]).astype(o_ref.dtype)


def _prenorm(parts, normt, modt, tok, layer, sub):
    h = None
    for x, row0 in parts:
        rows, D = x.shape
        tm = tok.tile(256)
        off = row0 // tm
        sh = tok.mod_index(layer, 3 * sub, tm)
        sc = tok.mod_index(layer, 3 * sub + 1, tm)
        in_specs = [
            pl.BlockSpec((tm, D), lambda i: (i, 0)),
            pl.BlockSpec((None, 1, D), lambda i: (layer * 3 + sub, 0, 0)),
            pl.BlockSpec((None, 1, D), lambda i: (sh(off + i), 0, 0)),
            pl.BlockSpec((None, 1, D), lambda i: (sc(off + i), 0, 0)),
        ]
        args = [x, normt, modt, modt]
        aliases = _in_place(in_specs, args, h)
        h = pl.pallas_call(
            _prenorm_kernel,
            grid=(rows // tm,),
            in_specs=in_specs,
            out_specs=pl.BlockSpec((tm, D), lambda i: (off + i, 0)),
            out_shape=jax.ShapeDtypeStruct((tok.m, D), BF),
            input_output_aliases=aliases,
            compiler_params=_cparams("parallel"),
            name="prenorm",
        )(*args)
    return h


def _final_norm_kernel(x_ref, g_ref, o_ref):
    o_ref[...] = _rms(x_ref[...]) * g_ref[...]


def _final_norm(x, g, row0, rows):
    D = x.shape[1]
    tm = _pick(math.gcd(row0, rows) if row0 else rows, 256, SUBLANES)
    off = row0 // tm
    return pl.pallas_call(
        _final_norm_kernel,
        grid=(rows // tm,),
        in_specs=[
            pl.BlockSpec((tm, D), lambda i: (off + i, 0)),
            pl.BlockSpec((1, D), lambda i: (0, 0)),
        ],
        out_specs=pl.BlockSpec((tm, D), lambda i: (i, 0)),
        out_shape=jax.ShapeDtypeStruct((rows, D), F32),
        compiler_params=_cparams("parallel"),
        name="final_norm",
    )(x, g.reshape(1, D))


def _mm_kernel(*refs, nk, residual, coef):
    a_ref, w_ref = refs[:2]
    if residual:
        r_ref, g_ref = refs[2:4]
    o_ref, acc_ref = (refs[-2], refs[-1]) if nk > 1 else (refs[-1], None)

    def finish(acc):
        if residual:
            o_ref[...] = r_ref[...] + (coef * g_ref[...]) * acc
        else:
            o_ref[...] = acc.astype(o_ref.dtype)

    if nk == 1:
        finish(jnp.dot(a_ref[...].astype(BF), w_ref[...], preferred_element_type=F32))
        return
    k = pl.program_id(2)

    @pl.when(k == 0)
    def _():
        acc_ref[...] = jnp.zeros_like(acc_ref)

    acc_ref[...] += jnp.dot(a_ref[...].astype(BF), w_ref[...], preferred_element_type=F32)

    @pl.when(k == nk - 1)
    def _():
        finish(acc_ref[...])


def _mm(a, w, out_dtype, *, tm, tn=None, tk=None, row0=0, rows=None, name="mm"):
    K, N = w.shape
    tn = _pick(N, 1024, LANES) if tn is None else tn
    rows = a.shape[0] if rows is None else rows
    tk = K if tk is None else tk
    nk = K // tk
    off = row0 // tm
    return pl.pallas_call(
        functools.partial(_mm_kernel, nk=nk, residual=False, coef=None),
        grid=(rows // tm, N // tn, nk),
        in_specs=[
            pl.BlockSpec((tm, tk), lambda i, j, k: (off + i, k)),
            pl.BlockSpec((tk, tn), lambda i, j, k: (k, j)),
        ],
        out_specs=pl.BlockSpec((tm, tn), lambda i, j, k: (i, j)),
        out_shape=jax.ShapeDtypeStruct((rows, N), out_dtype),
        scratch_shapes=[pltpu.VMEM((tm, tn), F32)] if nk > 1 else [],
        compiler_params=_cparams("parallel", "parallel", "arbitrary"),
        name=name,
    )(a, w)


def _mm_residual(a, w, parts, modt, tok, layer, chunk, coef, *, tm, tn, tk=None, name="mm_res"):
    K, N = w.shape
    tk = K if tk is None else tk
    nk = K // tk
    gate = tok.mod_index(layer, chunk, tm)
    out = None
    for x, row0 in parts:
        off = row0 // tm
        in_specs = [
            pl.BlockSpec((tm, tk), lambda i, j, k: (off + i, k)),
            pl.BlockSpec((tk, tn), lambda i, j, k: (k, j)),
            pl.BlockSpec((tm, tn), lambda i, j, k: (i, j)),
            pl.BlockSpec((None, 1, tn), lambda i, j, k: (gate(off + i), 0, j)),
        ]
        args = [a, w, x, modt]
        aliases = _in_place(in_specs, args, out)
        out = pl.pallas_call(
            functools.partial(_mm_kernel, nk=nk, residual=True, coef=coef),
            grid=(x.shape[0] // tm, N // tn, nk),
            in_specs=in_specs,
            out_specs=pl.BlockSpec((tm, tn), lambda i, j, k: (off + i, j)),
            out_shape=jax.ShapeDtypeStruct((tok.m, N), F32),
            input_output_aliases=aliases,
            scratch_shapes=[pltpu.VMEM((tm, tn), F32)] if nk > 1 else [],
            compiler_params=_cparams("parallel", "parallel", "arbitrary"),
            name=name,
        )(*args)
    return out


def _ffn_up_kernel(h_ref, wg_ref, wu_ref, o_ref):
    h = h_ref[...]
    g = jnp.dot(h, wg_ref[...], preferred_element_type=F32)
    u = jnp.dot(h, wu_ref[...], preferred_element_type=F32)
    o_ref[...] = ((g * _sigmoid(g)) * u).astype(o_ref.dtype)


def _ffn_up(h, wg, wu, *, tm, tf):
    M, D = h.shape
    FP = wg.shape[1]
    return pl.pallas_call(
        _ffn_up_kernel,
        grid=(M // tm, FP // tf),
        in_specs=[
            pl.BlockSpec((tm, D), lambda i, j: (i, 0)),
            pl.BlockSpec((D, tf), lambda i, j: (0, j)),
            pl.BlockSpec((D, tf), lambda i, j: (0, j)),
        ],
        out_specs=pl.BlockSpec((tm, tf), lambda i, j: (i, j)),
        out_shape=jax.ShapeDtypeStruct((M, FP), BF),
        compiler_params=_cparams("parallel", "parallel"),
        name="ffn_up",
    )(h, wg, wu)


def _ffn(parts, normt, modt, tok, layer, sub, wg, wu, wd):
    h = _prenorm(parts, normt, modt, tok, layer, sub)
    FP = wg.shape[1]
    tm = tok.tile(1024)
    a = _ffn_up(h, wg, wu, tm=tm, tf=_pick(FP, 512, LANES))
    tk = _pick(FP, max(FP // 4, LANES), LANES)
    return _mm_residual(a, wd, parts, modt, tok, layer, 3 * sub + 2, 0.5,
                        tm=tm, tn=_pick(wd.shape[1], 1024, LANES), tk=tk, name="ffn_down")


def _gelu_tanh(x):
    return x * (0.5 * (1.0 + jnp.tanh(math.sqrt(2.0 / math.pi) * (x + 0.044715 * (x * x * x)))))


def _softplus(x):
    return jnp.maximum(x, 0.0) + jnp.log1p(jnp.exp(-jnp.abs(x)))


def _rglru_kernel(*refs, S):
    xa_ref, ga_ref, cw_ref, cb_ref, w_ref, b_ref, lam_ref, h0_ref = refs[:8]
    y_ref, hfin_ref, pad_ref, af_ref, bf_ref, ab_ref, bb_ref = refs[-7:]
    C = LANES
    P0 = SUBLANES
    pad_ref[0:P0, :] = jnp.zeros((P0, C), F32)
    pad_ref[P0 + S:P0 + S + P0, :] = jnp.zeros((P0, C), F32)
    pad_ref[P0:P0 + S, :] = xa_ref[...]
    cw = cw_ref[...]
    xc = jnp.broadcast_to(cb_ref[...], (S, C))
    for j in range(CONV_W):
        xc = xc + pad_ref[pl.ds(P0 - 2 + j, S), :] * cw[j:j + 1, :]

    gates = jnp.dot(xc.astype(BF), w_ref[...], preferred_element_type=F32) + b_ref[...]
    lam = lam_ref[...]
    for d, (a_ref, b_ref_) in enumerate(((af_ref, bf_ref), (ab_ref, bb_ref))):
        r = _sigmoid(gates[:, (2 * d) * C:(2 * d + 1) * C])
        i = _sigmoid(gates[:, (2 * d + 1) * C:(2 * d + 2) * C])
        log_a = (-LRU_C * r) * _softplus(-lam[d:d + 1, :])
        a_ref[...] = jnp.exp(log_a)
        th = jnp.tanh(log_a)
        one_minus_a2 = (-2.0 * th) / (1.0 - th)
        b_ref_[...] = (jnp.sqrt(one_minus_a2) * i) * xc

    rows = lax.broadcasted_iota(jnp.int32, (SUBLANES, C), 0)

    def tile_scan(a, b, down):
        for k in (1, 2, 4):
            if down:
                keep = rows >= k
                shift = k
            else:
                keep = rows < SUBLANES - k
                shift = SUBLANES - k
            a1 = jnp.where(keep, pltpu.roll(a, shift, 0), 1.0)
            b1 = jnp.where(keep, pltpu.roll(b, shift, 0), 0.0)
            b = a * b1 + b
            a = a * a1
        return a, b

    nt = S // SUBLANES
    unroll = SCAN_UNROLL if nt % SCAN_UNROLL == 0 else 1

    def body(tu, carry):
        hf, hb = carry
        tiles = []
        for u in range(unroll):
            t = tu * unroll + u
            r0 = pl.multiple_of(t * SUBLANES, SUBLANES)
            r1 = pl.multiple_of((nt - 1 - t) * SUBLANES, SUBLANES)
            fwd = tile_scan(af_ref[pl.ds(r0, SUBLANES), :], bf_ref[pl.ds(r0, SUBLANES), :], True)
            bwd = tile_scan(ab_ref[pl.ds(r1, SUBLANES), :], bb_ref[pl.ds(r1, SUBLANES), :], False)
            tiles.append((r0, fwd, r1, bwd))
        for r0, (a, b), r1, (a2, b2) in tiles:
            h = a * hf + b
            bf_ref[pl.ds(r0, SUBLANES), :] = h
            hf = jnp.broadcast_to(h[SUBLANES - 1:SUBLANES, :], (SUBLANES, C))
            g = a2 * hb + b2
            bb_ref[pl.ds(r1, SUBLANES), :] = g
            hb = jnp.broadcast_to(g[0:1, :], (SUBLANES, C))
        return hf, hb

    h0 = h0_ref[...]
    hf, hb = lax.fori_loop(0, nt // unroll, body, (jnp.broadcast_to(h0[0:1, :], (SUBLANES, C)),
                                                    jnp.broadcast_to(h0[1:2, :], (SUBLANES, C))))
    hfin_ref[0:1, :] = hf[0:1, :]
    hfin_ref[1:2, :] = hb[0:1, :]
    y_ref[...] = (_gelu_tanh(ga_ref[...]) * (bf_ref[...] + bb_ref[...])).astype(y_ref.dtype)


def _rglru(z, row0, nb, S, d_rnn, conv_w, conv_b, wcat, bcat, lam, h0, out_buf, out_shape):
    C = LANES
    ncb = d_rnn // C
    roff = row0 // S
    in_specs = [
        pl.BlockSpec((S, C), lambda b, c: (roff + b, c)),
        pl.BlockSpec((S, C), lambda b, c: (roff + b, ncb + c)),
        pl.BlockSpec((CONV_W, C), lambda b, c: (0, c)),
        pl.BlockSpec((1, C), lambda b, c: (0, c)),
        pl.BlockSpec((None, C, 4 * C), lambda b, c: (c, 0, 0)),
        pl.BlockSpec((None, 1, 4 * C), lambda b, c: (c, 0, 0)),
        pl.BlockSpec((2, C), lambda b, c: (0, c)),
        pl.BlockSpec((None, 2, C), lambda b, c: (b, 0, c)),
    ]
    args = [z, z, conv_w, conv_b.reshape(1, d_rnn), wcat, bcat, lam, h0]
    aliases = _in_place(in_specs, args, out_buf)
    return pl.pallas_call(
        functools.partial(_rglru_kernel, S=S),
        grid=(nb, ncb),
        in_specs=in_specs,
        out_specs=[
            pl.BlockSpec((S, C), lambda b, c: (roff + b, c)),
            pl.BlockSpec((None, 2, C), lambda b, c: (b, 0, c)),
        ],
        out_shape=[
            jax.ShapeDtypeStruct(out_shape, BF),
            jax.ShapeDtypeStruct((nb, 2, d_rnn), F32),
        ],
        input_output_aliases=aliases,
        scratch_shapes=[pltpu.VMEM((S + 2 * SUBLANES, C), F32)] + [pltpu.VMEM((S, C), F32)] * 4,
        compiler_params=_cparams("parallel", "parallel"),
        name="rglru",
    )(*args)


def _rope_tables(n_tok):
    t = jnp.arange(n_tok, dtype=jnp.int32)
    row, col = t // GRID_W, t % GRID_W
    n = 16
    inv = ROPE_THETA ** (-jnp.arange(n, dtype=F32) / n)
    ang_r = row.astype(F32)[:, None] * inv
    ang_c = col.astype(F32)[:, None] * inv

    def grp(ang):
        c, s = jnp.cos(ang), jnp.sin(ang)
        return jnp.concatenate([c, c], -1), jnp.concatenate([-s, s], -1)
    cr, sr = grp(ang_r)
    cc, sc = grp(ang_c)
    return jnp.concatenate([cr, cc, cr, cc], -1), jnp.concatenate([sr, sc, sr, sc], -1)


def _rope_kernel(x_ref, cos_ref, sin_ref, o_ref):
    cos = cos_ref[...]
    sin = sin_ref[...]
    first = (lax.broadcasted_iota(jnp.int32, cos.shape, 1) % 32) < 16
    for c in range(x_ref.shape[1] // LANES):
        x = x_ref[:, c * LANES:(c + 1) * LANES].astype(F32)
        partner = jnp.where(first, pltpu.roll(x, LANES - 16, 1), pltpu.roll(x, 16, 1))
        o_ref[:, c * LANES:(c + 1) * LANES] = (x * cos + partner * sin).astype(o_ref.dtype)


def _rope(x, cos, sin, row0, rows, col0, width, ds):
    tm = _pick(ds, 256, SUBLANES)
    tw = _pick(width, 1024, LANES)
    roff, coff, per = row0 // tm, col0 // tw, ds // tm
    return pl.pallas_call(
        _rope_kernel,
        grid=(rows // tm, width // tw),
        in_specs=[
            pl.BlockSpec((tm, tw), lambda i, j: (roff + i, coff + j)),
            pl.BlockSpec((tm, LANES), lambda i, j: (i % per, 0)),
            pl.BlockSpec((tm, LANES), lambda i, j: (i % per, 0)),
        ],
        out_specs=pl.BlockSpec((tm, tw), lambda i, j: (i, j)),
        out_shape=jax.ShapeDtypeStruct((rows, width), BF),
        compiler_params=_cparams("parallel", "parallel"),
        name="rope",
    )(x, cos, sin)


def _dattn_kernel(*refs, P, S, tq, G, lam_init):
    if P:
        q_ref, kc_ref, vc_ref, kl_ref, vl_ref, lq_ref, lk_ref, g_ref = refs[:8]
    else:
        q_ref, kl_ref, vl_ref, lq_ref, lk_ref, g_ref = refs[:6]
    o_ref, kk, vv = refs[-3:]
    C = 2 * DK_B

    @pl.when(pl.program_id(2) == 0)
    def _():
        if P:
            kk[0:P, :] = kc_ref[...].astype(BF)
            vv[0:P, :] = vc_ref[...].astype(BF)
        kk[P:P + S, :] = kl_ref[...].astype(BF)
        vv[P:P + S, :] = vl_ref[...].astype(BF)

    el = jnp.exp(jnp.sum(lq_ref[...] * lk_ref[...], axis=-1, keepdims=True))
    lam = el[0:1, :] - el[1:2, :] + lam_init
    sub = min(DATT_ROWS, tq)
    chains = []
    for g in range(G):
        cols = slice(g * C, (g + 1) * C)
        for r0 in range(0, tq, sub):
            q = q_ref[r0:r0 + sub, cols].astype(F32)
            lane = lax.broadcasted_iota(jnp.int32, q.shape, 1)
            qq = jnp.concatenate([jnp.where(lane < DK_B, q, 0.0), jnp.where(lane >= DK_B, q, 0.0)],
                                 axis=0).astype(BF)
            s = lax.dot_general(qq, kk[:, cols], (((1,), (1,)), ((), ())), preferred_element_type=F32)
            chains.append((cols, r0, s))
    for cols, r0, s in chains:
        e = jnp.exp2((s - jnp.max(s, axis=-1, keepdims=True)) * ((DK_B ** -0.5) * LOG2E))
        l = jnp.sum(e, axis=-1, keepdims=True)
        a = e[:sub, :] - e[sub:, :] * (lam * l[:sub, :] / l[sub:, :])
        o = jnp.dot(a.astype(BF), vv[:, cols], preferred_element_type=F32) / l[:sub, :]
        o_ref[r0:r0 + sub, cols] = ((_rms(o) * g_ref[...]) * (1.0 - lam_init)).astype(o_ref.dtype)


def _dattn(q, qrow0, qcol0, k_lat, krow0, kcol0, v_lat, vrow0, vcol0, cache, nb, S, lq, lk, g, lam_init,
           out_buf, orow0, ocol0):
    C = 2 * DK_B
    tq = _pick(S, 512, SUBLANES)
    nq = S // tq
    G = math.gcd(H_B, SHORT_SEQ_HEADS) if nq == 1 else 1
    W = C * G
    P = cache[0].shape[1] if cache is not None else 0
    qr, kr, vr, orr = qrow0 // tq, krow0 // S, vrow0 // S, orow0 // tq
    qc, kc, vc, oc = qcol0 // W, kcol0 // W, vcol0 // W, ocol0 // W
    in_specs = [pl.BlockSpec((tq, W), lambda b, h, i: (qr + b * nq + i, qc + h))]
    args = [q]
    if P:
        in_specs += [pl.BlockSpec((None, P, W), lambda b, h, i: (b, 0, h)),
                     pl.BlockSpec((None, P, W), lambda b, h, i: (b, 0, h))]
        args += [cache[0], cache[1]]
    in_specs += [
        pl.BlockSpec((S, W), lambda b, h, i: (kr + b, kc + h)),
        pl.BlockSpec((S, W), lambda b, h, i: (vr + b, vc + h)),
        pl.BlockSpec((2, DK_B), lambda b, h, i: (0, 0)),
        pl.BlockSpec((2, DK_B), lambda b, h, i: (0, 0)),
        pl.BlockSpec((1, C), lambda b, h, i: (0, 0)),
    ]
    args += [k_lat, v_lat, lq, lk, g.reshape(1, C)]
    aliases = _in_place(in_specs, args, out_buf)
    return pl.pallas_call(
        functools.partial(_dattn_kernel, P=P, S=S, tq=tq, G=G, lam_init=lam_init),
        grid=(nb, H_B // G, nq),
        in_specs=in_specs,
        out_specs=pl.BlockSpec((tq, W), lambda b, h, i: (orr + b * nq + i, oc + h)),
        out_shape=jax.ShapeDtypeStruct(out_buf.shape, BF),
        input_output_aliases=aliases,
        scratch_shapes=[pltpu.VMEM((P + S, W), BF), pltpu.VMEM((P + S, W), BF)],
        compiler_params=_cparams("parallel", "parallel", "arbitrary"),
        name="diff_attn",
    )(*args)


def _mla_in_kernel(h_ref, w_ref, gq_ref, gkv_ref, cq_ref, ckv_ref, kr_ref, *, ql, kvl):
    z = jnp.dot(h_ref[...], w_ref[...], preferred_element_type=F32)
    cq_ref[...] = (_rms(z[:, :ql]) * gq_ref[...]).astype(cq_ref.dtype)
    ckv_ref[...] = _rms(z[:, ql:ql + kvl]) * gkv_ref[...]
    kr_ref[...] = z[:, ql + kvl:]


def _mla_in(h, w, gq, gkv, ql, kvl, tm):
    M, D = h.shape
    N = w.shape[1]
    return pl.pallas_call(
        functools.partial(_mla_in_kernel, ql=ql, kvl=kvl),
        grid=(M // tm,),
        in_specs=[
            pl.BlockSpec((tm, D), lambda i: (i, 0)),
            pl.BlockSpec((D, N), lambda i: (0, 0)),
            pl.BlockSpec((1, ql), lambda i: (0, 0)),
            pl.BlockSpec((1, kvl), lambda i: (0, 0)),
        ],
        out_specs=[
            pl.BlockSpec((tm, ql), lambda i: (i, 0)),
            pl.BlockSpec((tm, kvl), lambda i: (i, 0)),
            pl.BlockSpec((tm, LANES), lambda i: (i, 0)),
        ],
        out_shape=[
            jax.ShapeDtypeStruct((M, ql), BF),
            jax.ShapeDtypeStruct((M, kvl), F32),
            jax.ShapeDtypeStruct((M, LANES), F32),
        ],
        compiler_params=_cparams("parallel"),
        name="mla_in",
    )(h, w, gq.reshape(1, ql), gkv.reshape(1, kvl))


def _mla_kernel(*refs, P, S, tq, G):
    if P:
        qn_ref, qr_ref, kvc_ref, krc_ref, kvl_ref, krl_ref = refs[:6]
        segs = ((0, P, kvc_ref, krc_ref), (P, S, kvl_ref, krl_ref))
    else:
        qn_ref, qr_ref, kvl_ref, krl_ref = refs[:4]
        segs = ((0, S, kvl_ref, krl_ref),)
    o_ref, ks, vs = refs[-3:]
    C = LANES

    @pl.when(pl.program_id(2) == 0)
    def _():
        for r0, n, kv_ref, kr_ref in segs:
            kr = kr_ref[...].astype(F32)
            kr2 = (kr + pltpu.roll(kr, ROPE_C, 1)).astype(BF)
            for hd in range(2 * G):
                ks[hd, r0:r0 + n, 0:C] = kv_ref[:, 2 * hd * C:(2 * hd + 1) * C]
                ks[hd, r0:r0 + n, C:2 * C] = kr2
                vs[hd, r0:r0 + n, 0:C] = kv_ref[:, (2 * hd + 1) * C:(2 * hd + 2) * C]
                vs[hd, r0:r0 + n, C:2 * C] = jnp.ones((n, C), BF)

    c = ((NOPE_C + ROPE_C) ** -0.5) * LOG2E
    chains = []
    sub = min(ATT_ROWS, tq)
    for r0 in range(0, tq, sub):
        rows = slice(r0, r0 + sub)
        for p in range(G):
            qr = qr_ref[rows, p * C:(p + 1) * C].astype(F32)
            lane = lax.broadcasted_iota(jnp.int32, qr.shape, 1)
            for hh in range(2):
                hd = 2 * p + hh
                keep = (lane < ROPE_C) if hh == 0 else (lane >= ROPE_C)
                qf = jnp.concatenate([qn_ref[rows, hd * C:(hd + 1) * C], jnp.where(keep, qr, 0.0).astype(BF)],
                                     axis=1)
                s = lax.dot_general(qf, ks[hd], (((1,), (1,)), ((), ())), preferred_element_type=F32)
                chains.append((rows, hd, s))
    for rows, hd, s in chains:
        e = jnp.exp2((s - jnp.max(s, axis=-1, keepdims=True)) * c).astype(BF)
        oe = jnp.dot(e, vs[hd], preferred_element_type=F32)
        o_ref[rows, hd * C:(hd + 1) * C] = (oe[:, 0:C] / oe[:, C:2 * C]).astype(o_ref.dtype)


def _mla_attn(qn, qr, qr_row0, kv_lat, kr_lat, kr_row0, row0, cache, nb, S, out_buf, out_shape):
    C = LANES
    tq = _pick(S, 512, SUBLANES)
    nq = S // tq
    G = math.gcd(H_C // 2, SHORT_SEQ_HEADS) if nq == 1 else 1
    P = cache[0].shape[0] // nb if cache is not None else 0
    r_q, r_qr, r_kv, r_kr = row0 // tq, qr_row0 // tq, row0 // S, kr_row0 // S
    in_specs = [pl.BlockSpec((tq, 2 * C * G), lambda b, h, i: (r_q + b * nq + i, h)),
                pl.BlockSpec((tq, C * G), lambda b, h, i: (r_qr + b * nq + i, h))]
    args = [qn, qr]
    if P:
        in_specs += [pl.BlockSpec((P, 4 * C * G), lambda b, h, i: (b, h)),
                     pl.BlockSpec((P, C), lambda b, h, i: (b, 0))]
        args += [cache[0], cache[1]]
    in_specs += [pl.BlockSpec((S, 4 * C * G), lambda b, h, i: (r_kv + b, h)),
                 pl.BlockSpec((S, C), lambda b, h, i: (r_kr + b, 0))]
    args += [kv_lat, kr_lat]
    aliases = _in_place(in_specs, args, out_buf)
    return pl.pallas_call(
        functools.partial(_mla_kernel, P=P, S=S, tq=tq, G=G),
        grid=(nb, H_C // (2 * G), nq),
        in_specs=in_specs,
        out_specs=pl.BlockSpec((tq, 2 * C * G), lambda b, h, i: (r_q + b * nq + i, h)),
        out_shape=jax.ShapeDtypeStruct(out_shape, BF),
        input_output_aliases=aliases,
        scratch_shapes=[pltpu.VMEM((2 * G, P + S, 2 * C), BF)] * 2,
        compiler_params=_cparams("parallel", "parallel", "arbitrary"),
        name="mla_attn",
    )(*args)


def kernel(x_prompt, x_sample, c, c_ctx, state_rglru, cache_dk, cache_dv, cache_ckv, cache_krope,
           mod_w, mod_b, norm_g, final_g, ffn_wg, ffn_wu, ffn_wd,
           ev_w_in, ev_conv_w, ev_conv_b, ev_wa, ev_ba, ev_wx, ev_bx, ev_lam, ev_lq, ev_lk,
           ev_subln_g, ev_w_out, od_w_in, od_qnorm_g, od_w_uq, od_kvnorm_g, od_w_ukv, od_w_out):
    assert NOPE_C == LANES and V_C == LANES and 2 * DK_B == LANES and 2 * ROPE_C == LANES
    NBP, SP, D = x_prompt.shape
    NB, DS, _ = x_sample.shape
    PAST = cache_dk.shape[2]
    L = mod_w.shape[0]
    MP, MS = NBP * SP, NB * DS
    tok = _Tokens(MP, DS, NB, 1 + NB)
    M = tok.m
    d_rnn = ev_lam.shape[-1]
    d_ff = ffn_wg.shape[-1]
    ql, kvl = od_qnorm_g.shape[-1], od_kvnorm_g.shape[-1]
    assert d_rnn // H_A == LANES

    n_cond = -(-(1 + NB) // SUBLANES) * SUBLANES
    cond = jnp.concatenate([c_ctx[None, :], c, jnp.zeros((n_cond - 1 - NB, D), F32)], axis=0)
    mod = _modulation(cond, mod_w, mod_b)
    modt = mod[:, :1 + NB].reshape(L * (1 + NB) * N_MOD, 1, D)
    normt = norm_g.reshape(L * 3, 1, D)

    fpad = -(-d_ff // FF_ALIGN) * FF_ALIGN - d_ff
    wg = jnp.pad(ffn_wg.astype(BF), ((0, 0), (0, 0), (0, 0), (0, fpad)))
    wu = jnp.pad(ffn_wu.astype(BF), ((0, 0), (0, 0), (0, 0), (0, fpad)))
    wd = jnp.pad(ffn_wd.astype(BF), ((0, 0), (0, 0), (0, fpad), (0, 0)))

    cos, sin = _rope_tables(DS)
    tm = tok.tile(1024)
    new = {}
    parts = [(x_prompt.reshape(MP, D), 0), (x_sample.reshape(MS, D), MP)]

    for l in range(L):
        x = _ffn(parts, normt, modt, tok, l, 0, wg[l, 0], wu[l, 0], wd[l, 0])
        h = _prenorm([(x, 0)], normt, modt, tok, l, 1)
        if l % 2 == 0:
            e = l // 2
            lam_init = 0.8 - 0.6 * math.exp(-0.3 * l)
            z = _mm(h, ev_w_in[e].astype(BF), F32, tm=tm, name="ev_in")
            q0, k0_, v0 = 2 * d_rnn, 2 * d_rnn + H_B * LANES, 2 * d_rnn + 2 * H_B * LANES
            wcat = jnp.concatenate([ev_wa[e, 0], ev_wx[e, 0], ev_wa[e, 1], ev_wx[e, 1]], axis=-1).astype(BF)
            bcat = jnp.concatenate([t.reshape(H_A, 1, LANES) for t in
                                    (ev_ba[e, 0], ev_bx[e, 0], ev_ba[e, 1], ev_bx[e, 1])], axis=-1)
            mix_shape = (M, d_rnn + H_B * LANES)
            rg = functools.partial(_rglru, z, d_rnn=d_rnn, conv_w=ev_conv_w[e], conv_b=ev_conv_b[e],
                                   wcat=wcat, bcat=bcat, lam=ev_lam[e], out_shape=mix_shape)
            y_in, hfin = rg(row0=0, nb=NBP, S=SP, h0=jnp.zeros((NBP, 2, d_rnn), F32), out_buf=None)
            y_in, _ = rg(row0=MP, nb=NB, S=DS, h0=state_rglru[:, e], out_buf=y_in)
            y_in = _dattn(z, 0, q0, z, 0, k0_, z, 0, v0, None, NBP, SP,
                          ev_lq[e], ev_lk[e], ev_subln_g[e], lam_init, y_in, 0, d_rnn)
            qk = _rope(z, cos, sin, MP, MS, q0, 2 * H_B * LANES, DS)
            cache = (cache_dk[:, e].reshape(NB, PAST, H_B * LANES), cache_dv[:, e].reshape(NB, PAST, H_B * LANES))
            y_in = _dattn(qk, 0, 0, qk, 0, H_B * LANES, z, MP, v0, cache, NB, DS,
                          ev_lq[e], ev_lk[e], ev_subln_g[e], lam_init, y_in, MP, d_rnn)
            w_out = ev_w_out[e].astype(BF)
            new.setdefault('rec', []).append(hfin)
            new.setdefault('dk', []).append(z[:MP, k0_:v0].reshape(NBP, SP, H_B, 2 * DK_B))
            new.setdefault('dv', []).append(z[:MP, v0:].reshape(NBP, SP, H_B, 2 * DK_B))
        else:
            o = l // 2
            w_in = jnp.pad(od_w_in[o].astype(BF), ((0, 0), (0, LANES - ROPE_C)))
            cqn, ckvn, kr = _mla_in(h, w_in, od_qnorm_g[o], od_kvnorm_g[o], ql, kvl, tok.tile(512))
            w_uq = od_w_uq[o].astype(BF).reshape(ql, H_C, NOPE_C + ROPE_C)
            qn = _mm(cqn, w_uq[:, :, :NOPE_C].reshape(ql, H_C * NOPE_C), BF, tm=tm, name="uq_nope")
            qr = _mm(cqn, w_uq[:, :, NOPE_C:].reshape(ql, H_C * ROPE_C), F32, tm=tm, name="uq_rope")
            w_ukv = od_w_ukv[o].astype(BF)
            kv = _mm(ckvn, w_ukv, BF, tm=tm, name="ukv")
            mix_shape = (M, H_C * V_C)
            y_in = _mla_attn(qn, qr, 0, kv, kr, 0, 0, None, NBP, SP, None, mix_shape)
            qr_s = _rope(qr, cos, sin, MP, MS, 0, H_C * ROPE_C, DS)
            kr_s = _rope(kr, cos, sin, MP, MS, 0, LANES, DS)
            ckv_c = cache_ckv[:, o].reshape(NB * PAST, kvl)
            kv_c = _mm(ckv_c, w_ukv, BF, tm=_pick(NB * PAST, 1024, SUBLANES), name="ukv_ctx")
            kr_c = jnp.pad(cache_krope[:, o].reshape(NB * PAST, ROPE_C), ((0, 0), (0, LANES - ROPE_C)))
            y_in = _mla_attn(qn, qr_s, 0, kv, kr_s, 0, MP, (kv_c, kr_c), NB, DS, y_in, mix_shape)
            w_out = od_w_out[o].astype(BF)
            new.setdefault('ckv', []).append(ckvn[:MP].reshape(NBP, SP, kvl))
            new.setdefault('kr', []).append(kr[:MP, :ROPE_C].reshape(NBP, SP, ROPE_C))
        x = _mm_residual(y_in, w_out, [(x, 0)], modt, tok, l, 5, 1.0, tm=tm, tn=_pick(D, 1024, LANES),
                         name="mix_out")
        x = _ffn([(x, 0)], normt, modt, tok, l, 2, wg[l, 1], wu[l, 1], wd[l, 1])
        parts = [(x, 0)]

    y_prompt = _final_norm(x, final_g, 0, MP).reshape(NBP, SP, D)
    y_sample = _final_norm(x, final_g, MP, MS).reshape(NB, DS, D)
    return (y_prompt, y_sample, jnp.stack(new['rec'], axis=1), jnp.stack(new['dk'], axis=1),
            jnp.stack(new['dv'], axis=1), jnp.stack(new['ckv'], axis=1), jnp.stack(new['kr'], axis=1))
```

```python
import functools
import math

import jax
import jax.numpy as jnp
from jax import lax
from jax.experimental import pallas as pl
from jax.experimental.pallas import tpu as pltpu

BF = jnp.bfloat16
F32 = jnp.float32

GRID_W = 64
EPS = 1e-6
ROPE_THETA = 10000.0
N_MOD = 9
H_A = 16
CONV_W = 4
LRU_C = 8.0
H_B = 16
DK_B = 64
H_C = 32
NOPE_C = 128
ROPE_C = 64
V_C = 128

LANES = 128
SUBLANES = 8
VMEM_LIMIT = 56 * 1024 * 1024
FF_ALIGN = 1024
LOG2E = math.log2(math.e)
ATT_ROWS = 256
DATT_ROWS = 128
SHORT_SEQ_HEADS = 4
SCAN_UNROLL = 4


def _cparams(*sem):
    return pltpu.CompilerParams(dimension_semantics=sem, vmem_limit_bytes=VMEM_LIMIT)


def _pick(n, pref, mult):
    best = None
    d = mult
    while d <= min(n, pref):
        if n % d == 0:
            best = d
        d += mult
    return n if best is None else best


def _sigmoid(x):
    return 1.0 / (1.0 + jnp.exp2(x * (-LOG2E)))


def _rms(x):
    return x * lax.rsqrt(jnp.mean(x * x, axis=-1, keepdims=True) + EPS)


def _in_place(in_specs, args, out_buf):
    if out_buf is None:
        return {}
    in_specs.append(pl.BlockSpec(memory_space=pl.ANY))
    args.append(out_buf)
    return {len(args) - 1: 0}


def _mod_kernel(c_ref, w_ref, b_ref, o_ref):
    k = pl.program_id(2)
    c = c_ref[...]
    s = (c * _sigmoid(c)).astype(BF)
    part = jnp.dot(s, w_ref[...].astype(BF), preferred_element_type=F32)

    @pl.when(k == 0)
    def _():
        o_ref[...] = part + b_ref[...]

    @pl.when(k > 0)
    def _():
        o_ref[...] += part


def _modulation(cond, mod_w, mod_b):
    R, D = cond.shape
    L, _, N = mod_w.shape
    tn = _pick(N, 2048, LANES)
    tk = _pick(D, 1024, LANES)
    return pl.pallas_call(
        _mod_kernel,
        grid=(L, N // tn, D // tk),
        in_specs=[
            pl.BlockSpec((R, tk), lambda l, n, k: (0, k)),
            pl.BlockSpec((None, tk, tn), lambda l, n, k: (l, k, n)),
            pl.BlockSpec((None, 1, tn), lambda l, n, k: (l, 0, n)),
        ],
        out_specs=pl.BlockSpec((None, R, tn), lambda l, n, k: (l, 0, n)),
        out_shape=jax.ShapeDtypeStruct((L, R, N), F32),
        compiler_params=_cparams("parallel", "parallel", "arbitrary"),
        name="modulation",
    )(cond, mod_w, mod_b.reshape(L, 1, N))


class _Tokens:
    def __init__(self, mp, ds, nb, n_rows):
        self.mp, self.ds, self.nb, self.n_rows = mp, ds, nb, n_rows
        self.m = mp + ds * nb

    def tile(self, pref):
        return _pick(math.gcd(self.mp, self.ds), pref, SUBLANES)

    def mod_index(self, layer, chunk, tm):
        mp, ds, n_rows = self.mp, self.ds, self.n_rows

        def f(i):
            r = jnp.where(i * tm < mp, 0, 1 + (i * tm - mp) // ds)
            return (layer * n_rows + r) * N_MOD + chunk
        return f


def _prenorm_kernel(x_ref, g_ref, sh_ref, sc_ref, *rest):
    o_ref = rest[-1]
    y = _rms(x_ref[...]) * g_ref[...]
    o_ref[...] = (y * (1.0 + sc_ref[...]) + sh_ref[...]).astype(o_ref.dtype)


def _prenorm(parts, normt, modt, tok, layer, sub):
    h = None
    for x, row0 in parts:
        rows, D = x.shape
        tm = tok.tile(256)
        off = row0 // tm
        sh = tok.mod_index(layer, 3 * sub, tm)
        sc = tok.mod_index(layer, 3 * sub + 1, tm)
        in_specs = [
            pl.BlockSpec((tm, D), lambda i: (i, 0)),
            pl.BlockSpec((None, 1, D), lambda i: (layer * 3 + sub, 0, 0)),
            pl.BlockSpec((None, 1, D), lambda i: (sh(off + i), 0, 0)),
            pl.BlockSpec((None, 1, D), lambda i: (sc(off + i), 0, 0)),
        ]
        args = [x, normt, modt, modt]
        aliases = _in_place(in_specs, args, h)
        h = pl.pallas_call(
            _prenorm_kernel,
            grid=(rows // tm,),
            in_specs=in_specs,
            out_specs=pl.BlockSpec((tm, D), lambda i: (off + i, 0)),
            out_shape=jax.ShapeDtypeStruct((tok.m, D), BF),
            input_output_aliases=aliases,
            compiler_params=_cparams("parallel"),
            name="prenorm",
        )(*args)
    return h


def _final_norm_kernel(x_ref, g_ref, o_ref):
    o_ref[...] = _rms(x_ref[...]) * g_ref[...]


def _final_norm(x, g, row0, rows):
    D = x.shape[1]
    tm = _pick(math.gcd(row0, rows) if row0 else rows, 256, SUBLANES)
    off = row0 // tm
    return pl.pallas_call(
        _final_norm_kernel,
        grid=(rows // tm,),
        in_specs=[
            pl.BlockSpec((tm, D), lambda i: (off + i, 0)),
            pl.BlockSpec((1, D), lambda i: (0, 0)),
        ],
        out_specs=pl.BlockSpec((tm, D), lambda i: (i, 0)),
        out_shape=jax.ShapeDtypeStruct((rows, D), F32),
        compiler_params=_cparams("parallel"),
        name="final_norm",
    )(x, g.reshape(1, D))


def _rope_lanes(x, cos, sin, first):
    partner = jnp.where(first, pltpu.roll(x, LANES - 16, 1), pltpu.roll(x, 16, 1))
    return x * cos + partner * sin


def _mm_kernel(*refs, nk, residual, coef, rope=False):
    a_ref, w_ref = refs[:2]
    if residual:
        r_ref, g_ref = refs[2:4]
    if rope:
        cos_ref, sin_ref = refs[2:4]
    o_ref, acc_ref = (refs[-2], refs[-1]) if nk > 1 else (refs[-1], None)

    def finish(acc):
        if residual:
            o_ref[...] = r_ref[...] + (coef * g_ref[...]) * acc
        elif rope:
            cos, sin = cos_ref[...], sin_ref[...]
            first = (lax.broadcasted_iota(jnp.int32, cos.shape, 1) % 32) < 16
            for c in range(acc.shape[1] // LANES):
                cols = slice(c * LANES, (c + 1) * LANES)
                o_ref[:, cols] = _rope_lanes(acc[:, cols], cos, sin, first).astype(o_ref.dtype)
        else:
            o_ref[...] = acc.astype(o_ref.dtype)

    if nk == 1:
        finish(jnp.dot(a_ref[...].astype(BF), w_ref[...], preferred_element_type=F32))
        return
    k = pl.program_id(2)

    @pl.when(k == 0)
    def _():
        acc_ref[...] = jnp.zeros_like(acc_ref)

    acc_ref[...] += jnp.dot(a_ref[...].astype(BF), w_ref[...], preferred_element_type=F32)

    @pl.when(k == nk - 1)
    def _():
        finish(acc_ref[...])


def _mm(a, w, out_dtype, *, tm, tn=None, tk=None, row0=0, rows=None, rope=None, name="mm"):
    K, N = w.shape
    tn = _pick(N, 1024, LANES) if tn is None else tn
    rows = a.shape[0] if rows is None else rows
    tk = K if tk is None else tk
    nk = K // tk
    off = row0 // tm
    in_specs = [
        pl.BlockSpec((tm, tk), lambda i, j, k: (off + i, k)),
        pl.BlockSpec((tk, tn), lambda i, j, k: (k, j)),
    ]
    args = [a, w]
    if rope is not None:
        cos, sin, ds = rope
        per = ds // tm
        in_specs += [pl.BlockSpec((tm, LANES), lambda i, j, k: (i % per, 0))] * 2
        args += [cos, sin]
    return pl.pallas_call(
        functools.partial(_mm_kernel, nk=nk, residual=False, coef=None, rope=rope is not None),
        grid=(rows // tm, N // tn, nk),
        in_specs=in_specs,
        out_specs=pl.BlockSpec((tm, tn), lambda i, j, k: (i, j)),
        out_shape=jax.ShapeDtypeStruct((rows, N), out_dtype),
        scratch_shapes=[pltpu.VMEM((tm, tn), F32)] if nk > 1 else [],
        compiler_params=_cparams("parallel", "parallel", "arbitrary"),
        name=name,
    )(*args)


def _mm_residual(a, w, w_lead, parts, modt, tok, layer, chunk, coef, *, tm, tn, tk=None, name="mm_res"):
    K, N = w.shape[-2:]
    tk = K if tk is None else tk
    nk = K // tk
    gate = tok.mod_index(layer, chunk, tm)
    out = None
    for x, row0 in parts:
        off = row0 // tm
        in_specs = [
            pl.BlockSpec((tm, tk), lambda i, j, k: (off + i, k)),
            pl.BlockSpec((None,) * len(w_lead) + (tk, tn), lambda i, j, k: w_lead + (k, j)),
            pl.BlockSpec((tm, tn), lambda i, j, k: (i, j)),
            pl.BlockSpec((None, 1, tn), lambda i, j, k: (gate(off + i), 0, j)),
        ]
        args = [a, w, x, modt]
        aliases = _in_place(in_specs, args, out)
        out = pl.pallas_call(
            functools.partial(_mm_kernel, nk=nk, residual=True, coef=coef),
            grid=(x.shape[0] // tm, N // tn, nk),
            in_specs=in_specs,
            out_specs=pl.BlockSpec((tm, tn), lambda i, j, k: (off + i, j)),
            out_shape=jax.ShapeDtypeStruct((tok.m, N), F32),
            input_output_aliases=aliases,
            scratch_shapes=[pltpu.VMEM((tm, tn), F32)] if nk > 1 else [],
            compiler_params=_cparams("parallel", "parallel", "arbitrary"),
            name=name,
        )(*args)
    return out


def _ffn_up_kernel(h_ref, wg_ref, wu_ref, o_ref, *, nf):
    j = pl.program_id(1)

    @pl.when(j < nf)
    def _():
        h = h_ref[...]
        g = jnp.dot(h, wg_ref[...], preferred_element_type=F32)
        u = jnp.dot(h, wu_ref[...], preferred_element_type=F32)
        o_ref[...] = ((g * _sigmoid(g)) * u).astype(o_ref.dtype)

    @pl.when(j >= nf)
    def _():
        o_ref[...] = jnp.zeros_like(o_ref)


def _ffn_up(h, wg, wu, w_lead, fp, *, tm, tf):
    M, D = h.shape
    nf = wg.shape[-1] // tf
    wspec = pl.BlockSpec((None,) * len(w_lead) + (D, tf), lambda i, j: w_lead + (0, jnp.minimum(j, nf - 1)))
    return pl.pallas_call(
        functools.partial(_ffn_up_kernel, nf=nf),
        grid=(M // tm, fp // tf),
        in_specs=[pl.BlockSpec((tm, D), lambda i, j: (i, 0)), wspec, wspec],
        out_specs=pl.BlockSpec((tm, tf), lambda i, j: (i, j)),
        out_shape=jax.ShapeDtypeStruct((M, fp), BF),
        compiler_params=_cparams("parallel", "arbitrary"),
        name="ffn_up",
    )(h, wg, wu)


def _ffn(parts, normt, modt, tok, layer, sub, wg, wu, wd, w_lead):
    h = _prenorm(parts, normt, modt, tok, layer, sub)
    d_ff, fp, D = wg.shape[-1], wd.shape[-2], wd.shape[-1]
    tf = _pick(d_ff, 256, LANES)
    assert d_ff % tf == 0 and fp % tf == 0
    a = _ffn_up(h, wg, wu, w_lead, fp, tm=tok.tile(2048), tf=tf)
    tk = _pick(fp, max(fp // 4, LANES), LANES)
    return _mm_residual(a, wd, w_lead, parts, modt, tok, layer, 3 * sub + 2, 0.5,
                        tm=tok.tile(1024), tn=_pick(D, 1024, LANES), tk=tk, name="ffn_down")


def _gelu_tanh(x):
    return x * (0.5 * (1.0 + jnp.tanh(math.sqrt(2.0 / math.pi) * (x + 0.044715 * (x * x * x)))))


def _softplus(x):
    return jnp.maximum(x, 0.0) + jnp.log1p(jnp.exp(-jnp.abs(x)))


def _rglru_kernel(*refs, S):
    xa_ref, ga_ref, cw_ref, cb_ref, w_ref, b_ref, lam_ref, h0_ref = refs[:8]
    y_ref, hfin_ref, pad_ref, af_ref, bf_ref, ab_ref, bb_ref = refs[-7:]
    C = LANES
    P0 = SUBLANES
    pad_ref[0:P0, :] = jnp.zeros((P0, C), F32)
    pad_ref[P0 + S:P0 + S + P0, :] = jnp.zeros((P0, C), F32)
    pad_ref[P0:P0 + S, :] = xa_ref[...]
    cw = cw_ref[...]
    xc = jnp.broadcast_to(cb_ref[...], (S, C))
    for j in range(CONV_W):
        xc = xc + pad_ref[pl.ds(P0 - 2 + j, S), :] * cw[j:j + 1, :]

    gates = jnp.dot(xc.astype(BF), w_ref[...], preferred_element_type=F32) + b_ref[...]
    lam = lam_ref[...]
    for d, (a_ref, b_ref_) in enumerate(((af_ref, bf_ref), (ab_ref, bb_ref))):
        r = _sigmoid(gates[:, (2 * d) * C:(2 * d + 1) * C])
        i = _sigmoid(gates[:, (2 * d + 1) * C:(2 * d + 2) * C])
        log_a = (-LRU_C * r) * _softplus(-lam[d:d + 1, :])
        a_ref[...] = jnp.exp(log_a)
        th = jnp.tanh(log_a)
        one_minus_a2 = (-2.0 * th) / (1.0 - th)
        b_ref_[...] = (jnp.sqrt(one_minus_a2) * i) * xc

    rows = lax.broadcasted_iota(jnp.int32, (SUBLANES, C), 0)

    def tile_scan(a, b, down):
        for k in (1, 2, 4):
            if down:
                keep = rows >= k
                shift = k
            else:
                keep = rows < SUBLANES - k
                shift = SUBLANES - k
            a1 = jnp.where(keep, pltpu.roll(a, shift, 0), 1.0)
            b1 = jnp.where(keep, pltpu.roll(b, shift, 0), 0.0)
            b = a * b1 + b
            a = a * a1
        return a, b

    nt = S // SUBLANES
    unroll = SCAN_UNROLL if nt % SCAN_UNROLL == 0 else 1

    def body(tu, carry):
        hf, hb = carry
        tiles = []
        for u in range(unroll):
            t = tu * unroll + u
            r0 = pl.multiple_of(t * SUBLANES, SUBLANES)
            r1 = pl.multiple_of((nt - 1 - t) * SUBLANES, SUBLANES)
            fwd = tile_scan(af_ref[pl.ds(r0, SUBLANES), :], bf_ref[pl.ds(r0, SUBLANES), :], True)
            bwd = tile_scan(ab_ref[pl.ds(r1, SUBLANES), :], bb_ref[pl.ds(r1, SUBLANES), :], False)
            tiles.append((r0, fwd, r1, bwd))
        for r0, (a, b), r1, (a2, b2) in tiles:
            h = a * hf + b
            bf_ref[pl.ds(r0, SUBLANES), :] = h
            hf = jnp.broadcast_to(h[SUBLANES - 1:SUBLANES, :], (SUBLANES, C))
            g = a2 * hb + b2
            bb_ref[pl.ds(r1, SUBLANES), :] = g
            hb = jnp.broadcast_to(g[0:1, :], (SUBLANES, C))
        return hf, hb

    h0 = h0_ref[...]
    hf, hb = lax.fori_loop(0, nt // unroll, body, (jnp.broadcast_to(h0[0:1, :], (SUBLANES, C)),
                                                    jnp.broadcast_to(h0[1:2, :], (SUBLANES, C))))
    hfin_ref[0:1, :] = hf[0:1, :]
    hfin_ref[1:2, :] = hb[0:1, :]
    y_ref[...] = (_gelu_tanh(ga_ref[...]) * (bf_ref[...] + bb_ref[...])).astype(y_ref.dtype)


def _rglru(z, row0, nb, S, d_rnn, conv_w, conv_b, wcat, bcat, lam, h0, out_buf, out_shape):
    C = LANES
    ncb = d_rnn // C
    roff = row0 // S
    in_specs = [
        pl.BlockSpec((S, C), lambda b, c: (roff + b, c)),
        pl.BlockSpec((S, C), lambda b, c: (roff + b, ncb + c)),
        pl.BlockSpec((CONV_W, C), lambda b, c: (0, c)),
        pl.BlockSpec((1, C), lambda b, c: (0, c)),
        pl.BlockSpec((None, C, 4 * C), lambda b, c: (c, 0, 0)),
        pl.BlockSpec((None, 1, 4 * C), lambda b, c: (c, 0, 0)),
        pl.BlockSpec((2, C), lambda b, c: (0, c)),
        pl.BlockSpec((None, 2, C), lambda b, c: (b, 0, c)),
    ]
    args = [z, z, conv_w, conv_b.reshape(1, d_rnn), wcat, bcat, lam, h0]
    aliases = _in_place(in_specs, args, out_buf)
    return pl.pallas_call(
        functools.partial(_rglru_kernel, S=S),
        grid=(nb, ncb),
        in_specs=in_specs,
        out_specs=[
            pl.BlockSpec((S, C), lambda b, c: (roff + b, c)),
            pl.BlockSpec((None, 2, C), lambda b, c: (b, 0, c)),
        ],
        out_shape=[
            jax.ShapeDtypeStruct(out_shape, BF),
            jax.ShapeDtypeStruct((nb, 2, d_rnn), F32),
        ],
        input_output_aliases=aliases,
        scratch_shapes=[pltpu.VMEM((S + 2 * SUBLANES, C), F32)] + [pltpu.VMEM((S, C), F32)] * 4,
        compiler_params=_cparams("parallel", "parallel"),
        name="rglru",
    )(*args)


def _rope_tables(n_tok):
    t = jnp.arange(n_tok, dtype=jnp.int32)
    row, col = t // GRID_W, t % GRID_W
    n = 16
    inv = ROPE_THETA ** (-jnp.arange(n, dtype=F32) / n)
    ang_r = row.astype(F32)[:, None] * inv
    ang_c = col.astype(F32)[:, None] * inv

    def grp(ang):
        c, s = jnp.cos(ang), jnp.sin(ang)
        return jnp.concatenate([c, c], -1), jnp.concatenate([-s, s], -1)
    cr, sr = grp(ang_r)
    cc, sc = grp(ang_c)
    return jnp.concatenate([cr, cc, cr, cc], -1), jnp.concatenate([sr, sc, sr, sc], -1)


def _rope_kernel(x_ref, cos_ref, sin_ref, o_ref):
    cos = cos_ref[...]
    sin = sin_ref[...]
    first = (lax.broadcasted_iota(jnp.int32, cos.shape, 1) % 32) < 16
    for c in range(x_ref.shape[1] // LANES):
        x = x_ref[:, c * LANES:(c + 1) * LANES].astype(F32)
        partner = jnp.where(first, pltpu.roll(x, LANES - 16, 1), pltpu.roll(x, 16, 1))
        o_ref[:, c * LANES:(c + 1) * LANES] = (x * cos + partner * sin).astype(o_ref.dtype)


def _rope(x, cos, sin, row0, rows, col0, width, ds):
    tm = _pick(ds, 256, SUBLANES)
    tw = _pick(width, 1024, LANES)
    roff, coff, per = row0 // tm, col0 // tw, ds // tm
    return pl.pallas_call(
        _rope_kernel,
        grid=(rows // tm, width // tw),
        in_specs=[
            pl.BlockSpec((tm, tw), lambda i, j: (roff + i, coff + j)),
            pl.BlockSpec((tm, LANES), lambda i, j: (i % per, 0)),
            pl.BlockSpec((tm, LANES), lambda i, j: (i % per, 0)),
        ],
        out_specs=pl.BlockSpec((tm, tw), lambda i, j: (i, j)),
        out_shape=jax.ShapeDtypeStruct((rows, width), BF),
        compiler_params=_cparams("parallel", "parallel"),
        name="rope",
    )(x, cos, sin)


def _dattn_kernel(*refs, P, S, tq, G, lam_init):
    if P:
        q_ref, kc_ref, vc_ref, kl_ref, vl_ref, lq_ref, lk_ref, g_ref = refs[:8]
    else:
        q_ref, kl_ref, vl_ref, lq_ref, lk_ref, g_ref = refs[:6]
    o_ref, kk, vv = refs[-3:]
    C = 2 * DK_B

    @pl.when(pl.program_id(2) == 0)
    def _():
        if P:
            kk[0:P, :] = kc_ref[...].astype(BF)
            vv[0:P, :] = vc_ref[...].astype(BF)
        kk[P:P + S, :] = kl_ref[...].astype(BF)
        vv[P:P + S, :] = vl_ref[...].astype(BF)

    el = jnp.exp(jnp.sum(lq_ref[...] * lk_ref[...], axis=-1, keepdims=True))
    lam = el[0:1, :] - el[1:2, :] + lam_init
    sub = min(DATT_ROWS, tq)
    chains = []
    for g in range(G):
        cols = slice(g * C, (g + 1) * C)
        for r0 in range(0, tq, sub):
            q = q_ref[r0:r0 + sub, cols].astype(F32)
            lane = lax.broadcasted_iota(jnp.int32, q.shape, 1)
            qq = jnp.concatenate([jnp.where(lane < DK_B, q, 0.0), jnp.where(lane >= DK_B, q, 0.0)],
                                 axis=0).astype(BF)
            s = lax.dot_general(qq, kk[:, cols], (((1,), (1,)), ((), ())), preferred_element_type=F32)
            chains.append((cols, r0, s))
    for cols, r0, s in chains:
        e = jnp.exp2((s - jnp.max(s, axis=-1, keepdims=True)) * ((DK_B ** -0.5) * LOG2E))
        l = jnp.sum(e, axis=-1, keepdims=True)
        a = e[:sub, :] - e[sub:, :] * (lam * l[:sub, :] / l[sub:, :])
        o = jnp.dot(a.astype(BF), vv[:, cols], preferred_element_type=F32) / l[:sub, :]
        o_ref[r0:r0 + sub, cols] = ((_rms(o) * g_ref[...]) * (1.0 - lam_init)).astype(o_ref.dtype)


def _dattn(q, qrow0, qcol0, k_lat, krow0, kcol0, v_lat, vrow0, vcol0, cache, nb, S, lq, lk, g, lam_init,
           out_buf, orow0, ocol0):
    C = 2 * DK_B
    tq = _pick(S, 512, SUBLANES)
    nq = S // tq
    G = math.gcd(H_B, SHORT_SEQ_HEADS) if nq == 1 else 1
    W = C * G
    P = cache[0].shape[1] if cache is not None else 0
    qr, kr, vr, orr = qrow0 // tq, krow0 // S, vrow0 // S, orow0 // tq
    qc, kc, vc, oc = qcol0 // W, kcol0 // W, vcol0 // W, ocol0 // W
    in_specs = [pl.BlockSpec((tq, W), lambda b, h, i: (qr + b * nq + i, qc + h))]
    args = [q]
    if P:
        in_specs += [pl.BlockSpec((None, P, W), lambda b, h, i: (b, 0, h)),
                     pl.BlockSpec((None, P, W), lambda b, h, i: (b, 0, h))]
        args += [cache[0], cache[1]]
    in_specs += [
        pl.BlockSpec((S, W), lambda b, h, i: (kr + b, kc + h)),
        pl.BlockSpec((S, W), lambda b, h, i: (vr + b, vc + h)),
        pl.BlockSpec((2, DK_B), lambda b, h, i: (0, 0)),
        pl.BlockSpec((2, DK_B), lambda b, h, i: (0, 0)),
        pl.BlockSpec((1, C), lambda b, h, i: (0, 0)),
    ]
    args += [k_lat, v_lat, lq, lk, g.reshape(1, C)]
    aliases = _in_place(in_specs, args, out_buf)
    return pl.pallas_call(
        functools.partial(_dattn_kernel, P=P, S=S, tq=tq, G=G, lam_init=lam_init),
        grid=(nb, H_B // G, nq),
        in_specs=in_specs,
        out_specs=pl.BlockSpec((tq, W), lambda b, h, i: (orr + b * nq + i, oc + h)),
        out_shape=jax.ShapeDtypeStruct(out_buf.shape, BF),
        input_output_aliases=aliases,
        scratch_shapes=[pltpu.VMEM((P + S, W), BF), pltpu.VMEM((P + S, W), BF)],
        compiler_params=_cparams("parallel", "parallel", "arbitrary"),
        name="diff_attn",
    )(*args)


def _mla_in_kernel(h_ref, w_ref, gq_ref, gkv_ref, cq_ref, ckv_ref, kr_ref, *, ql, kvl):
    z = jnp.dot(h_ref[...], w_ref[...], preferred_element_type=F32)
    cq_ref[...] = (_rms(z[:, :ql]) * gq_ref[...]).astype(cq_ref.dtype)
    ckv_ref[...] = _rms(z[:, ql:ql + kvl]) * gkv_ref[...]
    kr_ref[...] = z[:, ql + kvl:]


def _mla_in(h, w, gq, gkv, ql, kvl, tm):
    M, D = h.shape
    N = w.shape[1]
    return pl.pallas_call(
        functools.partial(_mla_in_kernel, ql=ql, kvl=kvl),
        grid=(M // tm,),
        in_specs=[
            pl.BlockSpec((tm, D), lambda i: (i, 0)),
            pl.BlockSpec((D, N), lambda i: (0, 0)),
            pl.BlockSpec((1, ql), lambda i: (0, 0)),
            pl.BlockSpec((1, kvl), lambda i: (0, 0)),
        ],
        out_specs=[
            pl.BlockSpec((tm, ql), lambda i: (i, 0)),
            pl.BlockSpec((tm, kvl), lambda i: (i, 0)),
            pl.BlockSpec((tm, LANES), lambda i: (i, 0)),
        ],
        out_shape=[
            jax.ShapeDtypeStruct((M, ql), BF),
            jax.ShapeDtypeStruct((M, kvl), F32),
            jax.ShapeDtypeStruct((M, LANES), F32),
        ],
        compiler_params=_cparams("parallel"),
        name="mla_in",
    )(h, w, gq.reshape(1, ql), gkv.reshape(1, kvl))


def _mla_kernel(*refs, P, S, tq, G):
    if P:
        qn_ref, qr_ref, kvc_ref, krc_ref, kvl_ref, krl_ref = refs[:6]
        segs = ((0, P, kvc_ref, krc_ref), (P, S, kvl_ref, krl_ref))
    else:
        qn_ref, qr_ref, kvl_ref, krl_ref = refs[:4]
        segs = ((0, S, kvl_ref, krl_ref),)
    o_ref, ks, vs = refs[-3:]
    C = LANES

    @pl.when(pl.program_id(2) == 0)
    def _():
        for r0, n, kv_ref, kr_ref in segs:
            kr = kr_ref[...].astype(F32)
            kr2 = (kr + pltpu.roll(kr, ROPE_C, 1)).astype(BF)
            for hd in range(2 * G):
                ks[hd, r0:r0 + n, 0:C] = kv_ref[:, 2 * hd * C:(2 * hd + 1) * C]
                ks[hd, r0:r0 + n, C:2 * C] = kr2
                vs[hd, r0:r0 + n, 0:C] = kv_ref[:, (2 * hd + 1) * C:(2 * hd + 2) * C]
                vs[hd, r0:r0 + n, C:2 * C] = jnp.ones((n, C), BF)

    c = ((NOPE_C + ROPE_C) ** -0.5) * LOG2E
    chains = []
    sub = min(ATT_ROWS, tq)
    for r0 in range(0, tq, sub):
        rows = slice(r0, r0 + sub)
        for p in range(G):
            qr = qr_ref[rows, p * C:(p + 1) * C].astype(F32)
            lane = lax.broadcasted_iota(jnp.int32, qr.shape, 1)
            for hh in range(2):
                hd = 2 * p + hh
                keep = (lane < ROPE_C) if hh == 0 else (lane >= ROPE_C)
                qf = jnp.concatenate([qn_ref[rows, hd * C:(hd + 1) * C], jnp.where(keep, qr, 0.0).astype(BF)],
                                     axis=1)
                s = lax.dot_general(qf, ks[hd], (((1,), (1,)), ((), ())), preferred_element_type=F32)
                chains.append((rows, hd, s))
    for rows, hd, s in chains:
        e = jnp.exp2((s - jnp.max(s, axis=-1, keepdims=True)) * c).astype(BF)
        oe = jnp.dot(e, vs[hd], preferred_element_type=F32)
        o_ref[rows, hd * C:(hd + 1) * C] = (oe[:, 0:C] / oe[:, C:2 * C]).astype(o_ref.dtype)


def _mla_attn(qn, qr, qr_row0, kv_lat, kr_lat, kr_row0, row0, cache, nb, S, out_buf, out_shape):
    C = LANES
    tq = _pick(S, 512, SUBLANES)
    nq = S // tq
    G = math.gcd(H_C // 2, SHORT_SEQ_HEADS) if nq == 1 else 1
    P = cache[0].shape[0] // nb if cache is not None else 0
    r_q, r_qr, r_kv, r_kr = row0 // tq, qr_row0 // tq, row0 // S, kr_row0 // S
    in_specs = [pl.BlockSpec((tq, 2 * C * G), lambda b, h, i: (r_q + b * nq + i, h)),
                pl.BlockSpec((tq, C * G), lambda b, h, i: (r_qr + b * nq + i, h))]
    args = [qn, qr]
    if P:
        in_specs += [pl.BlockSpec((P, 4 * C * G), lambda b, h, i: (b, h)),
                     pl.BlockSpec((P, C), lambda b, h, i: (b, 0))]
        args += [cache[0], cache[1]]
    in_specs += [pl.BlockSpec((S, 4 * C * G), lambda b, h, i: (r_kv + b, h)),
                 pl.BlockSpec((S, C), lambda b, h, i: (r_kr + b, 0))]
    args += [kv_lat, kr_lat]
    aliases = _in_place(in_specs, args, out_buf)
    return pl.pallas_call(
        functools.partial(_mla_kernel, P=P, S=S, tq=tq, G=G),
        grid=(nb, H_C // (2 * G), nq),
        in_specs=in_specs,
        out_specs=pl.BlockSpec((tq, 2 * C * G), lambda b, h, i: (r_q + b * nq + i, h)),
        out_shape=jax.ShapeDtypeStruct(out_shape, BF),
        input_output_aliases=aliases,
        scratch_shapes=[pltpu.VMEM((2 * G, P + S, 2 * C), BF)] * 2,
        compiler_params=_cparams("parallel", "parallel", "arbitrary"),
        name="mla_attn",
    )(*args)


def kernel(x_prompt, x_sample, c, c_ctx, state_rglru, cache_dk, cache_dv, cache_ckv, cache_krope,
           mod_w, mod_b, norm_g, final_g, ffn_wg, ffn_wu, ffn_wd,
           ev_w_in, ev_conv_w, ev_conv_b, ev_wa, ev_ba, ev_wx, ev_bx, ev_lam, ev_lq, ev_lk,
           ev_subln_g, ev_w_out, od_w_in, od_qnorm_g, od_w_uq, od_kvnorm_g, od_w_ukv, od_w_out):
    assert NOPE_C == LANES and V_C == LANES and 2 * DK_B == LANES and 2 * ROPE_C == LANES
    NBP, SP, D = x_prompt.shape
    NB, DS, _ = x_sample.shape
    PAST = cache_dk.shape[2]
    L = mod_w.shape[0]
    MP, MS = NBP * SP, NB * DS
    tok = _Tokens(MP, DS, NB, 1 + NB)
    M = tok.m
    d_rnn = ev_lam.shape[-1]
    d_ff = ffn_wg.shape[-1]
    ql, kvl = od_qnorm_g.shape[-1], od_kvnorm_g.shape[-1]
    assert d_rnn // H_A == LANES

    n_cond = -(-(1 + NB) // SUBLANES) * SUBLANES
    cond = jnp.concatenate([c_ctx[None, :], c, jnp.zeros((n_cond - 1 - NB, D), F32)], axis=0)
    mod = _modulation(cond, mod_w, mod_b)
    modt = mod[:, :1 + NB].reshape(L * (1 + NB) * N_MOD, 1, D)
    normt = norm_g.reshape(L * 3, 1, D)

    fpad = -(-d_ff // FF_ALIGN) * FF_ALIGN - d_ff
    wg = ffn_wg.astype(BF)
    wu = ffn_wu.astype(BF)
    wd = jnp.pad(ffn_wd, ((0, 0), (0, 0), (0, fpad), (0, 0))).astype(BF)

    cos, sin = _rope_tables(DS)
    tm = tok.tile(1024)
    new = {}
    parts = [(x_prompt.reshape(MP, D), 0), (x_sample.reshape(MS, D), MP)]

    for l in range(L):
        x = _ffn(parts, normt, modt, tok, l, 0, wg, wu, wd, (l, 0))
        h = _prenorm([(x, 0)], normt, modt, tok, l, 1)
        if l % 2 == 0:
            e = l // 2
            lam_init = 0.8 - 0.6 * math.exp(-0.3 * l)
            nh = H_B * LANES
            w_in = ev_w_in[e].astype(BF)
            w_qk = w_in[:, 2 * d_rnn:2 * d_rnn + 2 * nh]
            xg = _mm(h, w_in[:, :2 * d_rnn], F32, tm=tm, name="ev_in_rec")
            v = _mm(h, w_in[:, 2 * d_rnn + 2 * nh:], F32, tm=tm, name="ev_in_v")
            qk_p = _mm(h, w_qk, F32, tm=tm, rows=MP, name="ev_in_qk")
            qk_s = _mm(h, w_qk, BF, tm=tm, row0=MP, rows=MS, rope=(cos, sin, DS), name="ev_in_qk_rope")
            wcat = jnp.concatenate([ev_wa[e, 0], ev_wx[e, 0], ev_wa[e, 1], ev_wx[e, 1]], axis=-1).astype(BF)
            bcat = jnp.concatenate([t.reshape(H_A, 1, LANES) for t in
                                    (ev_ba[e, 0], ev_bx[e, 0], ev_ba[e, 1], ev_bx[e, 1])], axis=-1)
            mix_shape = (M, d_rnn + nh)
            rg = functools.partial(_rglru, xg, d_rnn=d_rnn, conv_w=ev_conv_w[e], conv_b=ev_conv_b[e],
                                   wcat=wcat, bcat=bcat, lam=ev_lam[e], out_shape=mix_shape)
            y_in, hfin = rg(row0=0, nb=NBP, S=SP, h0=jnp.zeros((NBP, 2, d_rnn), F32), out_buf=None)
            y_in, _ = rg(row0=MP, nb=NB, S=DS, h0=state_rglru[:, e], out_buf=y_in)
            y_in = _dattn(qk_p, 0, 0, qk_p, 0, nh, v, 0, 0, None, NBP, SP,
                          ev_lq[e], ev_lk[e], ev_subln_g[e], lam_init, y_in, 0, d_rnn)
            cache = (cache_dk[:, e].reshape(NB, PAST, nh), cache_dv[:, e].reshape(NB, PAST, nh))
            y_in = _dattn(qk_s, 0, 0, qk_s, 0, nh, v, MP, 0, cache, NB, DS,
                          ev_lq[e], ev_lk[e], ev_subln_g[e], lam_init, y_in, MP, d_rnn)
            w_out = ev_w_out[e].astype(BF)
            new.setdefault('rec', []).append(hfin)
            new.setdefault('dk', []).append(qk_p[:, nh:].reshape(NBP, SP, H_B, 2 * DK_B))
            new.setdefault('dv', []).append(v[:MP].reshape(NBP, SP, H_B, 2 * DK_B))
        else:
            o = l // 2
            w_in = jnp.pad(od_w_in[o].astype(BF), ((0, 0), (0, LANES - ROPE_C)))
            cqn, ckvn, kr = _mla_in(h, w_in, od_qnorm_g[o], od_kvnorm_g[o], ql, kvl, tok.tile(512))
            w_uq = od_w_uq[o].astype(BF).reshape(ql, H_C, NOPE_C + ROPE_C)
            qn = _mm(cqn, w_uq[:, :, :NOPE_C].reshape(ql, H_C * NOPE_C), BF, tm=tm, name="uq_nope")
            w_uq_r = w_uq[:, :, NOPE_C:].reshape(ql, H_C * ROPE_C)
            qr_p = _mm(cqn, w_uq_r, BF, tm=tm, rows=MP, name="uq_rope")
            qr_s = _mm(cqn, w_uq_r, BF, tm=tm, row0=MP, rows=MS, rope=(cos, sin, DS), name="uq_rope_rot")
            w_ukv = od_w_ukv[o].astype(BF)
            kv = _mm(ckvn, w_ukv, BF, tm=tm, name="ukv")
            mix_shape = (M, H_C * V_C)
            y_in = _mla_attn(qn, qr_p, 0, kv, kr, 0, 0, None, NBP, SP, None, mix_shape)
            kr_s = _rope(kr, cos, sin, MP, MS, 0, LANES, DS)
            ckv_c = cache_ckv[:, o].reshape(NB * PAST, kvl)
            kv_c = _mm(ckv_c, w_ukv, BF, tm=_pick(NB * PAST, 1024, SUBLANES), name="ukv_ctx")
            kr_c = jnp.pad(cache_krope[:, o].reshape(NB * PAST, ROPE_C), ((0, 0), (0, LANES - ROPE_C)))
            y_in = _mla_attn(qn, qr_s, 0, kv, kr_s, 0, MP, (kv_c, kr_c), NB, DS, y_in, mix_shape)
            w_out = od_w_out[o].astype(BF)
            new.setdefault('ckv', []).append(ckvn[:MP].reshape(NBP, SP, kvl))
            new.setdefault('kr', []).append(kr[:MP, :ROPE_C].reshape(NBP, SP, ROPE_C))
        x = _mm_residual(y_in, w_out, (), [(x, 0)], modt, tok, l, 5, 1.0, tm=tm, tn=_pick(D, 1024, LANES),
                         name="mix_out")
        x = _ffn([(x, 0)], normt, modt, tok, l, 2, wg, wu, wd, (l, 1))
        parts = [(x, 0)]

    y_prompt = _final_norm(x, final_g, 0, MP).reshape(NBP, SP, D)
    y_sample = _final_norm(x, final_g, MP, MS).reshape(NB, DS, D)
    return (y_prompt, y_sample, jnp.stack(new['rec'], axis=1), jnp.stack(new['dk'], axis=1),
            jnp.stack(new['dv'], axis=1), jnp.stack(new['ckv'], axis=1), jnp.stack(new['kr'], axis=1))
```

```python
import functools
import math

import jax
import jax.numpy as jnp
from jax import lax
from jax.experimental import pallas as pl
from jax.experimental.pallas import tpu as pltpu

BF = jnp.bfloat16
F32 = jnp.float32

GRID_W = 64
EPS = 1e-6
ROPE_THETA = 10000.0
N_MOD = 9
H_A = 16
CONV_W = 4
LRU_C = 8.0
H_B = 16
DK_B = 64
H_C = 32
NOPE_C = 128
ROPE_C = 64
V_C = 128

LANES = 128
SUBLANES = 8
VMEM_LIMIT = 56 * 1024 * 1024
FF_ALIGN = 1024
LOG2E = math.log2(math.e)
ATT_ROWS = 256
DATT_ROWS = 128
SHORT_SEQ_HEADS = 4
SCAN_UNROLL = 4


def _cparams(*sem):
    return pltpu.CompilerParams(dimension_semantics=sem, vmem_limit_bytes=VMEM_LIMIT)


def _pick(n, pref, mult):
    best = None
    d = mult
    while d <= min(n, pref):
        if n % d == 0:
            best = d
        d += mult
    return n if best is None else best


def _sigmoid(x):
    return 1.0 / (1.0 + jnp.exp2(x * (-LOG2E)))


def _rms(x):
    return x * lax.rsqrt(jnp.mean(x * x, axis=-1, keepdims=True) + EPS)


def _in_place(in_specs, args, out_buf):
    if out_buf is None:
        return {}
    in_specs.append(pl.BlockSpec(memory_space=pl.ANY))
    args.append(out_buf)
    return {len(args) - 1: 0}


def _mod_kernel(c_ref, w_ref, b_ref, o_ref):
    k = pl.program_id(2)
    c = c_ref[...]
    s = (c * _sigmoid(c)).astype(BF)
    part = jnp.dot(s, w_ref[...].astype(BF), preferred_element_type=F32)

    @pl.when(k == 0)
    def _():
        o_ref[...] = part + b_ref[...]

    @pl.when(k > 0)
    def _():
        o_ref[...] += part


def _modulation(cond, mod_w, mod_b):
    R, D = cond.shape
    L, _, N = mod_w.shape
    tn = _pick(N, 2048, LANES)
    tk = _pick(D, 1024, LANES)
    return pl.pallas_call(
        _mod_kernel,
        grid=(L, N // tn, D // tk),
        in_specs=[
            pl.BlockSpec((R, tk), lambda l, n, k: (0, k)),
            pl.BlockSpec((None, tk, tn), lambda l, n, k: (l, k, n)),
            pl.BlockSpec((None, 1, tn), lambda l, n, k: (l, 0, n)),
        ],
        out_specs=pl.BlockSpec((None, R, tn), lambda l, n, k: (l, 0, n)),
        out_shape=jax.ShapeDtypeStruct((L, R, N), F32),
        compiler_params=_cparams("parallel", "parallel", "arbitrary"),
        name="modulation",
    )(cond, mod_w, mod_b.reshape(L, 1, N))


class _Tokens:
    def __init__(self, mp, ds, nb, n_rows):
        self.mp, self.ds, self.nb, self.n_rows = mp, ds, nb, n_rows
        self.m = mp + ds * nb

    def tile(self, pref):
        return _pick(math.gcd(self.mp, self.ds), pref, SUBLANES)

    def mod_index(self, layer, chunk, tm):
        mp, ds, n_rows = self.mp, self.ds, self.n_rows

        def f(i):
            r = jnp.where(i * tm < mp, 0, 1 + (i * tm - mp) // ds)
            return (layer * n_rows + r) * N_MOD + chunk
        return f


def _prenorm_kernel(x_ref, g_ref, sh_ref, sc_ref, *rest):
    o_ref = rest[-1]
    y = _rms(x_ref[...]) * g_ref[...]
    o_ref[...] = (y * (1.0 + sc_ref[...]) + sh_ref[...]).astype(o_ref.dtype)


def _prenorm(parts, normt, modt, tok, layer, sub):
    h = None
    for x, row0 in parts:
        rows, D = x.shape
        tm = tok.tile(256)
        off = row0 // tm
        sh = tok.mod_index(layer, 3 * sub, tm)
        sc = tok.mod_index(layer, 3 * sub + 1, tm)
        in_specs = [
            pl.BlockSpec((tm, D), lambda i: (i, 0)),
            pl.BlockSpec((None, 1, D), lambda i: (layer * 3 + sub, 0, 0)),
            pl.BlockSpec((None, 1, D), lambda i: (sh(off + i), 0, 0)),
            pl.BlockSpec((None, 1, D), lambda i: (sc(off + i), 0, 0)),
        ]
        args = [x, normt, modt, modt]
        aliases = _in_place(in_specs, args, h)
        h = pl.pallas_call(
            _prenorm_kernel,
            grid=(rows // tm,),
            in_specs=in_specs,
            out_specs=pl.BlockSpec((tm, D), lambda i: (off + i, 0)),
            out_shape=jax.ShapeDtypeStruct((tok.m, D), BF),
            input_output_aliases=aliases,
            compiler_params=_cparams("parallel"),
            name="prenorm",
        )(*args)
    return h


def _final_norm_kernel(x_ref, g_ref, o_ref):
    o_ref[...] = _rms(x_ref[...]) * g_ref[...]


def _final_norm(x, g, row0, rows):
    D = x.shape[1]
    tm = _pick(math.gcd(row0, rows) if row0 else rows, 256, SUBLANES)
    off = row0 // tm
    return pl.pallas_call(
        _final_norm_kernel,
        grid=(rows // tm,),
        in_specs=[
            pl.BlockSpec((tm, D), lambda i: (off + i, 0)),
            pl.BlockSpec((1, D), lambda i: (0, 0)),
        ],
        out_specs=pl.BlockSpec((tm, D), lambda i: (i, 0)),
        out_shape=jax.ShapeDtypeStruct((rows, D), F32),
        compiler_params=_cparams("parallel"),
        name="final_norm",
    )(x, g.reshape(1, D))


def _rope_lanes(x, cos, sin, first):
    partner = jnp.where(first, pltpu.roll(x, LANES - 16, 1), pltpu.roll(x, 16, 1))
    return x * cos + partner * sin


def _mm_kernel(*refs, nk, residual, coef, rope=False):
    a_ref, w_ref = refs[:2]
    if residual:
        r_ref, g_ref = refs[2:4]
    if rope:
        cos_ref, sin_ref = refs[2:4]
    o_ref, acc_ref = (refs[-2], refs[-1]) if nk > 1 else (refs[-1], None)

    def finish(acc):
        if residual:
            o_ref[...] = r_ref[...] + (coef * g_ref[...]) * acc
        elif rope:
            cos, sin = cos_ref[...], sin_ref[...]
            first = (lax.broadcasted_iota(jnp.int32, cos.shape, 1) % 32) < 16
            for c in range(acc.shape[1] // LANES):
                cols = slice(c * LANES, (c + 1) * LANES)
                o_ref[:, cols] = _rope_lanes(acc[:, cols], cos, sin, first).astype(o_ref.dtype)
        else:
            o_ref[...] = acc.astype(o_ref.dtype)

    if nk == 1:
        finish(jnp.dot(a_ref[...].astype(BF), w_ref[...], preferred_element_type=F32))
        return
    k = pl.program_id(2)

    @pl.when(k == 0)
    def _():
        acc_ref[...] = jnp.zeros_like(acc_ref)

    acc_ref[...] += jnp.dot(a_ref[...].astype(BF), w_ref[...], preferred_element_type=F32)

    @pl.when(k == nk - 1)
    def _():
        finish(acc_ref[...])


def _mm(a, w, out_dtype, *, tm, tn=None, tk=None, row0=0, rows=None, rope=None, name="mm"):
    K, N = w.shape
    tn = _pick(N, 1024, LANES) if tn is None else tn
    rows = a.shape[0] if rows is None else rows
    tk = K if tk is None else tk
    nk = K // tk
    off = row0 // tm
    in_specs = [
        pl.BlockSpec((tm, tk), lambda i, j, k: (off + i, k)),
        pl.BlockSpec((tk, tn), lambda i, j, k: (k, j)),
    ]
    args = [a, w]
    if rope is not None:
        cos, sin, ds = rope
        per = ds // tm
        in_specs += [pl.BlockSpec((tm, LANES), lambda i, j, k: (i % per, 0))] * 2
        args += [cos, sin]
    return pl.pallas_call(
        functools.partial(_mm_kernel, nk=nk, residual=False, coef=None, rope=rope is not None),
        grid=(rows // tm, N // tn, nk),
        in_specs=in_specs,
        out_specs=pl.BlockSpec((tm, tn), lambda i, j, k: (i, j)),
        out_shape=jax.ShapeDtypeStruct((rows, N), out_dtype),
        scratch_shapes=[pltpu.VMEM((tm, tn), F32)] if nk > 1 else [],
        compiler_params=_cparams("parallel", "parallel", "arbitrary"),
        name=name,
    )(*args)


def _mm_residual(a, w, w_lead, parts, modt, tok, layer, chunk, coef, *, tm, tn, tk=None, name="mm_res"):
    K, N = w.shape[-2:]
    tk = K if tk is None else tk
    nk = K // tk
    gate = tok.mod_index(layer, chunk, tm)
    out = None
    for x, row0 in parts:
        off = row0 // tm
        in_specs = [
            pl.BlockSpec((tm, tk), lambda i, j, k: (off + i, k)),
            pl.BlockSpec((None,) * len(w_lead) + (tk, tn), lambda i, j, k: w_lead + (k, j)),
            pl.BlockSpec((tm, tn), lambda i, j, k: (i, j)),
            pl.BlockSpec((None, 1, tn), lambda i, j, k: (gate(off + i), 0, j)),
        ]
        args = [a, w, x, modt]
        aliases = _in_place(in_specs, args, out)
        out = pl.pallas_call(
            functools.partial(_mm_kernel, nk=nk, residual=True, coef=coef),
            grid=(x.shape[0] // tm, N // tn, nk),
            in_specs=in_specs,
            out_specs=pl.BlockSpec((tm, tn), lambda i, j, k: (off + i, j)),
            out_shape=jax.ShapeDtypeStruct((tok.m, N), F32),
            input_output_aliases=aliases,
            scratch_shapes=[pltpu.VMEM((tm, tn), F32)] if nk > 1 else [],
            compiler_params=_cparams("parallel", "parallel", "arbitrary"),
            name=name,
        )(*args)
    return out


def _ffn_up_kernel(h_ref, wg_ref, wu_ref, o_ref, *, d_ff):
    h = h_ref[...]
    g = jnp.dot(h, wg_ref[...], preferred_element_type=F32)
    u = jnp.dot(h, wu_ref[...], preferred_element_type=F32)
    tf = o_ref.shape[1]
    col = pl.program_id(1) * tf + lax.broadcasted_iota(jnp.int32, (1, tf), 1)
    o_ref[...] = jnp.where(col < d_ff, (g * _sigmoid(g)) * u, 0.0).astype(o_ref.dtype)


def _ffn_up(h, wg, wu, w_lead, fp, *, tm, tf):
    M, D = h.shape
    d_ff = wg.shape[-1]
    last = (d_ff - 1) // tf
    wspec = pl.BlockSpec((None,) * len(w_lead) + (D, tf), lambda i, j: w_lead + (0, jnp.minimum(j, last)))
    return pl.pallas_call(
        functools.partial(_ffn_up_kernel, d_ff=d_ff),
        grid=(M // tm, fp // tf),
        in_specs=[pl.BlockSpec((tm, D), lambda i, j: (i, 0)), wspec, wspec],
        out_specs=pl.BlockSpec((tm, tf), lambda i, j: (i, j)),
        out_shape=jax.ShapeDtypeStruct((M, fp), BF),
        compiler_params=_cparams("parallel", "parallel"),
        name="ffn_up",
    )(h, wg, wu)


def _ffn(parts, normt, modt, tok, layer, sub, wg, wu, wd, w_lead):
    h = _prenorm(parts, normt, modt, tok, layer, sub)
    fp, D = wd.shape[-2:]
    tm = tok.tile(1024)
    a = _ffn_up(h, wg, wu, w_lead, fp, tm=tm, tf=_pick(fp, 512, LANES))
    tk = _pick(fp, max(fp // 4, LANES), LANES)
    return _mm_residual(a, wd, w_lead, parts, modt, tok, layer, 3 * sub + 2, 0.5,
                        tm=tm, tn=_pick(D, 1024, LANES), tk=tk, name="ffn_down")


def _gelu_tanh(x):
    return x * (0.5 * (1.0 + jnp.tanh(math.sqrt(2.0 / math.pi) * (x + 0.044715 * (x * x * x)))))


def _softplus(x):
    return jnp.maximum(x, 0.0) + jnp.log1p(jnp.exp(-jnp.abs(x)))


def _rglru_kernel(*refs, S):
    xa_ref, ga_ref, cw_ref, cb_ref, w_ref, b_ref, lam_ref, h0_ref = refs[:8]
    y_ref, hfin_ref, pad_ref, af_ref, bf_ref, ab_ref, bb_ref = refs[-7:]
    C = LANES
    P0 = SUBLANES
    pad_ref[0:P0, :] = jnp.zeros((P0, C), F32)
    pad_ref[P0 + S:P0 + S + P0, :] = jnp.zeros((P0, C), F32)
    pad_ref[P0:P0 + S, :] = xa_ref[...]
    cw = cw_ref[...]
    xc = jnp.broadcast_to(cb_ref[...], (S, C))
    for j in range(CONV_W):
        xc = xc + pad_ref[pl.ds(P0 - 2 + j, S), :] * cw[j:j + 1, :]

    gates = jnp.dot(xc.astype(BF), w_ref[...], preferred_element_type=F32) + b_ref[...]
    lam = lam_ref[...]
    for d, (a_ref, b_ref_) in enumerate(((af_ref, bf_ref), (ab_ref, bb_ref))):
        r = _sigmoid(gates[:, (2 * d) * C:(2 * d + 1) * C])
        i = _sigmoid(gates[:, (2 * d + 1) * C:(2 * d + 2) * C])
        log_a = (-LRU_C * r) * _softplus(-lam[d:d + 1, :])
        a_ref[...] = jnp.exp(log_a)
        th = jnp.tanh(log_a)
        one_minus_a2 = (-2.0 * th) / (1.0 - th)
        b_ref_[...] = (jnp.sqrt(one_minus_a2) * i) * xc

    rows = lax.broadcasted_iota(jnp.int32, (SUBLANES, C), 0)

    def tile_scan(a, b, down):
        for k in (1, 2, 4):
            if down:
                keep = rows >= k
                shift = k
            else:
                keep = rows < SUBLANES - k
                shift = SUBLANES - k
            a1 = jnp.where(keep, pltpu.roll(a, shift, 0), 1.0)
            b1 = jnp.where(keep, pltpu.roll(b, shift, 0), 0.0)
            b = a * b1 + b
            a = a * a1
        return a, b

    nt = S // SUBLANES
    unroll = SCAN_UNROLL if nt % SCAN_UNROLL == 0 else 1

    def body(tu, carry):
        hf, hb = carry
        tiles = []
        for u in range(unroll):
            t = tu * unroll + u
            r0 = pl.multiple_of(t * SUBLANES, SUBLANES)
            r1 = pl.multiple_of((nt - 1 - t) * SUBLANES, SUBLANES)
            fwd = tile_scan(af_ref[pl.ds(r0, SUBLANES), :], bf_ref[pl.ds(r0, SUBLANES), :], True)
            bwd = tile_scan(ab_ref[pl.ds(r1, SUBLANES), :], bb_ref[pl.ds(r1, SUBLANES), :], False)
            tiles.append((r0, fwd, r1, bwd))
        for r0, (a, b), r1, (a2, b2) in tiles:
            h = a * hf + b
            bf_ref[pl.ds(r0, SUBLANES), :] = h
            hf = jnp.broadcast_to(h[SUBLANES - 1:SUBLANES, :], (SUBLANES, C))
            g = a2 * hb + b2
            bb_ref[pl.ds(r1, SUBLANES), :] = g
            hb = jnp.broadcast_to(g[0:1, :], (SUBLANES, C))
        return hf, hb

    h0 = h0_ref[...]
    hf, hb = lax.fori_loop(0, nt // unroll, body, (jnp.broadcast_to(h0[0:1, :], (SUBLANES, C)),
                                                    jnp.broadcast_to(h0[1:2, :], (SUBLANES, C))))
    hfin_ref[0:1, :] = hf[0:1, :]
    hfin_ref[1:2, :] = hb[0:1, :]
    y_ref[...] = (_gelu_tanh(ga_ref[...]) * (bf_ref[...] + bb_ref[...])).astype(y_ref.dtype)


def _rglru(z, row0, nb, S, d_rnn, conv_w, conv_b, wcat, bcat, lam, h0, out_buf, out_shape):
    C = LANES
    ncb = d_rnn // C
    roff = row0 // S
    in_specs = [
        pl.BlockSpec((S, C), lambda b, c: (roff + b, c)),
        pl.BlockSpec((S, C), lambda b, c: (roff + b, ncb + c)),
        pl.BlockSpec((CONV_W, C), lambda b, c: (0, c)),
        pl.BlockSpec((1, C), lambda b, c: (0, c)),
        pl.BlockSpec((None, C, 4 * C), lambda b, c: (c, 0, 0)),
        pl.BlockSpec((None, 1, 4 * C), lambda b, c: (c, 0, 0)),
        pl.BlockSpec((2, C), lambda b, c: (0, c)),
        pl.BlockSpec((None, 2, C), lambda b, c: (b, 0, c)),
    ]
    args = [z, z, conv_w, conv_b.reshape(1, d_rnn), wcat, bcat, lam, h0]
    aliases = _in_place(in_specs, args, out_buf)
    return pl.pallas_call(
        functools.partial(_rglru_kernel, S=S),
        grid=(nb, ncb),
        in_specs=in_specs,
        out_specs=[
            pl.BlockSpec((S, C), lambda b, c: (roff + b, c)),
            pl.BlockSpec((None, 2, C), lambda b, c: (b, 0, c)),
        ],
        out_shape=[
            jax.ShapeDtypeStruct(out_shape, BF),
            jax.ShapeDtypeStruct((nb, 2, d_rnn), F32),
        ],
        input_output_aliases=aliases,
        scratch_shapes=[pltpu.VMEM((S + 2 * SUBLANES, C), F32)] + [pltpu.VMEM((S, C), F32)] * 4,
        compiler_params=_cparams("parallel", "parallel"),
        name="rglru",
    )(*args)


def _rope_tables(n_tok):
    t = jnp.arange(n_tok, dtype=jnp.int32)
    row, col = t // GRID_W, t % GRID_W
    n = 16
    inv = ROPE_THETA ** (-jnp.arange(n, dtype=F32) / n)
    ang_r = row.astype(F32)[:, None] * inv
    ang_c = col.astype(F32)[:, None] * inv

    def grp(ang):
        c, s = jnp.cos(ang), jnp.sin(ang)
        return jnp.concatenate([c, c], -1), jnp.concatenate([-s, s], -1)
    cr, sr = grp(ang_r)
    cc, sc = grp(ang_c)
    return jnp.concatenate([cr, cc, cr, cc], -1), jnp.concatenate([sr, sc, sr, sc], -1)


def _rope_kernel(x_ref, cos_ref, sin_ref, o_ref):
    cos = cos_ref[...]
    sin = sin_ref[...]
    first = (lax.broadcasted_iota(jnp.int32, cos.shape, 1) % 32) < 16
    for c in range(x_ref.shape[1] // LANES):
        x = x_ref[:, c * LANES:(c + 1) * LANES].astype(F32)
        partner = jnp.where(first, pltpu.roll(x, LANES - 16, 1), pltpu.roll(x, 16, 1))
        o_ref[:, c * LANES:(c + 1) * LANES] = (x * cos + partner * sin).astype(o_ref.dtype)


def _rope(x, cos, sin, row0, rows, col0, width, ds):
    tm = _pick(ds, 256, SUBLANES)
    tw = _pick(width, 1024, LANES)
    roff, coff, per = row0 // tm, col0 // tw, ds // tm
    return pl.pallas_call(
        _rope_kernel,
        grid=(rows // tm, width // tw),
        in_specs=[
            pl.BlockSpec((tm, tw), lambda i, j: (roff + i, coff + j)),
            pl.BlockSpec((tm, LANES), lambda i, j: (i % per, 0)),
            pl.BlockSpec((tm, LANES), lambda i, j: (i % per, 0)),
        ],
        out_specs=pl.BlockSpec((tm, tw), lambda i, j: (i, j)),
        out_shape=jax.ShapeDtypeStruct((rows, width), BF),
        compiler_params=_cparams("parallel", "parallel"),
        name="rope",
    )(x, cos, sin)


def _dattn_kernel(*refs, P, S, tq, G, lam_init):
    if P:
        q_ref, kc_ref, vc_ref, kl_ref, vl_ref, lq_ref, lk_ref, g_ref = refs[:8]
    else:
        q_ref, kl_ref, vl_ref, lq_ref, lk_ref, g_ref = refs[:6]
    o_ref, kk, vv = refs[-3:]
    C = 2 * DK_B

    @pl.when(pl.program_id(2) == 0)
    def _():
        if P:
            kk[0:P, :] = kc_ref[...].astype(BF)
            vv[0:P, :] = vc_ref[...].astype(BF)
        kk[P:P + S, :] = kl_ref[...].astype(BF)
        vv[P:P + S, :] = vl_ref[...].astype(BF)

    el = jnp.exp(jnp.sum(lq_ref[...] * lk_ref[...], axis=-1, keepdims=True))
    lam = el[0:1, :] - el[1:2, :] + lam_init
    sub = min(DATT_ROWS, tq)
    chains = []
    for g in range(G):
        cols = slice(g * C, (g + 1) * C)
        for r0 in range(0, tq, sub):
            q = q_ref[r0:r0 + sub, cols].astype(F32)
            lane = lax.broadcasted_iota(jnp.int32, q.shape, 1)
            qq = jnp.concatenate([jnp.where(lane < DK_B, q, 0.0), jnp.where(lane >= DK_B, q, 0.0)],
                                 axis=0).astype(BF)
            s = lax.dot_general(qq, kk[:, cols], (((1,), (1,)), ((), ())), preferred_element_type=F32)
            chains.append((cols, r0, s))
    for cols, r0, s in chains:
        e = jnp.exp2((s - jnp.max(s, axis=-1, keepdims=True)) * ((DK_B ** -0.5) * LOG2E))
        l = jnp.sum(e, axis=-1, keepdims=True)
        a = e[:sub, :] - e[sub:, :] * (lam * l[:sub, :] / l[sub:, :])
        o = jnp.dot(a.astype(BF), vv[:, cols], preferred_element_type=F32) / l[:sub, :]
        o_ref[r0:r0 + sub, cols] = ((_rms(o) * g_ref[...]) * (1.0 - lam_init)).astype(o_ref.dtype)


def _dattn(q, qrow0, qcol0, k_lat, krow0, kcol0, v_lat, vrow0, vcol0, cache, nb, S, lq, lk, g, lam_init,
           out_buf, orow0, ocol0):
    C = 2 * DK_B
    tq = _pick(S, 512, SUBLANES)
    nq = S // tq
    G = math.gcd(H_B, SHORT_SEQ_HEADS) if nq == 1 else 1
    W = C * G
    P = cache[0].shape[1] if cache is not None else 0
    qr, kr, vr, orr = qrow0 // tq, krow0 // S, vrow0 // S, orow0 // tq
    qc, kc, vc, oc = qcol0 // W, kcol0 // W, vcol0 // W, ocol0 // W
    in_specs = [pl.BlockSpec((tq, W), lambda b, h, i: (qr + b * nq + i, qc + h))]
    args = [q]
    if P:
        in_specs += [pl.BlockSpec((None, P, W), lambda b, h, i: (b, 0, h)),
                     pl.BlockSpec((None, P, W), lambda b, h, i: (b, 0, h))]
        args += [cache[0], cache[1]]
    in_specs += [
        pl.BlockSpec((S, W), lambda b, h, i: (kr + b, kc + h)),
        pl.BlockSpec((S, W), lambda b, h, i: (vr + b, vc + h)),
        pl.BlockSpec((2, DK_B), lambda b, h, i: (0, 0)),
        pl.BlockSpec((2, DK_B), lambda b, h, i: (0, 0)),
        pl.BlockSpec((1, C), lambda b, h, i: (0, 0)),
    ]
    args += [k_lat, v_lat, lq, lk, g.reshape(1, C)]
    aliases = _in_place(in_specs, args, out_buf)
    return pl.pallas_call(
        functools.partial(_dattn_kernel, P=P, S=S, tq=tq, G=G, lam_init=lam_init),
        grid=(nb, H_B // G, nq),
        in_specs=in_specs,
        out_specs=pl.BlockSpec((tq, W), lambda b, h, i: (orr + b * nq + i, oc + h)),
        out_shape=jax.ShapeDtypeStruct(out_buf.shape, BF),
        input_output_aliases=aliases,
        scratch_shapes=[pltpu.VMEM((P + S, W), BF), pltpu.VMEM((P + S, W), BF)],
        compiler_params=_cparams("parallel", "parallel", "arbitrary"),
        name="diff_attn",
    )(*args)


def _mla_in_kernel(h_ref, w_ref, gq_ref, gkv_ref, cq_ref, ckv_ref, kr_ref, *, ql, kvl):
    z = jnp.dot(h_ref[...], w_ref[...], preferred_element_type=F32)
    cq_ref[...] = (_rms(z[:, :ql]) * gq_ref[...]).astype(cq_ref.dtype)
    ckv_ref[...] = _rms(z[:, ql:ql + kvl]) * gkv_ref[...]
    kr_ref[...] = z[:, ql + kvl:]


def _mla_in(h, w, gq, gkv, ql, kvl, tm):
    M, D = h.shape
    N = w.shape[1]
    return pl.pallas_call(
        functools.partial(_mla_in_kernel, ql=ql, kvl=kvl),
        grid=(M // tm,),
        in_specs=[
            pl.BlockSpec((tm, D), lambda i: (i, 0)),
            pl.BlockSpec((D, N), lambda i: (0, 0)),
            pl.BlockSpec((1, ql), lambda i: (0, 0)),
            pl.BlockSpec((1, kvl), lambda i: (0, 0)),
        ],
        out_specs=[
            pl.BlockSpec((tm, ql), lambda i: (i, 0)),
            pl.BlockSpec((tm, kvl), lambda i: (i, 0)),
            pl.BlockSpec((tm, LANES), lambda i: (i, 0)),
        ],
        out_shape=[
            jax.ShapeDtypeStruct((M, ql), BF),
            jax.ShapeDtypeStruct((M, kvl), F32),
            jax.ShapeDtypeStruct((M, LANES), F32),
        ],
        compiler_params=_cparams("parallel"),
        name="mla_in",
    )(h, w, gq.reshape(1, ql), gkv.reshape(1, kvl))


def _mla_kernel(*refs, P, S, tq, G):
    if P:
        qn_ref, qr_ref, kvc_ref, krc_ref, kvl_ref, krl_ref = refs[:6]
        segs = ((0, P, kvc_ref, krc_ref), (P, S, kvl_ref, krl_ref))
    else:
        qn_ref, qr_ref, kvl_ref, krl_ref = refs[:4]
        segs = ((0, S, kvl_ref, krl_ref),)
    o_ref, ks, vs = refs[-3:]
    C = LANES

    @pl.when(pl.program_id(2) == 0)
    def _():
        for r0, n, kv_ref, kr_ref in segs:
            kr = kr_ref[...].astype(F32)
            kr2 = (kr + pltpu.roll(kr, ROPE_C, 1)).astype(BF)
            for hd in range(2 * G):
                ks[hd, r0:r0 + n, 0:C] = kv_ref[:, 2 * hd * C:(2 * hd + 1) * C]
                ks[hd, r0:r0 + n, C:2 * C] = kr2
                vs[hd, r0:r0 + n, 0:C] = kv_ref[:, (2 * hd + 1) * C:(2 * hd + 2) * C]
                vs[hd, r0:r0 + n, C:2 * C] = jnp.ones((n, C), BF)

    c = ((NOPE_C + ROPE_C) ** -0.5) * LOG2E
    chains = []
    sub = min(ATT_ROWS, tq)
    for r0 in range(0, tq, sub):
        rows = slice(r0, r0 + sub)
        for p in range(G):
            qr = qr_ref[rows, p * C:(p + 1) * C].astype(F32)
            lane = lax.broadcasted_iota(jnp.int32, qr.shape, 1)
            for hh in range(2):
                hd = 2 * p + hh
                keep = (lane < ROPE_C) if hh == 0 else (lane >= ROPE_C)
                qf = jnp.concatenate([qn_ref[rows, hd * C:(hd + 1) * C], jnp.where(keep, qr, 0.0).astype(BF)],
                                     axis=1)
                s = lax.dot_general(qf, ks[hd], (((1,), (1,)), ((), ())), preferred_element_type=F32)
                chains.append((rows, hd, s))
    for rows, hd, s in chains:
        e = jnp.exp2((s - jnp.max(s, axis=-1, keepdims=True)) * c).astype(BF)
        oe = jnp.dot(e, vs[hd], preferred_element_type=F32)
        o_ref[rows, hd * C:(hd + 1) * C] = (oe[:, 0:C] / oe[:, C:2 * C]).astype(o_ref.dtype)


def _mla_attn(qn, qr, qr_row0, kv_lat, kr_lat, kr_row0, row0, cache, nb, S, out_buf, out_shape):
    C = LANES
    tq = _pick(S, 512, SUBLANES)
    nq = S // tq
    G = math.gcd(H_C // 2, SHORT_SEQ_HEADS) if nq == 1 else 1
    P = cache[0].shape[0] // nb if cache is not None else 0
    r_q, r_qr, r_kv, r_kr = row0 // tq, qr_row0 // tq, row0 // S, kr_row0 // S
    in_specs = [pl.BlockSpec((tq, 2 * C * G), lambda b, h, i: (r_q + b * nq + i, h)),
                pl.BlockSpec((tq, C * G), lambda b, h, i: (r_qr + b * nq + i, h))]
    args = [qn, qr]
    if P:
        in_specs += [pl.BlockSpec((P, 4 * C * G), lambda b, h, i: (b, h)),
                     pl.BlockSpec((P, C), lambda b, h, i: (b, 0))]
        args += [cache[0], cache[1]]
    in_specs += [pl.BlockSpec((S, 4 * C * G), lambda b, h, i: (r_kv + b, h)),
                 pl.BlockSpec((S, C), lambda b, h, i: (r_kr + b, 0))]
    args += [kv_lat, kr_lat]
    aliases = _in_place(in_specs, args, out_buf)
    return pl.pallas_call(
        functools.partial(_mla_kernel, P=P, S=S, tq=tq, G=G),
        grid=(nb, H_C // (2 * G), nq),
        in_specs=in_specs,
        out_specs=pl.BlockSpec((tq, 2 * C * G), lambda b, h, i: (r_q + b * nq + i, h)),
        out_shape=jax.ShapeDtypeStruct(out_shape, BF),
        input_output_aliases=aliases,
        scratch_shapes=[pltpu.VMEM((2 * G, P + S, 2 * C), BF)] * 2,
        compiler_params=_cparams("parallel", "parallel", "arbitrary"),
        name="mla_attn",
    )(*args)


def kernel(x_prompt, x_sample, c, c_ctx, state_rglru, cache_dk, cache_dv, cache_ckv, cache_krope,
           mod_w, mod_b, norm_g, final_g, ffn_wg, ffn_wu, ffn_wd,
           ev_w_in, ev_conv_w, ev_conv_b, ev_wa, ev_ba, ev_wx, ev_bx, ev_lam, ev_lq, ev_lk,
           ev_subln_g, ev_w_out, od_w_in, od_qnorm_g, od_w_uq, od_kvnorm_g, od_w_ukv, od_w_out):
    assert NOPE_C == LANES and V_C == LANES and 2 * DK_B == LANES and 2 * ROPE_C == LANES
    NBP, SP, D = x_prompt.shape
    NB, DS, _ = x_sample.shape
    PAST = cache_dk.shape[2]
    L = mod_w.shape[0]
    MP, MS = NBP * SP, NB * DS
    tok = _Tokens(MP, DS, NB, 1 + NB)
    M = tok.m
    d_rnn = ev_lam.shape[-1]
    d_ff = ffn_wg.shape[-1]
    ql, kvl = od_qnorm_g.shape[-1], od_kvnorm_g.shape[-1]
    assert d_rnn // H_A == LANES

    n_cond = -(-(1 + NB) // SUBLANES) * SUBLANES
    cond = jnp.concatenate([c_ctx[None, :], c, jnp.zeros((n_cond - 1 - NB, D), F32)], axis=0)
    mod = _modulation(cond, mod_w, mod_b)
    modt = mod[:, :1 + NB].reshape(L * (1 + NB) * N_MOD, 1, D)
    normt = norm_g.reshape(L * 3, 1, D)

    fpad = -(-d_ff // FF_ALIGN) * FF_ALIGN - d_ff
    wg = ffn_wg.astype(BF)
    wu = ffn_wu.astype(BF)
    wd = jnp.pad(ffn_wd, ((0, 0), (0, 0), (0, fpad), (0, 0))).astype(BF)

    cos, sin = _rope_tables(DS)
    tm = tok.tile(1024)
    new = {}
    parts = [(x_prompt.reshape(MP, D), 0), (x_sample.reshape(MS, D), MP)]

    for l in range(L):
        x = _ffn(parts, normt, modt, tok, l, 0, wg, wu, wd, (l, 0))
        h = _prenorm([(x, 0)], normt, modt, tok, l, 1)
        if l % 2 == 0:
            e = l // 2
            lam_init = 0.8 - 0.6 * math.exp(-0.3 * l)
            nh = H_B * LANES
            w_in = ev_w_in[e].astype(BF)
            w_qk = w_in[:, 2 * d_rnn:2 * d_rnn + 2 * nh]
            xg = _mm(h, w_in[:, :2 * d_rnn], F32, tm=tm, name="ev_in_rec")
            v = _mm(h, w_in[:, 2 * d_rnn + 2 * nh:], F32, tm=tm, name="ev_in_v")
            qk_p = _mm(h, w_qk, F32, tm=tm, rows=MP, name="ev_in_qk")
            qk_s = _mm(h, w_qk, BF, tm=tm, row0=MP, rows=MS, rope=(cos, sin, DS), name="ev_in_qk_rope")
            wcat = jnp.concatenate([ev_wa[e, 0], ev_wx[e, 0], ev_wa[e, 1], ev_wx[e, 1]], axis=-1).astype(BF)
            bcat = jnp.concatenate([t.reshape(H_A, 1, LANES) for t in
                                    (ev_ba[e, 0], ev_bx[e, 0], ev_ba[e, 1], ev_bx[e, 1])], axis=-1)
            mix_shape = (M, d_rnn + nh)
            rg = functools.partial(_rglru, xg, d_rnn=d_rnn, conv_w=ev_conv_w[e], conv_b=ev_conv_b[e],
                                   wcat=wcat, bcat=bcat, lam=ev_lam[e], out_shape=mix_shape)
            y_in, hfin = rg(row0=0, nb=NBP, S=SP, h0=jnp.zeros((NBP, 2, d_rnn), F32), out_buf=None)
            y_in, _ = rg(row0=MP, nb=NB, S=DS, h0=state_rglru[:, e], out_buf=y_in)
            y_in = _dattn(qk_p, 0, 0, qk_p, 0, nh, v, 0, 0, None, NBP, SP,
                          ev_lq[e], ev_lk[e], ev_subln_g[e], lam_init, y_in, 0, d_rnn)
            cache = (cache_dk[:, e].reshape(NB, PAST, nh), cache_dv[:, e].reshape(NB, PAST, nh))
            y_in = _dattn(qk_s, 0, 0, qk_s, 0, nh, v, MP, 0, cache, NB, DS,
                          ev_lq[e], ev_lk[e], ev_subln_g[e], lam_init, y_in, MP, d_rnn)
            w_out = ev_w_out[e].astype(BF)
            new.setdefault('rec', []).append(hfin)
            new.setdefault('dk', []).append(qk_p[:, nh:].reshape(NBP, SP, H_B, 2 * DK_B))
            new.setdefault('dv', []).append(v[:MP].reshape(NBP, SP, H_B, 2 * DK_B))
        else:
            o = l // 2
            w_in = jnp.pad(od_w_in[o].astype(BF), ((0, 0), (0, LANES - ROPE_C)))
            cqn, ckvn, kr = _mla_in(h, w_in, od_qnorm_g[o], od_kvnorm_g[o], ql, kvl, tok.tile(512))
            w_uq = od_w_uq[o].astype(BF).reshape(ql, H_C, NOPE_C + ROPE_C)
            tm2, tn2 = tok.tile(2048), _pick(H_C * NOPE_C, 2048, LANES)
            qn = _mm(cqn, w_uq[:, :, :NOPE_C].reshape(ql, H_C * NOPE_C), BF, tm=tm2, tn=tn2, name="uq_nope")
            w_uq_r = w_uq[:, :, NOPE_C:].reshape(ql, H_C * ROPE_C)
            qr_p = _mm(cqn, w_uq_r, BF, tm=tm, rows=MP, name="uq_rope")
            qr_s = _mm(cqn, w_uq_r, BF, tm=tm, row0=MP, rows=MS, rope=(cos, sin, DS), name="uq_rope_rot")
            w_ukv = od_w_ukv[o].astype(BF)
            kv = _mm(ckvn, w_ukv, BF, tm=tm2, tn=tn2, name="ukv")
            mix_shape = (M, H_C * V_C)
            y_in = _mla_attn(qn, qr_p, 0, kv, kr, 0, 0, None, NBP, SP, None, mix_shape)
            kr_s = _rope(kr, cos, sin, MP, MS, 0, LANES, DS)
            ckv_c = cache_ckv[:, o].reshape(NB * PAST, kvl)
            kv_c = _mm(ckv_c, w_ukv, BF, tm=_pick(NB * PAST, 1024, SUBLANES), name="ukv_ctx")
            kr_c = jnp.pad(cache_krope[:, o].reshape(NB * PAST, ROPE_C), ((0, 0), (0, LANES - ROPE_C)))
            y_in = _mla_attn(qn, qr_s, 0, kv, kr_s, 0, MP, (kv_c, kr_c), NB, DS, y_in, mix_shape)
            w_out = od_w_out[o].astype(BF)
            new.setdefault('ckv', []).append(ckvn[:MP].reshape(NBP, SP, kvl))
            new.setdefault('kr', []).append(kr[:MP, :ROPE_C].reshape(NBP, SP, ROPE_C))
        x = _mm_residual(y_in, w_out, (), [(x, 0)], modt, tok, l, 5, 1.0, tm=tm, tn=_pick(D, 1024, LANES),
                         name="mix_out")
        x = _ffn([(x, 0)], normt, modt, tok, l, 2, wg, wu, wd, (l, 1))
        parts = [(x, 0)]

    y_prompt = _final_norm(x, final_g, 0, MP).reshape(NBP, SP, D)
    y_sample = _final_norm(x, final_g, MP, MS).reshape(NB, DS, D)
    return (y_prompt, y_sample, jnp.stack(new['rec'], axis=1), jnp.stack(new['dk'], axis=1),
            jnp.stack(new['dv'], axis=1), jnp.stack(new['ckv'], axis=1), jnp.stack(new['kr'], axis=1))
```

```python
import functools
import math

import jax
import jax.numpy as jnp
from jax import lax
from jax.experimental import pallas as pl
from jax.experimental.pallas import tpu as pltpu

BF = jnp.bfloat16
F32 = jnp.float32

GRID_W = 64
EPS = 1e-6
ROPE_THETA = 10000.0
N_MOD = 9
H_A = 16
CONV_W = 4
LRU_C = 8.0
H_B = 16
DK_B = 64
H_C = 32
NOPE_C = 128
ROPE_C = 64
V_C = 128

LANES = 128
SUBLANES = 8
VMEM_LIMIT = 56 * 1024 * 1024
FF_ALIGN = 1024
LOG2E = math.log2(math.e)
ATT_ROWS = 256
DATT_ROWS = 128
SHORT_SEQ_HEADS = 4
SCAN_UNROLL = 4


def _cparams(*sem):
    return pltpu.CompilerParams(dimension_semantics=sem, vmem_limit_bytes=VMEM_LIMIT)


def _pick(n, pref, mult):
    best = None
    d = mult
    while d <= min(n, pref):
        if n % d == 0:
            best = d
        d += mult
    return n if best is None else best


def _sigmoid(x):
    return 1.0 / (1.0 + jnp.exp2(x * (-LOG2E)))


def _rms(x):
    return x * lax.rsqrt(jnp.mean(x * x, axis=-1, keepdims=True) + EPS)


def _in_place(in_specs, args, out_buf):
    if out_buf is None:
        return {}
    in_specs.append(pl.BlockSpec(memory_space=pl.ANY))
    args.append(out_buf)
    return {len(args) - 1: 0}


def _mod_kernel(c_ref, w_ref, b_ref, o_ref):
    k = pl.program_id(2)
    c = c_ref[...]
    s = (c * _sigmoid(c)).astype(BF)
    part = jnp.dot(s, w_ref[...].astype(BF), preferred_element_type=F32)

    @pl.when(k == 0)
    def _():
        o_ref[...] = part + b_ref[...]

    @pl.when(k > 0)
    def _():
        o_ref[...] += part


def _modulation(cond, mod_w, mod_b):
    R, D = cond.shape
    L, _, N = mod_w.shape
    tn = _pick(N, 2048, LANES)
    tk = _pick(D, 1024, LANES)
    return pl.pallas_call(
        _mod_kernel,
        grid=(L, N // tn, D // tk),
        in_specs=[
            pl.BlockSpec((R, tk), lambda l, n, k: (0, k)),
            pl.BlockSpec((None, tk, tn), lambda l, n, k: (l, k, n)),
            pl.BlockSpec((None, 1, tn), lambda l, n, k: (l, 0, n)),
        ],
        out_specs=pl.BlockSpec((None, R, tn), lambda l, n, k: (l, 0, n)),
        out_shape=jax.ShapeDtypeStruct((L, R, N), F32),
        compiler_params=_cparams("parallel", "parallel", "arbitrary"),
        name="modulation",
    )(cond, mod_w, mod_b.reshape(L, 1, N))


class _Tokens:
    def __init__(self, mp, ds, nb, n_rows):
        self.mp, self.ds, self.nb, self.n_rows = mp, ds, nb, n_rows
        self.m = mp + ds * nb

    def tile(self, pref):
        return _pick(math.gcd(self.mp, self.ds), pref, SUBLANES)

    def mod_index(self, layer, chunk, tm):
        mp, ds, n_rows = self.mp, self.ds, self.n_rows

        def f(i):
            r = jnp.where(i * tm < mp, 0, 1 + (i * tm - mp) // ds)
            return (layer * n_rows + r) * N_MOD + chunk
        return f


def _prenorm_kernel(x_ref, g_ref, sh_ref, sc_ref, *rest):
    o_ref = rest[-1]
    y = _rms(x_ref[...]) * g_ref[...]
    o_ref[...] = (y * (1.0 + sc_ref[...]) + sh_ref[...]).astype(o_ref.dtype)


def _prenorm(parts, normt, modt, tok, layer, sub):
    h = None
    for x, row0 in parts:
        rows, D = x.shape
        tm = tok.tile(256)
        off = row0 // tm
        sh = tok.mod_index(layer, 3 * sub, tm)
        sc = tok.mod_index(layer, 3 * sub + 1, tm)
        in_specs = [
            pl.BlockSpec((tm, D), lambda i: (i, 0)),
            pl.BlockSpec((None, 1, D), lambda i: (layer * 3 + sub, 0, 0)),
            pl.BlockSpec((None, 1, D), lambda i: (sh(off + i), 0, 0)),
            pl.BlockSpec((None, 1, D), lambda i: (sc(off + i), 0, 0)),
        ]
        args = [x, normt, modt, modt]
        aliases = _in_place(in_specs, args, h)
        h = pl.pallas_call(
            _prenorm_kernel,
            grid=(rows // tm,),
            in_specs=in_specs,
            out_specs=pl.BlockSpec((tm, D), lambda i: (off + i, 0)),
            out_shape=jax.ShapeDtypeStruct((tok.m, D), BF),
            input_output_aliases=aliases,
            compiler_params=_cparams("parallel"),
            name="prenorm",
        )(*args)
    return h


def _final_norm_kernel(x_ref, g_ref, o_ref):
    o_ref[...] = _rms(x_ref[...]) * g_ref[...]


def _final_norm(x, g, row0, rows):
    D = x.shape[1]
    tm = _pick(math.gcd(row0, rows) if row0 else rows, 256, SUBLANES)
    off = row0 // tm
    return pl.pallas_call(
        _final_norm_kernel,
        grid=(rows // tm,),
        in_specs=[
            pl.BlockSpec((tm, D), lambda i: (off + i, 0)),
            pl.BlockSpec((1, D), lambda i: (0, 0)),
        ],
        out_specs=pl.BlockSpec((tm, D), lambda i: (i, 0)),
        out_shape=jax.ShapeDtypeStruct((rows, D), F32),
        compiler_params=_cparams("parallel"),
        name="final_norm",
    )(x, g.reshape(1, D))


def _rope_lanes(x, cos, sin, first):
    partner = jnp.where(first, pltpu.roll(x, LANES - 16, 1), pltpu.roll(x, 16, 1))
    return x * cos + partner * sin


def _mm_kernel(*refs, nk, residual, coef, rope=False):
    a_ref, w_ref = refs[:2]
    if residual:
        r_ref, g_ref = refs[2:4]
    if rope:
        cos_ref, sin_ref = refs[2:4]
    o_ref, acc_ref = (refs[-2], refs[-1]) if nk > 1 else (refs[-1], None)

    def finish(acc):
        if residual:
            o_ref[...] = r_ref[...] + (coef * g_ref[...]) * acc
        elif rope:
            cos, sin = cos_ref[...], sin_ref[...]
            first = (lax.broadcasted_iota(jnp.int32, cos.shape, 1) % 32) < 16
            for c in range(acc.shape[1] // LANES):
                cols = slice(c * LANES, (c + 1) * LANES)
                o_ref[:, cols] = _rope_lanes(acc[:, cols], cos, sin, first).astype(o_ref.dtype)
        else:
            o_ref[...] = acc.astype(o_ref.dtype)

    def part():
        return jnp.dot(a_ref[...].astype(BF), w_ref[...], preferred_element_type=F32)

    if nk == 1:
        finish(part())
        return
    k = pl.program_id(2)

    @pl.when(k == 0)
    def _():
        acc_ref[...] = part()

    @pl.when(jnp.logical_and(k > 0, k < nk - 1))
    def _():
        acc_ref[...] += part()

    @pl.when(k == nk - 1)
    def _():
        finish(acc_ref[...] + part())


def _mm(a, w, out_dtype, *, tm, tn=None, tk=None, row0=0, rows=None, rope=None, name="mm"):
    K, N = w.shape
    tn = _pick(N, 1024, LANES) if tn is None else tn
    rows = a.shape[0] if rows is None else rows
    tk = K if tk is None else tk
    nk = K // tk
    off = row0 // tm
    in_specs = [
        pl.BlockSpec((tm, tk), lambda i, j, k: (off + i, k)),
        pl.BlockSpec((tk, tn), lambda i, j, k: (k, j)),
    ]
    args = [a, w]
    if rope is not None:
        cos, sin, ds = rope
        per = ds // tm
        in_specs += [pl.BlockSpec((tm, LANES), lambda i, j, k: (i % per, 0))] * 2
        args += [cos, sin]
    return pl.pallas_call(
        functools.partial(_mm_kernel, nk=nk, residual=False, coef=None, rope=rope is not None),
        grid=(rows // tm, N // tn, nk),
        in_specs=in_specs,
        out_specs=pl.BlockSpec((tm, tn), lambda i, j, k: (i, j)),
        out_shape=jax.ShapeDtypeStruct((rows, N), out_dtype),
        scratch_shapes=[pltpu.VMEM((tm, tn), F32)] if nk > 1 else [],
        compiler_params=_cparams("parallel", "parallel", "arbitrary"),
        name=name,
    )(*args)


def _mm_residual(a, w, w_lead, parts, modt, tok, layer, chunk, coef, *, tm, tn, tk=None, name="mm_res"):
    K, N = w.shape[-2:]
    tk = K if tk is None else tk
    nk = K // tk
    gate = tok.mod_index(layer, chunk, tm)
    out = None
    for x, row0 in parts:
        off = row0 // tm
        in_specs = [
            pl.BlockSpec((tm, tk), lambda i, j, k: (off + i, k)),
            pl.BlockSpec((None,) * len(w_lead) + (tk, tn), lambda i, j, k: w_lead + (k, j)),
            pl.BlockSpec((tm, tn), lambda i, j, k: (i, j)),
            pl.BlockSpec((None, 1, tn), lambda i, j, k: (gate(off + i), 0, j)),
        ]
        args = [a, w, x, modt]
        aliases = _in_place(in_specs, args, out)
        out = pl.pallas_call(
            functools.partial(_mm_kernel, nk=nk, residual=True, coef=coef),
            grid=(x.shape[0] // tm, N // tn, nk),
            in_specs=in_specs,
            out_specs=pl.BlockSpec((tm, tn), lambda i, j, k: (off + i, j)),
            out_shape=jax.ShapeDtypeStruct((tok.m, N), F32),
            input_output_aliases=aliases,
            scratch_shapes=[pltpu.VMEM((tm, tn), F32)] if nk > 1 else [],
            compiler_params=_cparams("parallel", "parallel", "arbitrary"),
            name=name,
        )(*args)
    return out


def _ffn_up_kernel(h_ref, wg_ref, wu_ref, o_ref, *, d_ff):
    h = h_ref[...]
    g = jnp.dot(h, wg_ref[...], preferred_element_type=F32)
    u = jnp.dot(h, wu_ref[...], preferred_element_type=F32)
    tf = o_ref.shape[1]
    col = pl.program_id(1) * tf + lax.broadcasted_iota(jnp.int32, (1, tf), 1)
    o_ref[...] = jnp.where(col < d_ff, (g * _sigmoid(g)) * u, 0.0).astype(o_ref.dtype)


def _ffn_tail_kernel(h_ref, wg_ref, wu_ref, buf_ref, o_ref, *, rem):
    h = h_ref[...]
    g = jnp.dot(h, wg_ref[...], preferred_element_type=F32)
    u = jnp.dot(h, wu_ref[...], preferred_element_type=F32)
    o_ref[:, :rem] = ((g * _sigmoid(g)) * u).astype(o_ref.dtype)
    o_ref[:, rem:] = jnp.zeros((o_ref.shape[0], o_ref.shape[1] - rem), o_ref.dtype)


def _ffn_up(h, wg, wu, w_lead, fp, *, tm, tf):
    M, D = h.shape
    d_ff = wg.shape[-1]
    lead = (None,) * len(w_lead)
    nfull = d_ff // tf
    rem = d_ff - nfull * tf
    split = rem > 0 and rem % LANES == 0 and (nfull * tf) % rem == 0 and fp == (nfull + 1) * tf
    last = (d_ff - 1) // tf
    wspec = pl.BlockSpec(lead + (D, tf), lambda i, j: w_lead + (0, jnp.minimum(j, last)))
    a = pl.pallas_call(
        functools.partial(_ffn_up_kernel, d_ff=d_ff),
        grid=(M // tm, nfull if split else fp // tf),
        in_specs=[pl.BlockSpec((tm, D), lambda i, j: (i, 0)), wspec, wspec],
        out_specs=pl.BlockSpec((tm, tf), lambda i, j: (i, j)),
        out_shape=jax.ShapeDtypeStruct((M, fp), BF),
        compiler_params=_cparams("parallel", "parallel"),
        name="ffn_up",
    )(h, wg, wu)
    if not split:
        return a
    tspec = pl.BlockSpec(lead + (D, rem), lambda i: w_lead + (0, (nfull * tf) // rem))
    return pl.pallas_call(
        functools.partial(_ffn_tail_kernel, rem=rem),
        grid=(M // tm,),
        in_specs=[pl.BlockSpec((tm, D), lambda i: (i, 0)), tspec, tspec, pl.BlockSpec(memory_space=pl.ANY)],
        out_specs=pl.BlockSpec((tm, tf), lambda i: (i, nfull)),
        out_shape=jax.ShapeDtypeStruct((M, fp), BF),
        input_output_aliases={3: 0},
        compiler_params=_cparams("parallel"),
        name="ffn_up_tail",
    )(h, wg, wu, a)


def _ffn(parts, normt, modt, tok, layer, sub, wg, wu, wd, w_lead):
    h = _prenorm(parts, normt, modt, tok, layer, sub)
    fp, D = wd.shape[-2:]
    tm = tok.tile(1024)
    a = _ffn_up(h, wg, wu, w_lead, fp, tm=tm, tf=_pick(fp, 512, LANES))
    tk = _pick(fp, max(fp // 4, LANES), LANES)
    return _mm_residual(a, wd, w_lead, parts, modt, tok, layer, 3 * sub + 2, 0.5,
                        tm=tm, tn=_pick(D, 1024, LANES), tk=tk, name="ffn_down")


def _gelu_tanh(x):
    return x * (0.5 * (1.0 + jnp.tanh(math.sqrt(2.0 / math.pi) * (x + 0.044715 * (x * x * x)))))


def _softplus(x):
    return jnp.maximum(x, 0.0) + jnp.log1p(jnp.exp(-jnp.abs(x)))


def _rglru_kernel(*refs, S):
    xa_ref, ga_ref, cw_ref, cb_ref, w_ref, b_ref, lam_ref, h0_ref = refs[:8]
    y_ref, hfin_ref, pad_ref, af_ref, bf_ref, ab_ref, bb_ref = refs[-7:]
    C = LANES
    P0 = SUBLANES
    pad_ref[0:P0, :] = jnp.zeros((P0, C), F32)
    pad_ref[P0 + S:P0 + S + P0, :] = jnp.zeros((P0, C), F32)
    pad_ref[P0:P0 + S, :] = xa_ref[...]
    cw = cw_ref[...]
    xc = jnp.broadcast_to(cb_ref[...], (S, C))
    for j in range(CONV_W):
        xc = xc + pad_ref[pl.ds(P0 - 2 + j, S), :] * cw[j:j + 1, :]

    gates = jnp.dot(xc.astype(BF), w_ref[...], preferred_element_type=F32) + b_ref[...]
    lam = lam_ref[...]
    for d, (a_ref, b_ref_) in enumerate(((af_ref, bf_ref), (ab_ref, bb_ref))):
        r = _sigmoid(gates[:, (2 * d) * C:(2 * d + 1) * C])
        i = _sigmoid(gates[:, (2 * d + 1) * C:(2 * d + 2) * C])
        log_a = (-LRU_C * r) * _softplus(-lam[d:d + 1, :])
        a_ref[...] = jnp.exp(log_a)
        th = jnp.tanh(log_a)
        one_minus_a2 = (-2.0 * th) / (1.0 - th)
        mult = jnp.where(one_minus_a2 > 0.0, one_minus_a2 * lax.rsqrt(one_minus_a2), 0.0)
        b_ref_[...] = (mult * i) * xc

    rows = lax.broadcasted_iota(jnp.int32, (SUBLANES, C), 0)

    def tile_scan(a, b, down):
        for k in (1, 2, 4):
            if down:
                keep = rows >= k
                shift = k
            else:
                keep = rows < SUBLANES - k
                shift = SUBLANES - k
            a1 = jnp.where(keep, pltpu.roll(a, shift, 0), 1.0)
            b1 = jnp.where(keep, pltpu.roll(b, shift, 0), 0.0)
            b = a * b1 + b
            a = a * a1
        return a, b

    nt = S // SUBLANES
    unroll = SCAN_UNROLL if nt % SCAN_UNROLL == 0 else 1

    def body(tu, carry):
        hf, hb = carry
        tiles = []
        for u in range(unroll):
            t = tu * unroll + u
            r0 = pl.multiple_of(t * SUBLANES, SUBLANES)
            r1 = pl.multiple_of((nt - 1 - t) * SUBLANES, SUBLANES)
            fwd = tile_scan(af_ref[pl.ds(r0, SUBLANES), :], bf_ref[pl.ds(r0, SUBLANES), :], True)
            bwd = tile_scan(ab_ref[pl.ds(r1, SUBLANES), :], bb_ref[pl.ds(r1, SUBLANES), :], False)
            tiles.append((r0, fwd, r1, bwd))
        for r0, (a, b), r1, (a2, b2) in tiles:
            h = a * hf + b
            bf_ref[pl.ds(r0, SUBLANES), :] = h
            hf = jnp.broadcast_to(h[SUBLANES - 1:SUBLANES, :], (SUBLANES, C))
            g = a2 * hb + b2
            bb_ref[pl.ds(r1, SUBLANES), :] = g
            hb = jnp.broadcast_to(g[0:1, :], (SUBLANES, C))
        return hf, hb

    h0 = h0_ref[...]
    hf, hb = lax.fori_loop(0, nt // unroll, body, (jnp.broadcast_to(h0[0:1, :], (SUBLANES, C)),
                                                    jnp.broadcast_to(h0[1:2, :], (SUBLANES, C))))
    hfin_ref[0:1, :] = hf[0:1, :]
    hfin_ref[1:2, :] = hb[0:1, :]
    y_ref[...] = (_gelu_tanh(ga_ref[...]) * (bf_ref[...] + bb_ref[...])).astype(y_ref.dtype)


def _rglru(z, row0, nb, S, d_rnn, conv_w, conv_b, wcat, bcat, lam, h0, out_buf, out_shape):
    C = LANES
    ncb = d_rnn // C
    roff = row0 // S
    in_specs = [
        pl.BlockSpec((S, C), lambda b, c: (roff + b, c)),
        pl.BlockSpec((S, C), lambda b, c: (roff + b, ncb + c)),
        pl.BlockSpec((CONV_W, C), lambda b, c: (0, c)),
        pl.BlockSpec((1, C), lambda b, c: (0, c)),
        pl.BlockSpec((None, C, 4 * C), lambda b, c: (c, 0, 0)),
        pl.BlockSpec((None, 1, 4 * C), lambda b, c: (c, 0, 0)),
        pl.BlockSpec((2, C), lambda b, c: (0, c)),
        pl.BlockSpec((None, 2, C), lambda b, c: (b, 0, c)),
    ]
    args = [z, z, conv_w, conv_b.reshape(1, d_rnn), wcat, bcat, lam, h0]
    aliases = _in_place(in_specs, args, out_buf)
    return pl.pallas_call(
        functools.partial(_rglru_kernel, S=S),
        grid=(nb, ncb),
        in_specs=in_specs,
        out_specs=[
            pl.BlockSpec((S, C), lambda b, c: (roff + b, c)),
            pl.BlockSpec((None, 2, C), lambda b, c: (b, 0, c)),
        ],
        out_shape=[
            jax.ShapeDtypeStruct(out_shape, BF),
            jax.ShapeDtypeStruct((nb, 2, d_rnn), F32),
        ],
        input_output_aliases=aliases,
        scratch_shapes=[pltpu.VMEM((S + 2 * SUBLANES, C), F32)] + [pltpu.VMEM((S, C), F32)] * 4,
        compiler_params=_cparams("parallel", "parallel"),
        name="rglru",
    )(*args)


def _rope_tables(n_tok):
    t = jnp.arange(n_tok, dtype=jnp.int32)
    row, col = t // GRID_W, t % GRID_W
    n = 16
    inv = ROPE_THETA ** (-jnp.arange(n, dtype=F32) / n)
    ang_r = row.astype(F32)[:, None] * inv
    ang_c = col.astype(F32)[:, None] * inv

    def grp(ang):
        c, s = jnp.cos(ang), jnp.sin(ang)
        return jnp.concatenate([c, c], -1), jnp.concatenate([-s, s], -1)
    cr, sr = grp(ang_r)
    cc, sc = grp(ang_c)
    return jnp.concatenate([cr, cc, cr, cc], -1), jnp.concatenate([sr, sc, sr, sc], -1)


def _rope_kernel(x_ref, cos_ref, sin_ref, o_ref):
    cos = cos_ref[...]
    sin = sin_ref[...]
    first = (lax.broadcasted_iota(jnp.int32, cos.shape, 1) % 32) < 16
    for c in range(x_ref.shape[1] // LANES):
        x = x_ref[:, c * LANES:(c + 1) * LANES].astype(F32)
        partner = jnp.where(first, pltpu.roll(x, LANES - 16, 1), pltpu.roll(x, 16, 1))
        o_ref[:, c * LANES:(c + 1) * LANES] = (x * cos + partner * sin).astype(o_ref.dtype)


def _rope(x, cos, sin, row0, rows, col0, width, ds):
    tm = _pick(ds, 256, SUBLANES)
    tw = _pick(width, 1024, LANES)
    roff, coff, per = row0 // tm, col0 // tw, ds // tm
    return pl.pallas_call(
        _rope_kernel,
        grid=(rows // tm, width // tw),
        in_specs=[
            pl.BlockSpec((tm, tw), lambda i, j: (roff + i, coff + j)),
            pl.BlockSpec((tm, LANES), lambda i, j: (i % per, 0)),
            pl.BlockSpec((tm, LANES), lambda i, j: (i % per, 0)),
        ],
        out_specs=pl.BlockSpec((tm, tw), lambda i, j: (i, j)),
        out_shape=jax.ShapeDtypeStruct((rows, width), BF),
        compiler_params=_cparams("parallel", "parallel"),
        name="rope",
    )(x, cos, sin)


def _dattn_kernel(*refs, P, S, tq, G, lam_init):
    if P:
        q_ref, kc_ref, vc_ref, kl_ref, vl_ref, lq_ref, lk_ref, g_ref = refs[:8]
    else:
        q_ref, kl_ref, vl_ref, lq_ref, lk_ref, g_ref = refs[:6]
    o_ref, kk, vv = refs[-3:]
    C = 2 * DK_B

    @pl.when(pl.program_id(2) == 0)
    def _():
        if P:
            kk[0:P, :] = kc_ref[...].astype(BF)
            vv[0:P, :] = vc_ref[...].astype(BF)
        kk[P:P + S, :] = kl_ref[...].astype(BF)
        vv[P:P + S, :] = vl_ref[...].astype(BF)

    el = jnp.exp(jnp.sum(lq_ref[...] * lk_ref[...], axis=-1, keepdims=True))
    lam = el[0:1, :] - el[1:2, :] + lam_init
    sub = min(DATT_ROWS, tq)
    chains = []
    for g in range(G):
        cols = slice(g * C, (g + 1) * C)
        for r0 in range(0, tq, sub):
            q = q_ref[r0:r0 + sub, cols].astype(F32)
            lane = lax.broadcasted_iota(jnp.int32, q.shape, 1)
            qq = jnp.concatenate([jnp.where(lane < DK_B, q, 0.0), jnp.where(lane >= DK_B, q, 0.0)],
                                 axis=0).astype(BF)
            s = lax.dot_general(qq, kk[:, cols], (((1,), (1,)), ((), ())), preferred_element_type=F32)
            chains.append((cols, r0, s))
    for cols, r0, s in chains:
        e = jnp.exp2((s - jnp.max(s, axis=-1, keepdims=True)) * ((DK_B ** -0.5) * LOG2E))
        l = jnp.sum(e, axis=-1, keepdims=True)
        a = e[:sub, :] - e[sub:, :] * (lam * l[:sub, :] / l[sub:, :])
        o = jnp.dot(a.astype(BF), vv[:, cols], preferred_element_type=F32) / l[:sub, :]
        o_ref[r0:r0 + sub, cols] = ((_rms(o) * g_ref[...]) * (1.0 - lam_init)).astype(o_ref.dtype)


def _dattn(q, qrow0, qcol0, k_lat, krow0, kcol0, v_lat, vrow0, vcol0, cache, nb, S, lq, lk, g, lam_init,
           out_buf, orow0, ocol0):
    C = 2 * DK_B
    tq = _pick(S, 512, SUBLANES)
    nq = S // tq
    G = math.gcd(H_B, SHORT_SEQ_HEADS) if nq == 1 else 1
    W = C * G
    P = cache[0].shape[1] if cache is not None else 0
    qr, kr, vr, orr = qrow0 // tq, krow0 // S, vrow0 // S, orow0 // tq
    qc, kc, vc, oc = qcol0 // W, kcol0 // W, vcol0 // W, ocol0 // W
    in_specs = [pl.BlockSpec((tq, W), lambda b, h, i: (qr + b * nq + i, qc + h))]
    args = [q]
    if P:
        in_specs += [pl.BlockSpec((None, P, W), lambda b, h, i: (b, 0, h)),
                     pl.BlockSpec((None, P, W), lambda b, h, i: (b, 0, h))]
        args += [cache[0], cache[1]]
    in_specs += [
        pl.BlockSpec((S, W), lambda b, h, i: (kr + b, kc + h)),
        pl.BlockSpec((S, W), lambda b, h, i: (vr + b, vc + h)),
        pl.BlockSpec((2, DK_B), lambda b, h, i: (0, 0)),
        pl.BlockSpec((2, DK_B), lambda b, h, i: (0, 0)),
        pl.BlockSpec((1, C), lambda b, h, i: (0, 0)),
    ]
    args += [k_lat, v_lat, lq, lk, g.reshape(1, C)]
    aliases = _in_place(in_specs, args, out_buf)
    return pl.pallas_call(
        functools.partial(_dattn_kernel, P=P, S=S, tq=tq, G=G, lam_init=lam_init),
        grid=(nb, H_B // G, nq),
        in_specs=in_specs,
        out_specs=pl.BlockSpec((tq, W), lambda b, h, i: (orr + b * nq + i, oc + h)),
        out_shape=jax.ShapeDtypeStruct(out_buf.shape, BF),
        input_output_aliases=aliases,
        scratch_shapes=[pltpu.VMEM((P + S, W), BF), pltpu.VMEM((P + S, W), BF)],
        compiler_params=_cparams("parallel", "parallel", "arbitrary"),
        name="diff_attn",
    )(*args)


def _mla_in_kernel(h_ref, w_ref, gq_ref, gkv_ref, cq_ref, ckv_ref, kr_ref, *, ql, kvl):
    z = jnp.dot(h_ref[...], w_ref[...], preferred_element_type=F32)
    cq_ref[...] = (_rms(z[:, :ql]) * gq_ref[...]).astype(cq_ref.dtype)
    ckv_ref[...] = _rms(z[:, ql:ql + kvl]) * gkv_ref[...]
    kr_ref[...] = z[:, ql + kvl:]


def _mla_in(h, w, gq, gkv, ql, kvl, tm):
    M, D = h.shape
    N = w.shape[1]
    return pl.pallas_call(
        functools.partial(_mla_in_kernel, ql=ql, kvl=kvl),
        grid=(M // tm,),
        in_specs=[
            pl.BlockSpec((tm, D), lambda i: (i, 0)),
            pl.BlockSpec((D, N), lambda i: (0, 0)),
            pl.BlockSpec((1, ql), lambda i: (0, 0)),
            pl.BlockSpec((1, kvl), lambda i: (0, 0)),
        ],
        out_specs=[
            pl.BlockSpec((tm, ql), lambda i: (i, 0)),
            pl.BlockSpec((tm, kvl), lambda i: (i, 0)),
            pl.BlockSpec((tm, LANES), lambda i: (i, 0)),
        ],
        out_shape=[
            jax.ShapeDtypeStruct((M, ql), BF),
            jax.ShapeDtypeStruct((M, kvl), F32),
            jax.ShapeDtypeStruct((M, LANES), F32),
        ],
        compiler_params=_cparams("parallel"),
        name="mla_in",
    )(h, w, gq.reshape(1, ql), gkv.reshape(1, kvl))


def _mla_kernel(*refs, P, S, tq, G):
    if P:
        qn_ref, qr_ref, kvc_ref, krc_ref, kvl_ref, krl_ref = refs[:6]
        segs = ((0, P, kvc_ref, krc_ref), (P, S, kvl_ref, krl_ref))
    else:
        qn_ref, qr_ref, kvl_ref, krl_ref = refs[:4]
        segs = ((0, S, kvl_ref, krl_ref),)
    o_ref, ks, vs = refs[-3:]
    C = LANES

    @pl.when(pl.program_id(2) == 0)
    def _():
        for r0, n, kv_ref, kr_ref in segs:
            kr = kr_ref[...].astype(F32)
            kr2 = (kr + pltpu.roll(kr, ROPE_C, 1)).astype(BF)
            for hd in range(2 * G):
                ks[hd, r0:r0 + n, 0:C] = kv_ref[:, 2 * hd * C:(2 * hd + 1) * C]
                ks[hd, r0:r0 + n, C:2 * C] = kr2
                vs[hd, r0:r0 + n, 0:C] = kv_ref[:, (2 * hd + 1) * C:(2 * hd + 2) * C]
                vs[hd, r0:r0 + n, C:2 * C] = jnp.ones((n, C), BF)

    c = ((NOPE_C + ROPE_C) ** -0.5) * LOG2E
    chains = []
    sub = min(ATT_ROWS, tq)
    for r0 in range(0, tq, sub):
        rows = slice(r0, r0 + sub)
        for p in range(G):
            qr = qr_ref[rows, p * C:(p + 1) * C].astype(F32)
            lane = lax.broadcasted_iota(jnp.int32, qr.shape, 1)
            for hh in range(2):
                hd = 2 * p + hh
                keep = (lane < ROPE_C) if hh == 0 else (lane >= ROPE_C)
                qf = jnp.concatenate([qn_ref[rows, hd * C:(hd + 1) * C], jnp.where(keep, qr, 0.0).astype(BF)],
                                     axis=1)
                s = lax.dot_general(qf, ks[hd], (((1,), (1,)), ((), ())), preferred_element_type=F32)
                chains.append((rows, hd, s))
    for rows, hd, s in chains:
        e = jnp.exp2((s - jnp.max(s, axis=-1, keepdims=True)) * c).astype(BF)
        oe = jnp.dot(e, vs[hd], preferred_element_type=F32)
        o_ref[rows, hd * C:(hd + 1) * C] = (oe[:, 0:C] / oe[:, C:2 * C]).astype(o_ref.dtype)


def _mla_attn(qn, qr, qr_row0, kv_lat, kr_lat, kr_row0, row0, cache, nb, S, out_buf, out_shape):
    C = LANES
    tq = _pick(S, 512, SUBLANES)
    nq = S // tq
    G = math.gcd(H_C // 2, SHORT_SEQ_HEADS) if nq == 1 else 1
    P = cache[0].shape[0] // nb if cache is not None else 0
    r_q, r_qr, r_kv, r_kr = row0 // tq, qr_row0 // tq, row0 // S, kr_row0 // S
    in_specs = [pl.BlockSpec((tq, 2 * C * G), lambda b, h, i: (r_q + b * nq + i, h)),
                pl.BlockSpec((tq, C * G), lambda b, h, i: (r_qr + b * nq + i, h))]
    args = [qn, qr]
    if P:
        in_specs += [pl.BlockSpec((P, 4 * C * G), lambda b, h, i: (b, h)),
                     pl.BlockSpec((P, C), lambda b, h, i: (b, 0))]
        args += [cache[0], cache[1]]
    in_specs += [pl.BlockSpec((S, 4 * C * G), lambda b, h, i: (r_kv + b, h)),
                 pl.BlockSpec((S, C), lambda b, h, i: (r_kr + b, 0))]
    args += [kv_lat, kr_lat]
    aliases = _in_place(in_specs, args, out_buf)
    return pl.pallas_call(
        functools.partial(_mla_kernel, P=P, S=S, tq=tq, G=G),
        grid=(nb, H_C // (2 * G), nq),
        in_specs=in_specs,
        out_specs=pl.BlockSpec((tq, 2 * C * G), lambda b, h, i: (r_q + b * nq + i, h)),
        out_shape=jax.ShapeDtypeStruct(out_shape, BF),
        input_output_aliases=aliases,
        scratch_shapes=[pltpu.VMEM((2 * G, P + S, 2 * C), BF)] * 2,
        compiler_params=_cparams("parallel", "parallel", "arbitrary"),
        name="mla_attn",
    )(*args)


def kernel(x_prompt, x_sample, c, c_ctx, state_rglru, cache_dk, cache_dv, cache_ckv, cache_krope,
           mod_w, mod_b, norm_g, final_g, ffn_wg, ffn_wu, ffn_wd,
           ev_w_in, ev_conv_w, ev_conv_b, ev_wa, ev_ba, ev_wx, ev_bx, ev_lam, ev_lq, ev_lk,
           ev_subln_g, ev_w_out, od_w_in, od_qnorm_g, od_w_uq, od_kvnorm_g, od_w_ukv, od_w_out):
    assert NOPE_C == LANES and V_C == LANES and 2 * DK_B == LANES and 2 * ROPE_C == LANES
    NBP, SP, D = x_prompt.shape
    NB, DS, _ = x_sample.shape
    PAST = cache_dk.shape[2]
    L = mod_w.shape[0]
    MP, MS = NBP * SP, NB * DS
    tok = _Tokens(MP, DS, NB, 1 + NB)
    M = tok.m
    d_rnn = ev_lam.shape[-1]
    d_ff = ffn_wg.shape[-1]
    ql, kvl = od_qnorm_g.shape[-1], od_kvnorm_g.shape[-1]
    assert d_rnn // H_A == LANES

    n_cond = -(-(1 + NB) // SUBLANES) * SUBLANES
    cond = jnp.concatenate([c_ctx[None, :], c, jnp.zeros((n_cond - 1 - NB, D), F32)], axis=0)
    mod = _modulation(cond, mod_w, mod_b)
    modt = mod[:, :1 + NB].reshape(L * (1 + NB) * N_MOD, 1, D)
    normt = norm_g.reshape(L * 3, 1, D)

    fpad = -(-d_ff // FF_ALIGN) * FF_ALIGN - d_ff
    wg = ffn_wg.astype(BF)
    wu = ffn_wu.astype(BF)
    wd = jnp.pad(ffn_wd, ((0, 0), (0, 0), (0, fpad), (0, 0))).astype(BF)

    cos, sin = _rope_tables(DS)
    tm = tok.tile(1024)
    new = {}
    parts = [(x_prompt.reshape(MP, D), 0), (x_sample.reshape(MS, D), MP)]

    for l in range(L):
        x = _ffn(parts, normt, modt, tok, l, 0, wg, wu, wd, (l, 0))
        h = _prenorm([(x, 0)], normt, modt, tok, l, 1)
        if l % 2 == 0:
            e = l // 2
            lam_init = 0.8 - 0.6 * math.exp(-0.3 * l)
            nh = H_B * LANES
            w_in = ev_w_in[e].astype(BF)
            w_qk = w_in[:, 2 * d_rnn:2 * d_rnn + 2 * nh]
            xg = _mm(h, w_in[:, :2 * d_rnn], F32, tm=tm, name="ev_in_rec")
            v = _mm(h, w_in[:, 2 * d_rnn + 2 * nh:], F32, tm=tm, name="ev_in_v")
            qk_p = _mm(h, w_qk, F32, tm=tm, rows=MP, name="ev_in_qk")
            qk_s = _mm(h, w_qk, BF, tm=tm, row0=MP, rows=MS, rope=(cos, sin, DS), name="ev_in_qk_rope")
            wcat = jnp.concatenate([ev_wa[e, 0], ev_wx[e, 0], ev_wa[e, 1], ev_wx[e, 1]], axis=-1).astype(BF)
            bcat = jnp.concatenate([t.reshape(H_A, 1, LANES) for t in
                                    (ev_ba[e, 0], ev_bx[e, 0], ev_ba[e, 1], ev_bx[e, 1])], axis=-1)
            mix_shape = (M, d_rnn + nh)
            rg = functools.partial(_rglru, xg, d_rnn=d_rnn, conv_w=ev_conv_w[e], conv_b=ev_conv_b[e],
                                   wcat=wcat, bcat=bcat, lam=ev_lam[e], out_shape=mix_shape)
            y_in, hfin = rg(row0=0, nb=NBP, S=SP, h0=jnp.zeros((NBP, 2, d_rnn), F32), out_buf=None)
            y_in, _ = rg(row0=MP, nb=NB, S=DS, h0=state_rglru[:, e], out_buf=y_in)
            y_in = _dattn(qk_p, 0, 0, qk_p, 0, nh, v, 0, 0, None, NBP, SP,
                          ev_lq[e], ev_lk[e], ev_subln_g[e], lam_init, y_in, 0, d_rnn)
            cache = (cache_dk[:, e].reshape(NB, PAST, nh), cache_dv[:, e].reshape(NB, PAST, nh))
            y_in = _dattn(qk_s, 0, 0, qk_s, 0, nh, v, MP, 0, cache, NB, DS,
                          ev_lq[e], ev_lk[e], ev_subln_g[e], lam_init, y_in, MP, d_rnn)
            w_out = ev_w_out[e].astype(BF)
            new.setdefault('rec', []).append(hfin)
            new.setdefault('dk', []).append(qk_p[:, nh:].reshape(NBP, SP, H_B, 2 * DK_B))
            new.setdefault('dv', []).append(v[:MP].reshape(NBP, SP, H_B, 2 * DK_B))
        else:
            o = l // 2
            w_in = jnp.pad(od_w_in[o].astype(BF), ((0, 0), (0, LANES - ROPE_C)))
            cqn, ckvn, kr = _mla_in(h, w_in, od_qnorm_g[o], od_kvnorm_g[o], ql, kvl, tok.tile(512))
            w_uq = od_w_uq[o].astype(BF).reshape(ql, H_C, NOPE_C + ROPE_C)
            tm2, tn2 = tok.tile(2048), _pick(H_C * NOPE_C, 2048, LANES)
            qn = _mm(cqn, w_uq[:, :, :NOPE_C].reshape(ql, H_C * NOPE_C), BF, tm=tm2, tn=tn2, name="uq_nope")
            w_uq_r = w_uq[:, :, NOPE_C:].reshape(ql, H_C * ROPE_C)
            qr_p = _mm(cqn, w_uq_r, BF, tm=tm, rows=MP, name="uq_rope")
            qr_s = _mm(cqn, w_uq_r, BF, tm=tm, row0=MP, rows=MS, rope=(cos, sin, DS), name="uq_rope_rot")
            w_ukv = od_w_ukv[o].astype(BF)
            kv = _mm(ckvn, w_ukv, BF, tm=tm2, tn=tn2, name="ukv")
            mix_shape = (M, H_C * V_C)
            y_in = _mla_attn(qn, qr_p, 0, kv, kr, 0, 0, None, NBP, SP, None, mix_shape)
            kr_s = _rope(kr, cos, sin, MP, MS, 0, LANES, DS)
            ckv_c = cache_ckv[:, o].reshape(NB * PAST, kvl)
            kv_c = _mm(ckv_c, w_ukv, BF, tm=_pick(NB * PAST, 1024, SUBLANES), name="ukv_ctx")
            kr_c = jnp.pad(cache_krope[:, o].reshape(NB * PAST, ROPE_C), ((0, 0), (0, LANES - ROPE_C)))
            y_in = _mla_attn(qn, qr_s, 0, kv, kr_s, 0, MP, (kv_c, kr_c), NB, DS, y_in, mix_shape)
            w_out = od_w_out[o].astype(BF)
            new.setdefault('ckv', []).append(ckvn[:MP].reshape(NBP, SP, kvl))
            new.setdefault('kr', []).append(kr[:MP, :ROPE_C].reshape(NBP, SP, ROPE_C))
        x = _mm_residual(y_in, w_out, (), [(x, 0)], modt, tok, l, 5, 1.0, tm=tm, tn=_pick(D, 1024, LANES),
                         name="mix_out")
        x = _ffn([(x, 0)], normt, modt, tok, l, 2, wg, wu, wd, (l, 1))
        parts = [(x, 0)]

    y_prompt = _final_norm(x, final_g, 0, MP).reshape(NBP, SP, D)
    y_sample = _final_norm(x, final_g, MP, MS).reshape(NB, DS, D)
    return (y_prompt, y_sample, jnp.stack(new['rec'], axis=1), jnp.stack(new['dk'], axis=1),
            jnp.stack(new['dv'], axis=1), jnp.stack(new['ckv'], axis=1), jnp.stack(new['kr'], axis=1))
```

```python
import functools
import math

import jax
import jax.numpy as jnp
from jax import lax
from jax.experimental import pallas as pl
from jax.experimental.pallas import tpu as pltpu

BF = jnp.bfloat16
F32 = jnp.float32

GRID_W = 64
EPS = 1e-6
ROPE_THETA = 10000.0
N_MOD = 9
H_A = 16
CONV_W = 4
LRU_C = 8.0
H_B = 16
DK_B = 64
H_C = 32
NOPE_C = 128
ROPE_C = 64
V_C = 128

LANES = 128
SUBLANES = 8
VMEM_LIMIT = 56 * 1024 * 1024
FF_ALIGN = 1024
LOG2E = math.log2(math.e)
ATT_ROWS = 256
DATT_ROWS = 128
SHORT_SEQ_HEADS = 4
SCAN_UNROLL = 4
NORM_ROWS = 512


def _cparams(*sem):
    return pltpu.CompilerParams(dimension_semantics=sem, vmem_limit_bytes=VMEM_LIMIT)


def _pick(n, pref, mult):
    best = None
    d = mult
    while d <= min(n, pref):
        if n % d == 0:
            best = d
        d += mult
    return n if best is None else best


def _sigmoid(x):
    return 1.0 / (1.0 + jnp.exp2(x * (-LOG2E)))


def _rms(x):
    return x * lax.rsqrt(jnp.mean(x * x, axis=-1, keepdims=True) + EPS)


def _in_place(in_specs, args, out_buf):
    if out_buf is None:
        return {}
    in_specs.append(pl.BlockSpec(memory_space=pl.ANY))
    args.append(out_buf)
    return {len(args) - 1: 0}


def _mod_kernel(c_ref, w_ref, b_ref, o_ref):
    k = pl.program_id(2)
    c = c_ref[...]
    s = (c * _sigmoid(c)).astype(BF)
    part = jnp.dot(s, w_ref[...].astype(BF), preferred_element_type=F32)

    @pl.when(k == 0)
    def _():
        o_ref[...] = part + b_ref[...]

    @pl.when(k > 0)
    def _():
        o_ref[...] += part


def _modulation(cond, mod_w, mod_b):
    R, D = cond.shape
    L, _, N = mod_w.shape
    tn = _pick(N, 2048, LANES)
    tk = _pick(D, 1024, LANES)
    return pl.pallas_call(
        _mod_kernel,
        grid=(L, N // tn, D // tk),
        in_specs=[
            pl.BlockSpec((R, tk), lambda l, n, k: (0, k)),
            pl.BlockSpec((None, tk, tn), lambda l, n, k: (l, k, n)),
            pl.BlockSpec((None, 1, tn), lambda l, n, k: (l, 0, n)),
        ],
        out_specs=pl.BlockSpec((None, R, tn), lambda l, n, k: (l, 0, n)),
        out_shape=jax.ShapeDtypeStruct((L, R, N), F32),
        compiler_params=_cparams("parallel", "parallel", "arbitrary"),
        name="modulation",
    )(cond, mod_w, mod_b.reshape(L, 1, N))


class _Tokens:
    def __init__(self, mp, ds, nb, n_rows):
        self.mp, self.ds, self.nb, self.n_rows = mp, ds, nb, n_rows
        self.m = mp + ds * nb

    def tile(self, pref):
        return _pick(math.gcd(self.mp, self.ds), pref, SUBLANES)

    def mod_index(self, layer, chunk, tm):
        mp, ds, n_rows = self.mp, self.ds, self.n_rows

        def f(i):
            r = jnp.where(i * tm < mp, 0, 1 + (i * tm - mp) // ds)
            return (layer * n_rows + r) * N_MOD + chunk
        return f


def _prenorm_kernel(x_ref, g_ref, sh_ref, sc_ref, *rest):
    o_ref = rest[-1]
    y = _rms(x_ref[...]) * g_ref[...]
    o_ref[...] = (y * (1.0 + sc_ref[...]) + sh_ref[...]).astype(o_ref.dtype)


def _prenorm(parts, normt, modt, tok, layer, sub):
    h = None
    for x, row0 in parts:
        rows, D = x.shape
        tm = tok.tile(NORM_ROWS)
        off = row0 // tm
        sh = tok.mod_index(layer, 3 * sub, tm)
        sc = tok.mod_index(layer, 3 * sub + 1, tm)
        in_specs = [
            pl.BlockSpec((tm, D), lambda i: (i, 0)),
            pl.BlockSpec((None, 1, D), lambda i: (layer * 3 + sub, 0, 0)),
            pl.BlockSpec((None, 1, D), lambda i: (sh(off + i), 0, 0)),
            pl.BlockSpec((None, 1, D), lambda i: (sc(off + i), 0, 0)),
        ]
        args = [x, normt, modt, modt]
        aliases = _in_place(in_specs, args, h)
        h = pl.pallas_call(
            _prenorm_kernel,
            grid=(rows // tm,),
            in_specs=in_specs,
            out_specs=pl.BlockSpec((tm, D), lambda i: (off + i, 0)),
            out_shape=jax.ShapeDtypeStruct((tok.m, D), BF),
            input_output_aliases=aliases,
            compiler_params=_cparams("parallel"),
            name="prenorm",
        )(*args)
    return h


def _final_norm_kernel(x_ref, g_ref, o_ref):
    o_ref[...] = _rms(x_ref[...]) * g_ref[...]


def _final_norm(x, g, row0, rows):
    D = x.shape[1]
    tm = _pick(math.gcd(row0, rows) if row0 else rows, NORM_ROWS, SUBLANES)
    off = row0 // tm
    return pl.pallas_call(
        _final_norm_kernel,
        grid=(rows // tm,),
        in_specs=[
            pl.BlockSpec((tm, D), lambda i: (off + i, 0)),
            pl.BlockSpec((1, D), lambda i: (0, 0)),
        ],
        out_specs=pl.BlockSpec((tm, D), lambda i: (i, 0)),
        out_shape=jax.ShapeDtypeStruct((rows, D), F32),
        compiler_params=_cparams("parallel"),
        name="final_norm",
    )(x, g.reshape(1, D))


def _rope_lanes(x, cos, sin, first):
    partner = jnp.where(first, pltpu.roll(x, LANES - 16, 1), pltpu.roll(x, 16, 1))
    return x * cos + partner * sin


def _mm_kernel(*refs, nk, residual, coef, rope=False, last_rows=None):
    a_ref, w_ref = refs[:2]
    if residual:
        r_ref, g_ref = refs[2:4]
    if rope:
        cos_ref, sin_ref = refs[2:4]
    o_ref, acc_ref = (refs[-2], refs[-1]) if nk > 1 else (refs[-1], None)

    def finish(acc):
        if residual:
            o_ref[...] = r_ref[...] + (coef * g_ref[...]) * acc
        elif rope:
            cos, sin = cos_ref[...], sin_ref[...]
            first = (lax.broadcasted_iota(jnp.int32, cos.shape, 1) % 32) < 16
            for c in range(acc.shape[1] // LANES):
                cols = slice(c * LANES, (c + 1) * LANES)
                o_ref[:, cols] = _rope_lanes(acc[:, cols], cos, sin, first).astype(o_ref.dtype)
        else:
            o_ref[...] = acc.astype(o_ref.dtype)

    def part(valid_rows=None):
        w = w_ref[...]
        if valid_rows is not None:
            w = jnp.where(lax.broadcasted_iota(jnp.int32, w.shape, 0) < valid_rows, w, jnp.zeros_like(w))
        return jnp.dot(a_ref[...].astype(BF), w, preferred_element_type=F32)

    if nk == 1:
        finish(part(last_rows))
        return
    k = pl.program_id(2)

    @pl.when(k == 0)
    def _():
        acc_ref[...] = part()

    @pl.when(jnp.logical_and(k > 0, k < nk - 1))
    def _():
        acc_ref[...] += part()

    @pl.when(k == nk - 1)
    def _():
        finish(acc_ref[...] + part(last_rows))


def _mm(a, w, out_dtype, *, tm, tn=None, tk=None, row0=0, rows=None, rope=None, name="mm"):
    K, N = w.shape
    tn = _pick(N, 1024, LANES) if tn is None else tn
    rows = a.shape[0] if rows is None else rows
    tk = K if tk is None else tk
    nk = K // tk
    off = row0 // tm
    in_specs = [
        pl.BlockSpec((tm, tk), lambda i, j, k: (off + i, k)),
        pl.BlockSpec((tk, tn), lambda i, j, k: (k, j)),
    ]
    args = [a, w]
    if rope is not None:
        cos, sin, ds = rope
        per = ds // tm
        in_specs += [pl.BlockSpec((tm, LANES), lambda i, j, k: (i % per, 0))] * 2
        args += [cos, sin]
    return pl.pallas_call(
        functools.partial(_mm_kernel, nk=nk, residual=False, coef=None, rope=rope is not None),
        grid=(rows // tm, N // tn, nk),
        in_specs=in_specs,
        out_specs=pl.BlockSpec((tm, tn), lambda i, j, k: (i, j)),
        out_shape=jax.ShapeDtypeStruct((rows, N), out_dtype),
        scratch_shapes=[pltpu.VMEM((tm, tn), F32)] if nk > 1 else [],
        compiler_params=_cparams("parallel", "parallel", "arbitrary"),
        name=name,
    )(*args)


def _mm_residual(a, w, w_lead, parts, modt, tok, layer, chunk, coef, *, tm, tn, tk=None, name="mm_res"):
    K, N = a.shape[1], w.shape[-1]
    tk = K if tk is None else tk
    nk = K // tk
    last_rows = w.shape[-2] - (nk - 1) * tk if w.shape[-2] < K else None
    assert last_rows is None or 0 < last_rows < tk
    gate = tok.mod_index(layer, chunk, tm)
    out = None
    for x, row0 in parts:
        off = row0 // tm
        in_specs = [
            pl.BlockSpec((tm, tk), lambda i, j, k: (off + i, k)),
            pl.BlockSpec((None,) * len(w_lead) + (tk, tn), lambda i, j, k: w_lead + (k, j)),
            pl.BlockSpec((tm, tn), lambda i, j, k: (i, j)),
            pl.BlockSpec((None, 1, tn), lambda i, j, k: (gate(off + i), 0, j)),
        ]
        args = [a, w, x, modt]
        aliases = _in_place(in_specs, args, out)
        out = pl.pallas_call(
            functools.partial(_mm_kernel, nk=nk, residual=True, coef=coef, last_rows=last_rows),
            grid=(x.shape[0] // tm, N // tn, nk),
            in_specs=in_specs,
            out_specs=pl.BlockSpec((tm, tn), lambda i, j, k: (off + i, j)),
            out_shape=jax.ShapeDtypeStruct((tok.m, N), F32),
            input_output_aliases=aliases,
            scratch_shapes=[pltpu.VMEM((tm, tn), F32)] if nk > 1 else [],
            compiler_params=_cparams("parallel", "parallel", "arbitrary"),
            name=name,
        )(*args)
    return out


def _ffn_up_kernel(h_ref, wg_ref, wu_ref, o_ref, *, d_ff):
    h = h_ref[...]
    g = jnp.dot(h, wg_ref[...], preferred_element_type=F32)
    u = jnp.dot(h, wu_ref[...], preferred_element_type=F32)
    tf = o_ref.shape[1]
    col = pl.program_id(1) * tf + lax.broadcasted_iota(jnp.int32, (1, tf), 1)
    o_ref[...] = jnp.where(col < d_ff, (g * _sigmoid(g)) * u, 0.0).astype(o_ref.dtype)


def _ffn_tail_kernel(h_ref, wg_ref, wu_ref, buf_ref, o_ref, *, rem):
    h = h_ref[...]
    g = jnp.dot(h, wg_ref[...], preferred_element_type=F32)
    u = jnp.dot(h, wu_ref[...], preferred_element_type=F32)
    o_ref[:, :rem] = ((g * _sigmoid(g)) * u).astype(o_ref.dtype)
    o_ref[:, rem:] = jnp.zeros((o_ref.shape[0], o_ref.shape[1] - rem), o_ref.dtype)


def _ffn_up(h, wg, wu, w_lead, fp, *, tm, tf):
    M, D = h.shape
    d_ff = wg.shape[-1]
    lead = (None,) * len(w_lead)
    nfull = d_ff // tf
    rem = d_ff - nfull * tf
    split = rem > 0 and rem % LANES == 0 and (nfull * tf) % rem == 0 and fp == (nfull + 1) * tf
    last = (d_ff - 1) // tf
    wspec = pl.BlockSpec(lead + (D, tf), lambda i, j: w_lead + (0, jnp.minimum(j, last)))
    a = pl.pallas_call(
        functools.partial(_ffn_up_kernel, d_ff=d_ff),
        grid=(M // tm, nfull if split else fp // tf),
        in_specs=[pl.BlockSpec((tm, D), lambda i, j: (i, 0)), wspec, wspec],
        out_specs=pl.BlockSpec((tm, tf), lambda i, j: (i, j)),
        out_shape=jax.ShapeDtypeStruct((M, fp), BF),
        compiler_params=_cparams("parallel", "parallel"),
        name="ffn_up",
    )(h, wg, wu)
    if not split:
        return a
    tspec = pl.BlockSpec(lead + (D, rem), lambda i: w_lead + (0, (nfull * tf) // rem))
    return pl.pallas_call(
        functools.partial(_ffn_tail_kernel, rem=rem),
        grid=(M // tm,),
        in_specs=[pl.BlockSpec((tm, D), lambda i: (i, 0)), tspec, tspec, pl.BlockSpec(memory_space=pl.ANY)],
        out_specs=pl.BlockSpec((tm, tf), lambda i: (i, nfull)),
        out_shape=jax.ShapeDtypeStruct((M, fp), BF),
        input_output_aliases={3: 0},
        compiler_params=_cparams("parallel"),
        name="ffn_up_tail",
    )(h, wg, wu, a)


def _ffn(parts, normt, modt, tok, layer, sub, wg, wu, wd, w_lead):
    h = _prenorm(parts, normt, modt, tok, layer, sub)
    d_ff, D = wd.shape[-2:]
    fp = -(-d_ff // FF_ALIGN) * FF_ALIGN
    tm = tok.tile(1024)
    a = _ffn_up(h, wg, wu, w_lead, fp, tm=tm, tf=_pick(fp, 512, LANES))
    tk = _pick(fp, max(fp // 4, LANES), LANES)
    return _mm_residual(a, wd, w_lead, parts, modt, tok, layer, 3 * sub + 2, 0.5,
                        tm=tm, tn=_pick(D, 1024, LANES), tk=tk, name="ffn_down")


def _gelu_tanh(x):
    return x * (0.5 * (1.0 + jnp.tanh(math.sqrt(2.0 / math.pi) * (x + 0.044715 * (x * x * x)))))


def _softplus(x):
    return jnp.maximum(x, 0.0) + jnp.log1p(jnp.exp(-jnp.abs(x)))


def _rglru_kernel(*refs, S):
    xa_ref, ga_ref, cw_ref, cb_ref, w_ref, b_ref, lam_ref, h0_ref = refs[:8]
    y_ref, hfin_ref, pad_ref, af_ref, bf_ref, ab_ref, bb_ref = refs[-7:]
    C = LANES
    P0 = SUBLANES
    pad_ref[0:P0, :] = jnp.zeros((P0, C), F32)
    pad_ref[P0 + S:P0 + S + P0, :] = jnp.zeros((P0, C), F32)
    pad_ref[P0:P0 + S, :] = xa_ref[...]
    cw = cw_ref[...]
    xc = jnp.broadcast_to(cb_ref[...], (S, C))
    for j in range(CONV_W):
        xc = xc + pad_ref[pl.ds(P0 - 2 + j, S), :] * cw[j:j + 1, :]

    gates = jnp.dot(xc.astype(BF), w_ref[...], preferred_element_type=F32) + b_ref[...]
    lam = lam_ref[...]
    for d, (a_ref, b_ref_) in enumerate(((af_ref, bf_ref), (ab_ref, bb_ref))):
        r = _sigmoid(gates[:, (2 * d) * C:(2 * d + 1) * C])
        i = _sigmoid(gates[:, (2 * d + 1) * C:(2 * d + 2) * C])
        log_a = (-LRU_C * r) * _softplus(-lam[d:d + 1, :])
        a_ref[...] = jnp.exp(log_a)
        th = jnp.tanh(log_a)
        one_minus_a2 = (-2.0 * th) / (1.0 - th)
        mult = jnp.where(one_minus_a2 > 0.0, one_minus_a2 * lax.rsqrt(one_minus_a2), 0.0)
        b_ref_[...] = (mult * i) * xc

    rows = lax.broadcasted_iota(jnp.int32, (SUBLANES, C), 0)

    def tile_scan(a, b, down):
        for k in (1, 2, 4):
            if down:
                keep = rows >= k
                shift = k
            else:
                keep = rows < SUBLANES - k
                shift = SUBLANES - k
            a1 = jnp.where(keep, pltpu.roll(a, shift, 0), 1.0)
            b1 = jnp.where(keep, pltpu.roll(b, shift, 0), 0.0)
            b = a * b1 + b
            a = a * a1
        return a, b

    nt = S // SUBLANES
    unroll = SCAN_UNROLL if nt % SCAN_UNROLL == 0 else 1

    def body(tu, carry):
        hf, hb = carry
        tiles = []
        for u in range(unroll):
            t = tu * unroll + u
            r0 = pl.multiple_of(t * SUBLANES, SUBLANES)
            r1 = pl.multiple_of((nt - 1 - t) * SUBLANES, SUBLANES)
            fwd = tile_scan(af_ref[pl.ds(r0, SUBLANES), :], bf_ref[pl.ds(r0, SUBLANES), :], True)
            bwd = tile_scan(ab_ref[pl.ds(r1, SUBLANES), :], bb_ref[pl.ds(r1, SUBLANES), :], False)
            tiles.append((r0, fwd, r1, bwd))
        for r0, (a, b), r1, (a2, b2) in tiles:
            h = a * hf + b
            bf_ref[pl.ds(r0, SUBLANES), :] = h
            hf = jnp.broadcast_to(h[SUBLANES - 1:SUBLANES, :], (SUBLANES, C))
            g = a2 * hb + b2
            bb_ref[pl.ds(r1, SUBLANES), :] = g
            hb = jnp.broadcast_to(g[0:1, :], (SUBLANES, C))
        return hf, hb

    h0 = h0_ref[...]
    hf, hb = lax.fori_loop(0, nt // unroll, body, (jnp.broadcast_to(h0[0:1, :], (SUBLANES, C)),
                                                    jnp.broadcast_to(h0[1:2, :], (SUBLANES, C))))
    hfin_ref[0:1, :] = hf[0:1, :]
    hfin_ref[1:2, :] = hb[0:1, :]
    y_ref[...] = (_gelu_tanh(ga_ref[...]) * (bf_ref[...] + bb_ref[...])).astype(y_ref.dtype)


def _rglru(z, row0, nb, S, d_rnn, conv_w, conv_b, wcat, bcat, lam, h0, out_buf, out_shape):
    C = LANES
    ncb = d_rnn // C
    roff = row0 // S
    in_specs = [
        pl.BlockSpec((S, C), lambda b, c: (roff + b, c)),
        pl.BlockSpec((S, C), lambda b, c: (roff + b, ncb + c)),
        pl.BlockSpec((CONV_W, C), lambda b, c: (0, c)),
        pl.BlockSpec((1, C), lambda b, c: (0, c)),
        pl.BlockSpec((None, C, 4 * C), lambda b, c: (c, 0, 0)),
        pl.BlockSpec((None, 1, 4 * C), lambda b, c: (c, 0, 0)),
        pl.BlockSpec((2, C), lambda b, c: (0, c)),
        pl.BlockSpec((None, 2, C), lambda b, c: (b, 0, c)),
    ]
    args = [z, z, conv_w, conv_b.reshape(1, d_rnn), wcat, bcat, lam, h0]
    aliases = _in_place(in_specs, args, out_buf)
    return pl.pallas_call(
        functools.partial(_rglru_kernel, S=S),
        grid=(nb, ncb),
        in_specs=in_specs,
        out_specs=[
            pl.BlockSpec((S, C), lambda b, c: (roff + b, c)),
            pl.BlockSpec((None, 2, C), lambda b, c: (b, 0, c)),
        ],
        out_shape=[
            jax.ShapeDtypeStruct(out_shape, BF),
            jax.ShapeDtypeStruct((nb, 2, d_rnn), F32),
        ],
        input_output_aliases=aliases,
        scratch_shapes=[pltpu.VMEM((S + 2 * SUBLANES, C), F32)] + [pltpu.VMEM((S, C), F32)] * 4,
        compiler_params=_cparams("parallel", "parallel"),
        name="rglru",
    )(*args)


def _rope_tables(n_tok):
    t = jnp.arange(n_tok, dtype=jnp.int32)
    row, col = t // GRID_W, t % GRID_W
    n = 16
    inv = ROPE_THETA ** (-jnp.arange(n, dtype=F32) / n)
    ang_r = row.astype(F32)[:, None] * inv
    ang_c = col.astype(F32)[:, None] * inv

    def grp(ang):
        c, s = jnp.cos(ang), jnp.sin(ang)
        return jnp.concatenate([c, c], -1), jnp.concatenate([-s, s], -1)
    cr, sr = grp(ang_r)
    cc, sc = grp(ang_c)
    return jnp.concatenate([cr, cc, cr, cc], -1), jnp.concatenate([sr, sc, sr, sc], -1)


def _rope_kernel(x_ref, cos_ref, sin_ref, o_ref):
    cos = cos_ref[...]
    sin = sin_ref[...]
    first = (lax.broadcasted_iota(jnp.int32, cos.shape, 1) % 32) < 16
    for c in range(x_ref.shape[1] // LANES):
        x = x_ref[:, c * LANES:(c + 1) * LANES].astype(F32)
        partner = jnp.where(first, pltpu.roll(x, LANES - 16, 1), pltpu.roll(x, 16, 1))
        o_ref[:, c * LANES:(c + 1) * LANES] = (x * cos + partner * sin).astype(o_ref.dtype)


def _rope(x, cos, sin, row0, rows, col0, width, ds):
    tm = _pick(ds, 256, SUBLANES)
    tw = _pick(width, 1024, LANES)
    roff, coff, per = row0 // tm, col0 // tw, ds // tm
    return pl.pallas_call(
        _rope_kernel,
        grid=(rows // tm, width // tw),
        in_specs=[
            pl.BlockSpec((tm, tw), lambda i, j: (roff + i, coff + j)),
            pl.BlockSpec((tm, LANES), lambda i, j: (i % per, 0)),
            pl.BlockSpec((tm, LANES), lambda i, j: (i % per, 0)),
        ],
        out_specs=pl.BlockSpec((tm, tw), lambda i, j: (i, j)),
        out_shape=jax.ShapeDtypeStruct((rows, width), BF),
        compiler_params=_cparams("parallel", "parallel"),
        name="rope",
    )(x, cos, sin)


def _dattn_kernel(*refs, P, S, tq, G, lam_init):
    if P:
        q_ref, kc_ref, vc_ref, kl_ref, vl_ref, lq_ref, lk_ref, g_ref = refs[:8]
    else:
        q_ref, kl_ref, vl_ref, lq_ref, lk_ref, g_ref = refs[:6]
    o_ref, kk, vv = refs[-3:]
    C = 2 * DK_B

    @pl.when(pl.program_id(2) == 0)
    def _():
        if P:
            kk[0:P, :] = kc_ref[...].astype(BF)
            vv[0:P, :] = vc_ref[...].astype(BF)
        kk[P:P + S, :] = kl_ref[...].astype(BF)
        vv[P:P + S, :] = vl_ref[...].astype(BF)

    el = jnp.exp(jnp.sum(lq_ref[...] * lk_ref[...], axis=-1, keepdims=True))
    lam = el[0:1, :] - el[1:2, :] + lam_init
    sub = min(DATT_ROWS, tq)
    chains = []
    for g in range(G):
        cols = slice(g * C, (g + 1) * C)
        for r0 in range(0, tq, sub):
            q = q_ref[r0:r0 + sub, cols].astype(F32)
            lane = lax.broadcasted_iota(jnp.int32, q.shape, 1)
            qq = jnp.concatenate([jnp.where(lane < DK_B, q, 0.0), jnp.where(lane >= DK_B, q, 0.0)],
                                 axis=0).astype(BF)
            s = lax.dot_general(qq, kk[:, cols], (((1,), (1,)), ((), ())), preferred_element_type=F32)
            chains.append((cols, r0, s))
    for cols, r0, s in chains:
        e = jnp.exp2((s - jnp.max(s, axis=-1, keepdims=True)) * ((DK_B ** -0.5) * LOG2E))
        l = jnp.sum(e, axis=-1, keepdims=True)
        a = e[:sub, :] - e[sub:, :] * (lam * l[:sub, :] / l[sub:, :])
        o = jnp.dot(a.astype(BF), vv[:, cols], preferred_element_type=F32) / l[:sub, :]
        o_ref[r0:r0 + sub, cols] = ((_rms(o) * g_ref[...]) * (1.0 - lam_init)).astype(o_ref.dtype)


def _dattn(q, qrow0, qcol0, k_lat, krow0, kcol0, v_lat, vrow0, vcol0, cache, nb, S, lq, lk, g, lam_init,
           out_buf, orow0, ocol0):
    C = 2 * DK_B
    tq = _pick(S, 512, SUBLANES)
    nq = S // tq
    G = math.gcd(H_B, SHORT_SEQ_HEADS) if nq == 1 else 1
    W = C * G
    P = cache[0].shape[1] if cache is not None else 0
    qr, kr, vr, orr = qrow0 // tq, krow0 // S, vrow0 // S, orow0 // tq
    qc, kc, vc, oc = qcol0 // W, kcol0 // W, vcol0 // W, ocol0 // W
    in_specs = [pl.BlockSpec((tq, W), lambda b, h, i: (qr + b * nq + i, qc + h))]
    args = [q]
    if P:
        in_specs += [pl.BlockSpec((None, P, W), lambda b, h, i: (b, 0, h)),
                     pl.BlockSpec((None, P, W), lambda b, h, i: (b, 0, h))]
        args += [cache[0], cache[1]]
    in_specs += [
        pl.BlockSpec((S, W), lambda b, h, i: (kr + b, kc + h)),
        pl.BlockSpec((S, W), lambda b, h, i: (vr + b, vc + h)),
        pl.BlockSpec((2, DK_B), lambda b, h, i: (0, 0)),
        pl.BlockSpec((2, DK_B), lambda b, h, i: (0, 0)),
        pl.BlockSpec((1, C), lambda b, h, i: (0, 0)),
    ]
    args += [k_lat, v_lat, lq, lk, g.reshape(1, C)]
    aliases = _in_place(in_specs, args, out_buf)
    return pl.pallas_call(
        functools.partial(_dattn_kernel, P=P, S=S, tq=tq, G=G, lam_init=lam_init),
        grid=(nb, H_B // G, nq),
        in_specs=in_specs,
        out_specs=pl.BlockSpec((tq, W), lambda b, h, i: (orr + b * nq + i, oc + h)),
        out_shape=jax.ShapeDtypeStruct(out_buf.shape, BF),
        input_output_aliases=aliases,
        scratch_shapes=[pltpu.VMEM((P + S, W), BF), pltpu.VMEM((P + S, W), BF)],
        compiler_params=_cparams("parallel", "parallel", "arbitrary"),
        name="diff_attn",
    )(*args)


def _mla_in_kernel(h_ref, w_ref, gq_ref, gkv_ref, cq_ref, ckv_ref, kr_ref, *, ql, kvl):
    z = jnp.dot(h_ref[...], w_ref[...], preferred_element_type=F32)
    cq_ref[...] = (_rms(z[:, :ql]) * gq_ref[...]).astype(cq_ref.dtype)
    ckv_ref[...] = _rms(z[:, ql:ql + kvl]) * gkv_ref[...]
    kr_ref[...] = z[:, ql + kvl:]


def _mla_in(h, w, gq, gkv, ql, kvl, tm):
    M, D = h.shape
    N = w.shape[1]
    return pl.pallas_call(
        functools.partial(_mla_in_kernel, ql=ql, kvl=kvl),
        grid=(M // tm,),
        in_specs=[
            pl.BlockSpec((tm, D), lambda i: (i, 0)),
            pl.BlockSpec((D, N), lambda i: (0, 0)),
            pl.BlockSpec((1, ql), lambda i: (0, 0)),
            pl.BlockSpec((1, kvl), lambda i: (0, 0)),
        ],
        out_specs=[
            pl.BlockSpec((tm, ql), lambda i: (i, 0)),
            pl.BlockSpec((tm, kvl), lambda i: (i, 0)),
            pl.BlockSpec((tm, LANES), lambda i: (i, 0)),
        ],
        out_shape=[
            jax.ShapeDtypeStruct((M, ql), BF),
            jax.ShapeDtypeStruct((M, kvl), F32),
            jax.ShapeDtypeStruct((M, LANES), F32),
        ],
        compiler_params=_cparams("parallel"),
        name="mla_in",
    )(h, w, gq.reshape(1, ql), gkv.reshape(1, kvl))


def _mla_kernel(*refs, P, S, tq, G):
    if P:
        qn_ref, qr_ref, kvc_ref, krc_ref, kvl_ref, krl_ref = refs[:6]
        segs = ((0, P, kvc_ref, krc_ref), (P, S, kvl_ref, krl_ref))
    else:
        qn_ref, qr_ref, kvl_ref, krl_ref = refs[:4]
        segs = ((0, S, kvl_ref, krl_ref),)
    o_ref, ks, vs = refs[-3:]
    C = LANES

    @pl.when(pl.program_id(2) == 0)
    def _():
        for r0, n, kv_ref, kr_ref in segs:
            kr = kr_ref[...].astype(F32)
            kr2 = (kr + pltpu.roll(kr, ROPE_C, 1)).astype(BF)
            for hd in range(2 * G):
                ks[hd, r0:r0 + n, 0:C] = kv_ref[:, 2 * hd * C:(2 * hd + 1) * C]
                ks[hd, r0:r0 + n, C:2 * C] = kr2
                vs[hd, r0:r0 + n, 0:C] = kv_ref[:, (2 * hd + 1) * C:(2 * hd + 2) * C]
                vs[hd, r0:r0 + n, C:2 * C] = jnp.ones((n, C), BF)

    c = ((NOPE_C + ROPE_C) ** -0.5) * LOG2E
    chains = []
    sub = min(ATT_ROWS, tq)
    for r0 in range(0, tq, sub):
        rows = slice(r0, r0 + sub)
        for p in range(G):
            qr = qr_ref[rows, p * C:(p + 1) * C].astype(F32)
            lane = lax.broadcasted_iota(jnp.int32, qr.shape, 1)
            for hh in range(2):
                hd = 2 * p + hh
                keep = (lane < ROPE_C) if hh == 0 else (lane >= ROPE_C)
                qf = jnp.concatenate([qn_ref[rows, hd * C:(hd + 1) * C], jnp.where(keep, qr, 0.0).astype(BF)],
                                     axis=1)
                s = lax.dot_general(qf, ks[hd], (((1,), (1,)), ((), ())), preferred_element_type=F32)
                chains.append((rows, hd, s))
    for rows, hd, s in chains:
        e = jnp.exp2((s - jnp.max(s, axis=-1, keepdims=True)) * c).astype(BF)
        oe = jnp.dot(e, vs[hd], preferred_element_type=F32)
        o_ref[rows, hd * C:(hd + 1) * C] = (oe[:, 0:C] / oe[:, C:2 * C]).astype(o_ref.dtype)


def _mla_attn(qn, qr, qr_row0, kv_lat, kr_lat, kr_row0, row0, cache, nb, S, out_buf, out_shape):
    C = LANES
    tq = _pick(S, 512, SUBLANES)
    nq = S // tq
    G = math.gcd(H_C // 2, SHORT_SEQ_HEADS) if nq == 1 else 1
    P = cache[0].shape[0] // nb if cache is not None else 0
    r_q, r_qr, r_kv, r_kr = row0 // tq, qr_row0 // tq, row0 // S, kr_row0 // S
    in_specs = [pl.BlockSpec((tq, 2 * C * G), lambda b, h, i: (r_q + b * nq + i, h)),
                pl.BlockSpec((tq, C * G), lambda b, h, i: (r_qr + b * nq + i, h))]
    args = [qn, qr]
    if P:
        in_specs += [pl.BlockSpec((P, 4 * C * G), lambda b, h, i: (b, h)),
                     pl.BlockSpec((P, C), lambda b, h, i: (b, 0))]
        args += [cache[0], cache[1]]
    in_specs += [pl.BlockSpec((S, 4 * C * G), lambda b, h, i: (r_kv + b, h)),
                 pl.BlockSpec((S, C), lambda b, h, i: (r_kr + b, 0))]
    args += [kv_lat, kr_lat]
    aliases = _in_place(in_specs, args, out_buf)
    return pl.pallas_call(
        functools.partial(_mla_kernel, P=P, S=S, tq=tq, G=G),
        grid=(nb, H_C // (2 * G), nq),
        in_specs=in_specs,
        out_specs=pl.BlockSpec((tq, 2 * C * G), lambda b, h, i: (r_q + b * nq + i, h)),
        out_shape=jax.ShapeDtypeStruct(out_shape, BF),
        input_output_aliases=aliases,
        scratch_shapes=[pltpu.VMEM((2 * G, P + S, 2 * C), BF)] * 2,
        compiler_params=_cparams("parallel", "parallel", "arbitrary"),
        name="mla_attn",
    )(*args)


def kernel(x_prompt, x_sample, c, c_ctx, state_rglru, cache_dk, cache_dv, cache_ckv, cache_krope,
           mod_w, mod_b, norm_g, final_g, ffn_wg, ffn_wu, ffn_wd,
           ev_w_in, ev_conv_w, ev_conv_b, ev_wa, ev_ba, ev_wx, ev_bx, ev_lam, ev_lq, ev_lk,
           ev_subln_g, ev_w_out, od_w_in, od_qnorm_g, od_w_uq, od_kvnorm_g, od_w_ukv, od_w_out):
    assert NOPE_C == LANES and V_C == LANES and 2 * DK_B == LANES and 2 * ROPE_C == LANES
    NBP, SP, D = x_prompt.shape
    NB, DS, _ = x_sample.shape
    PAST = cache_dk.shape[2]
    L = mod_w.shape[0]
    MP, MS = NBP * SP, NB * DS
    tok = _Tokens(MP, DS, NB, 1 + NB)
    M = tok.m
    d_rnn = ev_lam.shape[-1]
    d_ff = ffn_wg.shape[-1]
    ql, kvl = od_qnorm_g.shape[-1], od_kvnorm_g.shape[-1]
    assert d_rnn // H_A == LANES

    n_cond = -(-(1 + NB) // SUBLANES) * SUBLANES
    cond = jnp.concatenate([c_ctx[None, :], c, jnp.zeros((n_cond - 1 - NB, D), F32)], axis=0)
    mod = _modulation(cond, mod_w, mod_b)
    modt = mod[:, :1 + NB].reshape(L * (1 + NB) * N_MOD, 1, D)
    normt = norm_g.reshape(L * 3, 1, D)

    wg = ffn_wg.astype(BF)
    wu = ffn_wu.astype(BF)
    wd = ffn_wd.astype(BF)

    cos, sin = _rope_tables(DS)
    tm = tok.tile(1024)
    new = {}
    parts = [(x_prompt.reshape(MP, D), 0), (x_sample.reshape(MS, D), MP)]

    for l in range(L):
        x = _ffn(parts, normt, modt, tok, l, 0, wg, wu, wd, (l, 0))
        h = _prenorm([(x, 0)], normt, modt, tok, l, 1)
        if l % 2 == 0:
            e = l // 2
            lam_init = 0.8 - 0.6 * math.exp(-0.3 * l)
            nh = H_B * LANES
            w_in = ev_w_in[e].astype(BF)
            w_qk = w_in[:, 2 * d_rnn:2 * d_rnn + 2 * nh]
            xg = _mm(h, w_in[:, :2 * d_rnn], F32, tm=tm, name="ev_in_rec")
            v = _mm(h, w_in[:, 2 * d_rnn + 2 * nh:], F32, tm=tm, name="ev_in_v")
            qk_p = _mm(h, w_qk, F32, tm=tm, rows=MP, name="ev_in_qk")
            qk_s = _mm(h, w_qk, BF, tm=tm, row0=MP, rows=MS, rope=(cos, sin, DS), name="ev_in_qk_rope")
            wcat = jnp.concatenate([ev_wa[e, 0], ev_wx[e, 0], ev_wa[e, 1], ev_wx[e, 1]], axis=-1).astype(BF)
            bcat = jnp.concatenate([t.reshape(H_A, 1, LANES) for t in
                                    (ev_ba[e, 0], ev_bx[e, 0], ev_ba[e, 1], ev_bx[e, 1])], axis=-1)
            mix_shape = (M, d_rnn + nh)
            rg = functools.partial(_rglru, xg, d_rnn=d_rnn, conv_w=ev_conv_w[e], conv_b=ev_conv_b[e],
                                   wcat=wcat, bcat=bcat, lam=ev_lam[e], out_shape=mix_shape)
            y_in, hfin = rg(row0=0, nb=NBP, S=SP, h0=jnp.zeros((NBP, 2, d_rnn), F32), out_buf=None)
            y_in, _ = rg(row0=MP, nb=NB, S=DS, h0=state_rglru[:, e], out_buf=y_in)
            y_in = _dattn(qk_p, 0, 0, qk_p, 0, nh, v, 0, 0, None, NBP, SP,
                          ev_lq[e], ev_lk[e], ev_subln_g[e], lam_init, y_in, 0, d_rnn)
            cache = (cache_dk[:, e].reshape(NB, PAST, nh), cache_dv[:, e].reshape(NB, PAST, nh))
            y_in = _dattn(qk_s, 0, 0, qk_s, 0, nh, v, MP, 0, cache, NB, DS,
                          ev_lq[e], ev_lk[e], ev_subln_g[e], lam_init, y_in, MP, d_rnn)
            w_out = ev_w_out[e].astype(BF)
            new.setdefault('rec', []).append(hfin)
            new.setdefault('dk', []).append(qk_p[:, nh:].reshape(NBP, SP, H_B, 2 * DK_B))
            new.setdefault('dv', []).append(v[:MP].reshape(NBP, SP, H_B, 2 * DK_B))
        else:
            o = l // 2
            w_in = jnp.pad(od_w_in[o].astype(BF), ((0, 0), (0, LANES - ROPE_C)))
            cqn, ckvn, kr = _mla_in(h, w_in, od_qnorm_g[o], od_kvnorm_g[o], ql, kvl, tok.tile(512))
            w_uq = od_w_uq[o].astype(BF).reshape(ql, H_C, NOPE_C + ROPE_C)
            tm2, tn2 = tok.tile(2048), _pick(H_C * NOPE_C, 2048, LANES)
            qn = _mm(cqn, w_uq[:, :, :NOPE_C].reshape(ql, H_C * NOPE_C), BF, tm=tm2, tn=tn2, name="uq_nope")
            w_uq_r = w_uq[:, :, NOPE_C:].reshape(ql, H_C * ROPE_C)
            qr_p = _mm(cqn, w_uq_r, BF, tm=tm, rows=MP, name="uq_rope")
            qr_s = _mm(cqn, w_uq_r, BF, tm=tm, row0=MP, rows=MS, rope=(cos, sin, DS), name="uq_rope_rot")
            w_ukv = od_w_ukv[o].astype(BF)
            kv = _mm(ckvn, w_ukv, BF, tm=tm2, tn=tn2, name="ukv")
            mix_shape = (M, H_C * V_C)
            y_in = _mla_attn(qn, qr_p, 0, kv, kr, 0, 0, None, NBP, SP, None, mix_shape)
            kr_s = _rope(kr, cos, sin, MP, MS, 0, LANES, DS)
            ckv_c = cache_ckv[:, o].reshape(NB * PAST, kvl)
            kv_c = _mm(ckv_c, w_ukv, BF, tm=_pick(NB * PAST, 1024, SUBLANES), name="ukv_ctx")
            kr_c = jnp.pad(cache_krope[:, o].reshape(NB * PAST, ROPE_C), ((0, 0), (0, LANES - ROPE_C)))
            y_in = _mla_attn(qn, qr_s, 0, kv, kr_s, 0, MP, (kv_c, kr_c), NB, DS, y_in, mix_shape)
            w_out = od_w_out[o].astype(BF)
            new.setdefault('ckv', []).append(ckvn[:MP].reshape(NBP, SP, kvl))
            new.setdefault('kr', []).append(kr[:MP, :ROPE_C].reshape(NBP, SP, ROPE_C))
        x = _mm_residual(y_in, w_out, (), [(x, 0)], modt, tok, l, 5, 1.0, tm=tm, tn=_pick(D, 1024, LANES),
                         name="mix_out")
        x = _ffn([(x, 0)], normt, modt, tok, l, 2, wg, wu, wd, (l, 1))
        parts = [(x, 0)]

    y_prompt = _final_norm(x, final_g, 0, MP).reshape(NBP, SP, D)
    y_sample = _final_norm(x, final_g, MP, MS).reshape(NB, DS, D)
    return (y_prompt, y_sample, jnp.stack(new['rec'], axis=1), jnp.stack(new['dk'], axis=1),
            jnp.stack(new['dv'], axis=1), jnp.stack(new['ckv'], axis=1), jnp.stack(new['kr'], axis=1))
```

```python
import functools
import math

import jax
import jax.numpy as jnp
from jax import lax
from jax.experimental import pallas as pl
from jax.experimental.pallas import tpu as pltpu

BF = jnp.bfloat16
F32 = jnp.float32

GRID_W = 64
EPS = 1e-6
ROPE_THETA = 10000.0
N_MOD = 9
H_A = 16
CONV_W = 4
LRU_C = 8.0
H_B = 16
DK_B = 64
H_C = 32
NOPE_C = 128
ROPE_C = 64
V_C = 128

LANES = 128
SUBLANES = 8
VMEM_LIMIT = 56 * 1024 * 1024
FF_ALIGN = 1024
LOG2E = math.log2(math.e)
ATT_ROWS = 256
DATT_ROWS = 128
DATT_HEADS = (4, 2)
MLA_PAIRS = (8, 1)
SCAN_UNROLL = 4
NORM_ROWS = 512
RGLRU_BLOCKS = 2


def _cparams(*sem):
    return pltpu.CompilerParams(dimension_semantics=sem, vmem_limit_bytes=VMEM_LIMIT)


def _pick(n, pref, mult):
    best = None
    d = mult
    while d <= min(n, pref):
        if n % d == 0:
            best = d
        d += mult
    return n if best is None else best


def _sigmoid(x):
    return 1.0 / (1.0 + jnp.exp2(x * (-LOG2E)))


def _rms(x):
    return x * lax.rsqrt(jnp.mean(x * x, axis=-1, keepdims=True) + EPS)


def _in_place(in_specs, args, out_buf):
    if out_buf is None:
        return {}
    in_specs.append(pl.BlockSpec(memory_space=pl.ANY))
    args.append(out_buf)
    return {len(args) - 1: 0}


def _mod_kernel(c_ref, w_ref, b_ref, o_ref):
    k = pl.program_id(2)
    c = c_ref[...]
    s = (c * _sigmoid(c)).astype(BF)
    part = jnp.dot(s, w_ref[...].astype(BF), preferred_element_type=F32)

    @pl.when(k == 0)
    def _():
        o_ref[...] = part + b_ref[...]

    @pl.when(k > 0)
    def _():
        o_ref[...] += part


def _modulation(cond, mod_w, mod_b):
    R, D = cond.shape
    L, _, N = mod_w.shape
    tn = _pick(N, 2048, LANES)
    tk = _pick(D, 1024, LANES)
    return pl.pallas_call(
        _mod_kernel,
        grid=(L, N // tn, D // tk),
        in_specs=[
            pl.BlockSpec((R, tk), lambda l, n, k: (0, k)),
            pl.BlockSpec((None, tk, tn), lambda l, n, k: (l, k, n)),
            pl.BlockSpec((None, 1, tn), lambda l, n, k: (l, 0, n)),
        ],
        out_specs=pl.BlockSpec((None, R, tn), lambda l, n, k: (l, 0, n)),
        out_shape=jax.ShapeDtypeStruct((L, R, N), F32),
        compiler_params=_cparams("parallel", "parallel", "arbitrary"),
        name="modulation",
    )(cond, mod_w, mod_b.reshape(L, 1, N))


class _Tokens:
    def __init__(self, mp, ds, nb, n_rows):
        self.mp, self.ds, self.nb, self.n_rows = mp, ds, nb, n_rows
        self.m = mp + ds * nb

    def tile(self, pref):
        return _pick(math.gcd(self.mp, self.ds), pref, SUBLANES)

    def mod_index(self, layer, chunk, tm):
        mp, ds, n_rows = self.mp, self.ds, self.n_rows

        def f(i):
            r = jnp.where(i * tm < mp, 0, 1 + (i * tm - mp) // ds)
            return (layer * n_rows + r) * N_MOD + chunk
        return f


def _prenorm_kernel(x_ref, g_ref, sh_ref, sc_ref, *rest):
    o_ref = rest[-1]
    y = _rms(x_ref[...]) * g_ref[...]
    o_ref[...] = (y * (1.0 + sc_ref[...]) + sh_ref[...]).astype(o_ref.dtype)


def _prenorm(parts, normt, modt, tok, layer, sub):
    h = None
    for x, row0 in parts:
        rows, D = x.shape
        tm = tok.tile(NORM_ROWS)
        off = row0 // tm
        sh = tok.mod_index(layer, 3 * sub, tm)
        sc = tok.mod_index(layer, 3 * sub + 1, tm)
        in_specs = [
            pl.BlockSpec((tm, D), lambda i: (i, 0)),
            pl.BlockSpec((None, 1, D), lambda i: (layer * 3 + sub, 0, 0)),
            pl.BlockSpec((None, 1, D), lambda i: (sh(off + i), 0, 0)),
            pl.BlockSpec((None, 1, D), lambda i: (sc(off + i), 0, 0)),
        ]
        args = [x, normt, modt, modt]
        aliases = _in_place(in_specs, args, h)
        h = pl.pallas_call(
            _prenorm_kernel,
            grid=(rows // tm,),
            in_specs=in_specs,
            out_specs=pl.BlockSpec((tm, D), lambda i: (off + i, 0)),
            out_shape=jax.ShapeDtypeStruct((tok.m, D), BF),
            input_output_aliases=aliases,
            compiler_params=_cparams("parallel"),
            name="prenorm",
        )(*args)
    return h


def _final_norm_kernel(x_ref, g_ref, o_ref):
    o_ref[...] = _rms(x_ref[...]) * g_ref[...]


def _final_norm(x, g, row0, rows):
    D = x.shape[1]
    tm = _pick(math.gcd(row0, rows) if row0 else rows, NORM_ROWS, SUBLANES)
    off = row0 // tm
    return pl.pallas_call(
        _final_norm_kernel,
        grid=(rows // tm,),
        in_specs=[
            pl.BlockSpec((tm, D), lambda i: (off + i, 0)),
            pl.BlockSpec((1, D), lambda i: (0, 0)),
        ],
        out_specs=pl.BlockSpec((tm, D), lambda i: (i, 0)),
        out_shape=jax.ShapeDtypeStruct((rows, D), F32),
        compiler_params=_cparams("parallel"),
        name="final_norm",
    )(x, g.reshape(1, D))


def _rope_lanes(x, cos, sin, first):
    partner = jnp.where(first, pltpu.roll(x, LANES - 16, 1), pltpu.roll(x, 16, 1))
    return x * cos + partner * sin


def _mm_kernel(*refs, nk, residual, coef, rope=False, last_rows=None):
    a_ref, w_ref = refs[:2]
    if residual:
        r_ref, g_ref = refs[2:4]
    if rope:
        cos_ref, sin_ref = refs[2:4]
    o_ref, acc_ref = (refs[-2], refs[-1]) if nk > 1 else (refs[-1], None)

    def finish(acc):
        if residual:
            o_ref[...] = r_ref[...] + (coef * g_ref[...]) * acc
        elif rope:
            cos, sin = cos_ref[...], sin_ref[...]
            first = (lax.broadcasted_iota(jnp.int32, cos.shape, 1) % 32) < 16
            for c in range(acc.shape[1] // LANES):
                cols = slice(c * LANES, (c + 1) * LANES)
                o_ref[:, cols] = _rope_lanes(acc[:, cols], cos, sin, first).astype(o_ref.dtype)
        else:
            o_ref[...] = acc.astype(o_ref.dtype)

    def part(valid_rows=None):
        w = w_ref[...]
        if valid_rows is not None:
            w = jnp.where(lax.broadcasted_iota(jnp.int32, w.shape, 0) < valid_rows, w, jnp.zeros_like(w))
        return jnp.dot(a_ref[...].astype(BF), w, preferred_element_type=F32)

    if nk == 1:
        finish(part(last_rows))
        return
    k = pl.program_id(2)

    @pl.when(k == 0)
    def _():
        acc_ref[...] = part()

    @pl.when(jnp.logical_and(k > 0, k < nk - 1))
    def _():
        acc_ref[...] += part()

    @pl.when(k == nk - 1)
    def _():
        finish(acc_ref[...] + part(last_rows))


def _mm(a, w, out_dtype, *, tm, tn=None, tk=None, row0=0, rows=None, rope=None, name="mm"):
    K, N = w.shape
    tn = _pick(N, 1024, LANES) if tn is None else tn
    rows = a.shape[0] if rows is None else rows
    tk = K if tk is None else tk
    nk = K // tk
    off = row0 // tm
    in_specs = [
        pl.BlockSpec((tm, tk), lambda i, j, k: (off + i, k)),
        pl.BlockSpec((tk, tn), lambda i, j, k: (k, j)),
    ]
    args = [a, w]
    if rope is not None:
        cos, sin, ds = rope
        per = ds // tm
        in_specs += [pl.BlockSpec((tm, LANES), lambda i, j, k: (i % per, 0))] * 2
        args += [cos, sin]
    return pl.pallas_call(
        functools.partial(_mm_kernel, nk=nk, residual=False, coef=None, rope=rope is not None),
        grid=(rows // tm, N // tn, nk),
        in_specs=in_specs,
        out_specs=pl.BlockSpec((tm, tn), lambda i, j, k: (i, j)),
        out_shape=jax.ShapeDtypeStruct((rows, N), out_dtype),
        scratch_shapes=[pltpu.VMEM((tm, tn), F32)] if nk > 1 else [],
        compiler_params=_cparams("parallel", "parallel", "arbitrary"),
        name=name,
    )(*args)


def _mm_residual(a, w, w_lead, parts, modt, tok, layer, chunk, coef, *, tm, tn, tk=None, name="mm_res"):
    K, N = a.shape[1], w.shape[-1]
    tk = K if tk is None else tk
    nk = K // tk
    last_rows = w.shape[-2] - (nk - 1) * tk if w.shape[-2] < K else None
    assert last_rows is None or 0 < last_rows < tk
    gate = tok.mod_index(layer, chunk, tm)
    out = None
    for x, row0 in parts:
        off = row0 // tm
        in_specs = [
            pl.BlockSpec((tm, tk), lambda i, j, k: (off + i, k)),
            pl.BlockSpec((None,) * len(w_lead) + (tk, tn), lambda i, j, k: w_lead + (k, j)),
            pl.BlockSpec((tm, tn), lambda i, j, k: (i, j)),
            pl.BlockSpec((None, 1, tn), lambda i, j, k: (gate(off + i), 0, j)),
        ]
        args = [a, w, x, modt]
        aliases = _in_place(in_specs, args, out)
        out = pl.pallas_call(
            functools.partial(_mm_kernel, nk=nk, residual=True, coef=coef, last_rows=last_rows),
            grid=(x.shape[0] // tm, N // tn, nk),
            in_specs=in_specs,
            out_specs=pl.BlockSpec((tm, tn), lambda i, j, k: (off + i, j)),
            out_shape=jax.ShapeDtypeStruct((tok.m, N), F32),
            input_output_aliases=aliases,
            scratch_shapes=[pltpu.VMEM((tm, tn), F32)] if nk > 1 else [],
            compiler_params=_cparams("parallel", "parallel", "arbitrary"),
            name=name,
        )(*args)
    return out


def _ffn_up_kernel(h_ref, wg_ref, wu_ref, o_ref, *, d_ff):
    h = h_ref[...]
    g = jnp.dot(h, wg_ref[...], preferred_element_type=F32)
    u = jnp.dot(h, wu_ref[...], preferred_element_type=F32)
    tf = o_ref.shape[1]
    col = pl.program_id(1) * tf + lax.broadcasted_iota(jnp.int32, (1, tf), 1)
    o_ref[...] = jnp.where(col < d_ff, (g * _sigmoid(g)) * u, 0.0).astype(o_ref.dtype)


def _ffn_tail_kernel(h_ref, wg_ref, wu_ref, buf_ref, o_ref, *, rem):
    h = h_ref[...]
    g = jnp.dot(h, wg_ref[...], preferred_element_type=F32)
    u = jnp.dot(h, wu_ref[...], preferred_element_type=F32)
    o_ref[:, :rem] = ((g * _sigmoid(g)) * u).astype(o_ref.dtype)
    o_ref[:, rem:] = jnp.zeros((o_ref.shape[0], o_ref.shape[1] - rem), o_ref.dtype)


def _ffn_up(h, wg, wu, w_lead, fp, *, tm, tf):
    M, D = h.shape
    d_ff = wg.shape[-1]
    lead = (None,) * len(w_lead)
    nfull = d_ff // tf
    rem = d_ff - nfull * tf
    split = rem > 0 and rem % LANES == 0 and (nfull * tf) % rem == 0 and fp == (nfull + 1) * tf
    last = (d_ff - 1) // tf
    wspec = pl.BlockSpec(lead + (D, tf), lambda i, j: w_lead + (0, jnp.minimum(j, last)))
    a = pl.pallas_call(
        functools.partial(_ffn_up_kernel, d_ff=d_ff),
        grid=(M // tm, nfull if split else fp // tf),
        in_specs=[pl.BlockSpec((tm, D), lambda i, j: (i, 0)), wspec, wspec],
        out_specs=pl.BlockSpec((tm, tf), lambda i, j: (i, j)),
        out_shape=jax.ShapeDtypeStruct((M, fp), BF),
        compiler_params=_cparams("parallel", "parallel"),
        name="ffn_up",
    )(h, wg, wu)
    if not split:
        return a
    tspec = pl.BlockSpec(lead + (D, rem), lambda i: w_lead + (0, (nfull * tf) // rem))
    return pl.pallas_call(
        functools.partial(_ffn_tail_kernel, rem=rem),
        grid=(M // tm,),
        in_specs=[pl.BlockSpec((tm, D), lambda i: (i, 0)), tspec, tspec, pl.BlockSpec(memory_space=pl.ANY)],
        out_specs=pl.BlockSpec((tm, tf), lambda i: (i, nfull)),
        out_shape=jax.ShapeDtypeStruct((M, fp), BF),
        input_output_aliases={3: 0},
        compiler_params=_cparams("parallel"),
        name="ffn_up_tail",
    )(h, wg, wu, a)


def _ffn(parts, normt, modt, tok, layer, sub, wg, wu, wd, w_lead):
    h = _prenorm(parts, normt, modt, tok, layer, sub)
    d_ff, D = wd.shape[-2:]
    fp = -(-d_ff // FF_ALIGN) * FF_ALIGN
    tm = tok.tile(1024)
    a = _ffn_up(h, wg, wu, w_lead, fp, tm=tm, tf=_pick(fp, 512, LANES))
    tk = _pick(fp, max(fp // 4, LANES), LANES)
    return _mm_residual(a, wd, w_lead, parts, modt, tok, layer, 3 * sub + 2, 0.5,
                        tm=tm, tn=_pick(D, 1024, LANES), tk=tk, name="ffn_down")


def _gelu_tanh(x):
    return x * (0.5 * (1.0 + jnp.tanh(math.sqrt(2.0 / math.pi) * (x + 0.044715 * (x * x * x)))))


def _softplus(x):
    return jnp.maximum(x, 0.0) + jnp.log1p(jnp.exp(-jnp.abs(x)))


def _rglru_kernel(*refs, S):
    xa_ref, ga_ref, cw_ref, cb_ref, w_ref, b_ref, lam_ref, h0_ref = refs[:8]
    y_ref, hfin_ref, pad_ref, af_ref, bf_ref, ab_ref, bb_ref = refs[-7:]
    C = LANES
    W = xa_ref.shape[1]
    P0 = SUBLANES
    pad_ref[0:P0, :] = jnp.zeros((P0, W), F32)
    pad_ref[P0 + S:P0 + S + P0, :] = jnp.zeros((P0, W), F32)
    pad_ref[P0:P0 + S, :] = xa_ref[...]
    cw = cw_ref[...]
    xc = jnp.broadcast_to(cb_ref[...], (S, W))
    for j in range(CONV_W):
        xc = xc + pad_ref[pl.ds(P0 - 2 + j, S), :] * cw[j:j + 1, :]

    lam = lam_ref[...]
    for blk in range(W // C):
        cols = slice(blk * C, (blk + 1) * C)
        xb = xc[:, cols]
        gates = jnp.dot(xb.astype(BF), w_ref[blk], preferred_element_type=F32) + b_ref[blk]
        for d, (a_ref, b_ref_) in enumerate(((af_ref, bf_ref), (ab_ref, bb_ref))):
            r = _sigmoid(gates[:, (2 * d) * C:(2 * d + 1) * C])
            i = _sigmoid(gates[:, (2 * d + 1) * C:(2 * d + 2) * C])
            log_a = (-LRU_C * r) * _softplus(-lam[d:d + 1, cols])
            a_ref[:, cols] = jnp.exp(log_a)
            th = jnp.tanh(log_a)
            one_minus_a2 = (-2.0 * th) / (1.0 - th)
            mult = jnp.where(one_minus_a2 > 0.0, one_minus_a2 * lax.rsqrt(one_minus_a2), 0.0)
            b_ref_[:, cols] = (mult * i) * xb

    rows = lax.broadcasted_iota(jnp.int32, (SUBLANES, W), 0)

    def tile_scan(a, b, down):
        for k in (1, 2, 4):
            if down:
                keep = rows >= k
                shift = k
            else:
                keep = rows < SUBLANES - k
                shift = SUBLANES - k
            a1 = jnp.where(keep, pltpu.roll(a, shift, 0), 1.0)
            b1 = jnp.where(keep, pltpu.roll(b, shift, 0), 0.0)
            b = a * b1 + b
            a = a * a1
        return a, b

    nt = S // SUBLANES
    unroll = max(SCAN_UNROLL * C // W, 1)
    unroll = unroll if nt % unroll == 0 else 1

    def body(tu, carry):
        hf, hb = carry
        tiles = []
        for u in range(unroll):
            t = tu * unroll + u
            r0 = pl.multiple_of(t * SUBLANES, SUBLANES)
            r1 = pl.multiple_of((nt - 1 - t) * SUBLANES, SUBLANES)
            fwd = tile_scan(af_ref[pl.ds(r0, SUBLANES), :], bf_ref[pl.ds(r0, SUBLANES), :], True)
            bwd = tile_scan(ab_ref[pl.ds(r1, SUBLANES), :], bb_ref[pl.ds(r1, SUBLANES), :], False)
            tiles.append((r0, fwd, r1, bwd))
        for r0, (a, b), r1, (a2, b2) in tiles:
            h = a * hf + b
            bf_ref[pl.ds(r0, SUBLANES), :] = h
            hf = jnp.broadcast_to(h[SUBLANES - 1:SUBLANES, :], (SUBLANES, W))
            g = a2 * hb + b2
            bb_ref[pl.ds(r1, SUBLANES), :] = g
            hb = jnp.broadcast_to(g[0:1, :], (SUBLANES, W))
        return hf, hb

    h0 = h0_ref[...]
    hf, hb = lax.fori_loop(0, nt // unroll, body, (jnp.broadcast_to(h0[0:1, :], (SUBLANES, W)),
                                                    jnp.broadcast_to(h0[1:2, :], (SUBLANES, W))))
    hfin_ref[0:1, :] = hf[0:1, :]
    hfin_ref[1:2, :] = hb[0:1, :]
    y_ref[...] = (_gelu_tanh(ga_ref[...]) * (bf_ref[...] + bb_ref[...])).astype(y_ref.dtype)


def _rglru(z, row0, nb, S, d_rnn, conv_w, conv_b, wcat, bcat, lam, h0, out_buf, out_shape):
    C = LANES
    nblk = math.gcd(d_rnn // C, RGLRU_BLOCKS)
    W = C * nblk
    ncb = d_rnn // W
    roff = row0 // S
    in_specs = [
        pl.BlockSpec((S, W), lambda b, c: (roff + b, c)),
        pl.BlockSpec((S, W), lambda b, c: (roff + b, ncb + c)),
        pl.BlockSpec((CONV_W, W), lambda b, c: (0, c)),
        pl.BlockSpec((1, W), lambda b, c: (0, c)),
        pl.BlockSpec((nblk, C, 4 * C), lambda b, c: (c, 0, 0)),
        pl.BlockSpec((nblk, 1, 4 * C), lambda b, c: (c, 0, 0)),
        pl.BlockSpec((2, W), lambda b, c: (0, c)),
        pl.BlockSpec((None, 2, W), lambda b, c: (b, 0, c)),
    ]
    args = [z, z, conv_w, conv_b.reshape(1, d_rnn), wcat, bcat, lam, h0]
    aliases = _in_place(in_specs, args, out_buf)
    return pl.pallas_call(
        functools.partial(_rglru_kernel, S=S),
        grid=(nb, ncb),
        in_specs=in_specs,
        out_specs=[
            pl.BlockSpec((S, W), lambda b, c: (roff + b, c)),
            pl.BlockSpec((None, 2, W), lambda b, c: (b, 0, c)),
        ],
        out_shape=[
            jax.ShapeDtypeStruct(out_shape, BF),
            jax.ShapeDtypeStruct((nb, 2, d_rnn), F32),
        ],
        input_output_aliases=aliases,
        scratch_shapes=[pltpu.VMEM((S + 2 * SUBLANES, W), F32)] + [pltpu.VMEM((S, W), F32)] * 4,
        compiler_params=_cparams("parallel", "parallel"),
        name="rglru",
    )(*args)


def _rope_tables(n_tok):
    t = jnp.arange(n_tok, dtype=jnp.int32)
    row, col = t // GRID_W, t % GRID_W
    n = 16
    inv = ROPE_THETA ** (-jnp.arange(n, dtype=F32) / n)
    ang_r = row.astype(F32)[:, None] * inv
    ang_c = col.astype(F32)[:, None] * inv

    def grp(ang):
        c, s = jnp.cos(ang), jnp.sin(ang)
        return jnp.concatenate([c, c], -1), jnp.concatenate([-s, s], -1)
    cr, sr = grp(ang_r)
    cc, sc = grp(ang_c)
    return jnp.concatenate([cr, cc, cr, cc], -1), jnp.concatenate([sr, sc, sr, sc], -1)


def _rope_kernel(x_ref, cos_ref, sin_ref, o_ref):
    cos = cos_ref[...]
    sin = sin_ref[...]
    first = (lax.broadcasted_iota(jnp.int32, cos.shape, 1) % 32) < 16
    for c in range(x_ref.shape[1] // LANES):
        x = x_ref[:, c * LANES:(c + 1) * LANES].astype(F32)
        partner = jnp.where(first, pltpu.roll(x, LANES - 16, 1), pltpu.roll(x, 16, 1))
        o_ref[:, c * LANES:(c + 1) * LANES] = (x * cos + partner * sin).astype(o_ref.dtype)


def _rope(x, cos, sin, row0, rows, col0, width, ds):
    tm = _pick(ds, 256, SUBLANES)
    tw = _pick(width, 1024, LANES)
    roff, coff, per = row0 // tm, col0 // tw, ds // tm
    return pl.pallas_call(
        _rope_kernel,
        grid=(rows // tm, width // tw),
        in_specs=[
            pl.BlockSpec((tm, tw), lambda i, j: (roff + i, coff + j)),
            pl.BlockSpec((tm, LANES), lambda i, j: (i % per, 0)),
            pl.BlockSpec((tm, LANES), lambda i, j: (i % per, 0)),
        ],
        out_specs=pl.BlockSpec((tm, tw), lambda i, j: (i, j)),
        out_shape=jax.ShapeDtypeStruct((rows, width), BF),
        compiler_params=_cparams("parallel", "parallel"),
        name="rope",
    )(x, cos, sin)


def _dattn_kernel(*refs, P, S, tq, G, lam_init):
    if P:
        q_ref, kc_ref, vc_ref, kl_ref, vl_ref, lq_ref, lk_ref, g_ref = refs[:8]
    else:
        q_ref, kl_ref, vl_ref, lq_ref, lk_ref, g_ref = refs[:6]
    o_ref, kk, vv = refs[-3:]
    C = 2 * DK_B

    @pl.when(pl.program_id(2) == 0)
    def _():
        if P:
            kk[0:P, :] = kc_ref[...].astype(BF)
            vv[0:P, :] = vc_ref[...].astype(BF)
        kk[P:P + S, :] = kl_ref[...].astype(BF)
        vv[P:P + S, :] = vl_ref[...].astype(BF)

    el = jnp.exp(jnp.sum(lq_ref[...] * lk_ref[...], axis=-1, keepdims=True))
    lam = el[0:1, :] - el[1:2, :] + lam_init
    sub = min(DATT_ROWS, tq)
    chains = []
    for g in range(G):
        cols = slice(g * C, (g + 1) * C)
        for r0 in range(0, tq, sub):
            q = q_ref[r0:r0 + sub, cols].astype(F32)
            lane = lax.broadcasted_iota(jnp.int32, q.shape, 1)
            qq = jnp.concatenate([jnp.where(lane < DK_B, q, 0.0), jnp.where(lane >= DK_B, q, 0.0)],
                                 axis=0).astype(BF)
            s = lax.dot_general(qq, kk[:, cols], (((1,), (1,)), ((), ())), preferred_element_type=F32)
            chains.append((cols, r0, s))
    for cols, r0, s in chains:
        e = jnp.exp2((s - jnp.max(s, axis=-1, keepdims=True)) * ((DK_B ** -0.5) * LOG2E))
        l = jnp.sum(e, axis=-1, keepdims=True)
        a = e[:sub, :] - e[sub:, :] * (lam * l[:sub, :] / l[sub:, :])
        o = jnp.dot(a.astype(BF), vv[:, cols], preferred_element_type=F32) / l[:sub, :]
        o_ref[r0:r0 + sub, cols] = ((_rms(o) * g_ref[...]) * (1.0 - lam_init)).astype(o_ref.dtype)


def _dattn(q, qrow0, qcol0, k_lat, krow0, kcol0, v_lat, vrow0, vcol0, cache, nb, S, lq, lk, g, lam_init,
           out_buf, orow0, ocol0):
    C = 2 * DK_B
    tq = _pick(S, 512, SUBLANES)
    nq = S // tq
    G = math.gcd(H_B, DATT_HEADS[0] if nq == 1 else DATT_HEADS[1])
    W = C * G
    P = cache[0].shape[1] if cache is not None else 0
    qr, kr, vr, orr = qrow0 // tq, krow0 // S, vrow0 // S, orow0 // tq
    qc, kc, vc, oc = qcol0 // W, kcol0 // W, vcol0 // W, ocol0 // W
    in_specs = [pl.BlockSpec((tq, W), lambda b, h, i: (qr + b * nq + i, qc + h))]
    args = [q]
    if P:
        in_specs += [pl.BlockSpec((None, P, W), lambda b, h, i: (b, 0, h)),
                     pl.BlockSpec((None, P, W), lambda b, h, i: (b, 0, h))]
        args += [cache[0], cache[1]]
    in_specs += [
        pl.BlockSpec((S, W), lambda b, h, i: (kr + b, kc + h)),
        pl.BlockSpec((S, W), lambda b, h, i: (vr + b, vc + h)),
        pl.BlockSpec((2, DK_B), lambda b, h, i: (0, 0)),
        pl.BlockSpec((2, DK_B), lambda b, h, i: (0, 0)),
        pl.BlockSpec((1, C), lambda b, h, i: (0, 0)),
    ]
    args += [k_lat, v_lat, lq, lk, g.reshape(1, C)]
    aliases = _in_place(in_specs, args, out_buf)
    return pl.pallas_call(
        functools.partial(_dattn_kernel, P=P, S=S, tq=tq, G=G, lam_init=lam_init),
        grid=(nb, H_B // G, nq),
        in_specs=in_specs,
        out_specs=pl.BlockSpec((tq, W), lambda b, h, i: (orr + b * nq + i, oc + h)),
        out_shape=jax.ShapeDtypeStruct(out_buf.shape, BF),
        input_output_aliases=aliases,
        scratch_shapes=[pltpu.VMEM((P + S, W), BF), pltpu.VMEM((P + S, W), BF)],
        compiler_params=_cparams("parallel", "parallel", "arbitrary"),
        name="diff_attn",
    )(*args)


def _mla_in_kernel(h_ref, w_ref, gq_ref, gkv_ref, cq_ref, ckv_ref, kr_ref, *, ql, kvl):
    z = jnp.dot(h_ref[...], w_ref[...], preferred_element_type=F32)
    cq_ref[...] = (_rms(z[:, :ql]) * gq_ref[...]).astype(cq_ref.dtype)
    ckv_ref[...] = _rms(z[:, ql:ql + kvl]) * gkv_ref[...]
    kr_ref[...] = z[:, ql + kvl:]


def _mla_in(h, w, gq, gkv, ql, kvl, tm):
    M, D = h.shape
    N = w.shape[1]
    return pl.pallas_call(
        functools.partial(_mla_in_kernel, ql=ql, kvl=kvl),
        grid=(M // tm,),
        in_specs=[
            pl.BlockSpec((tm, D), lambda i: (i, 0)),
            pl.BlockSpec((D, N), lambda i: (0, 0)),
            pl.BlockSpec((1, ql), lambda i: (0, 0)),
            pl.BlockSpec((1, kvl), lambda i: (0, 0)),
        ],
        out_specs=[
            pl.BlockSpec((tm, ql), lambda i: (i, 0)),
            pl.BlockSpec((tm, kvl), lambda i: (i, 0)),
            pl.BlockSpec((tm, LANES), lambda i: (i, 0)),
        ],
        out_shape=[
            jax.ShapeDtypeStruct((M, ql), BF),
            jax.ShapeDtypeStruct((M, kvl), F32),
            jax.ShapeDtypeStruct((M, LANES), F32),
        ],
        compiler_params=_cparams("parallel"),
        name="mla_in",
    )(h, w, gq.reshape(1, ql), gkv.reshape(1, kvl))


def _mla_kernel(*refs, P, S, tq, G):
    if P:
        qn_ref, qr_ref, kvc_ref, krc_ref, kvl_ref, krl_ref = refs[:6]
        segs = ((0, P, kvc_ref, krc_ref), (P, S, kvl_ref, krl_ref))
    else:
        qn_ref, qr_ref, kvl_ref, krl_ref = refs[:4]
        segs = ((0, S, kvl_ref, krl_ref),)
    o_ref, ks, vs = refs[-3:]
    C = LANES

    @pl.when(pl.program_id(2) == 0)
    def _():
        for r0, n, kv_ref, kr_ref in segs:
            kr = kr_ref[...].astype(F32)
            kr2 = (kr + pltpu.roll(kr, ROPE_C, 1)).astype(BF)
            for hd in range(2 * G):
                ks[hd, r0:r0 + n, 0:C] = kv_ref[:, 2 * hd * C:(2 * hd + 1) * C]
                ks[hd, r0:r0 + n, C:2 * C] = kr2
                vs[hd, r0:r0 + n, 0:C] = kv_ref[:, (2 * hd + 1) * C:(2 * hd + 2) * C]
                vs[hd, r0:r0 + n, C:2 * C] = jnp.ones((n, C), BF)

    c = ((NOPE_C + ROPE_C) ** -0.5) * LOG2E
    chains = []
    sub = min(ATT_ROWS, tq)
    for r0 in range(0, tq, sub):
        rows = slice(r0, r0 + sub)
        for p in range(G):
            qr = qr_ref[rows, p * C:(p + 1) * C].astype(F32)
            lane = lax.broadcasted_iota(jnp.int32, qr.shape, 1)
            for hh in range(2):
                hd = 2 * p + hh
                keep = (lane < ROPE_C) if hh == 0 else (lane >= ROPE_C)
                qf = jnp.concatenate([qn_ref[rows, hd * C:(hd + 1) * C], jnp.where(keep, qr, 0.0).astype(BF)],
                                     axis=1)
                s = lax.dot_general(qf, ks[hd], (((1,), (1,)), ((), ())), preferred_element_type=F32)
                chains.append((rows, hd, s))
    for rows, hd, s in chains:
        e = jnp.exp2((s - jnp.max(s, axis=-1, keepdims=True)) * c).astype(BF)
        oe = jnp.dot(e, vs[hd], preferred_element_type=F32)
        o_ref[rows, hd * C:(hd + 1) * C] = (oe[:, 0:C] / oe[:, C:2 * C]).astype(o_ref.dtype)


def _mla_attn(qn, qr, qr_row0, kv_lat, kr_lat, kr_row0, row0, cache, nb, S, out_buf, out_shape):
    C = LANES
    tq = _pick(S, 512, SUBLANES)
    nq = S // tq
    G = math.gcd(H_C // 2, MLA_PAIRS[0] if nq == 1 else MLA_PAIRS[1])
    P = cache[0].shape[0] // nb if cache is not None else 0
    r_q, r_qr, r_kv, r_kr = row0 // tq, qr_row0 // tq, row0 // S, kr_row0 // S
    in_specs = [pl.BlockSpec((tq, 2 * C * G), lambda b, h, i: (r_q + b * nq + i, h)),
                pl.BlockSpec((tq, C * G), lambda b, h, i: (r_qr + b * nq + i, h))]
    args = [qn, qr]
    if P:
        in_specs += [pl.BlockSpec((P, 4 * C * G), lambda b, h, i: (b, h)),
                     pl.BlockSpec((P, C), lambda b, h, i: (b, 0))]
        args += [cache[0], cache[1]]
    in_specs += [pl.BlockSpec((S, 4 * C * G), lambda b, h, i: (r_kv + b, h)),
                 pl.BlockSpec((S, C), lambda b, h, i: (r_kr + b, 0))]
    args += [kv_lat, kr_lat]
    aliases = _in_place(in_specs, args, out_buf)
    return pl.pallas_call(
        functools.partial(_mla_kernel, P=P, S=S, tq=tq, G=G),
        grid=(nb, H_C // (2 * G), nq),
        in_specs=in_specs,
        out_specs=pl.BlockSpec((tq, 2 * C * G), lambda b, h, i: (r_q + b * nq + i, h)),
        out_shape=jax.ShapeDtypeStruct(out_shape, BF),
        input_output_aliases=aliases,
        scratch_shapes=[pltpu.VMEM((2 * G, P + S, 2 * C), BF)] * 2,
        compiler_params=_cparams("parallel", "parallel", "arbitrary"),
        name="mla_attn",
    )(*args)


def kernel(x_prompt, x_sample, c, c_ctx, state_rglru, cache_dk, cache_dv, cache_ckv, cache_krope,
           mod_w, mod_b, norm_g, final_g, ffn_wg, ffn_wu, ffn_wd,
           ev_w_in, ev_conv_w, ev_conv_b, ev_wa, ev_ba, ev_wx, ev_bx, ev_lam, ev_lq, ev_lk,
           ev_subln_g, ev_w_out, od_w_in, od_qnorm_g, od_w_uq, od_kvnorm_g, od_w_ukv, od_w_out):
    assert NOPE_C == LANES and V_C == LANES and 2 * DK_B == LANES and 2 * ROPE_C == LANES
    NBP, SP, D = x_prompt.shape
    NB, DS, _ = x_sample.shape
    PAST = cache_dk.shape[2]
    L = mod_w.shape[0]
    MP, MS = NBP * SP, NB * DS
    tok = _Tokens(MP, DS, NB, 1 + NB)
    M = tok.m
    d_rnn = ev_lam.shape[-1]
    d_ff = ffn_wg.shape[-1]
    ql, kvl = od_qnorm_g.shape[-1], od_kvnorm_g.shape[-1]
    assert d_rnn // H_A == LANES

    n_cond = -(-(1 + NB) // SUBLANES) * SUBLANES
    cond = jnp.concatenate([c_ctx[None, :], c, jnp.zeros((n_cond - 1 - NB, D), F32)], axis=0)
    mod = _modulation(cond, mod_w, mod_b)
    modt = mod[:, :1 + NB].reshape(L * (1 + NB) * N_MOD, 1, D)
    normt = norm_g.reshape(L * 3, 1, D)

    wg = ffn_wg.astype(BF)
    wu = ffn_wu.astype(BF)
    wd = ffn_wd.astype(BF)

    cos, sin = _rope_tables(DS)
    tm = tok.tile(1024)
    new = {}
    parts = [(x_prompt.reshape(MP, D), 0), (x_sample.reshape(MS, D), MP)]

    for l in range(L):
        x = _ffn(parts, normt, modt, tok, l, 0, wg, wu, wd, (l, 0))
        h = _prenorm([(x, 0)], normt, modt, tok, l, 1)
        if l % 2 == 0:
            e = l // 2
            lam_init = 0.8 - 0.6 * math.exp(-0.3 * l)
            nh = H_B * LANES
            w_in = ev_w_in[e].astype(BF)
            w_qk = w_in[:, 2 * d_rnn:2 * d_rnn + 2 * nh]
            xg = _mm(h, w_in[:, :2 * d_rnn], F32, tm=tm, name="ev_in_rec")
            v = _mm(h, w_in[:, 2 * d_rnn + 2 * nh:], F32, tm=tm, name="ev_in_v")
            qk_p = _mm(h, w_qk, F32, tm=tm, rows=MP, name="ev_in_qk")
            qk_s = _mm(h, w_qk, BF, tm=tm, row0=MP, rows=MS, rope=(cos, sin, DS), name="ev_in_qk_rope")
            wcat = jnp.concatenate([ev_wa[e, 0], ev_wx[e, 0], ev_wa[e, 1], ev_wx[e, 1]], axis=-1).astype(BF)
            bcat = jnp.concatenate([t.reshape(H_A, 1, LANES) for t in
                                    (ev_ba[e, 0], ev_bx[e, 0], ev_ba[e, 1], ev_bx[e, 1])], axis=-1)
            mix_shape = (M, d_rnn + nh)
            rg = functools.partial(_rglru, xg, d_rnn=d_rnn, conv_w=ev_conv_w[e], conv_b=ev_conv_b[e],
                                   wcat=wcat, bcat=bcat, lam=ev_lam[e], out_shape=mix_shape)
            y_in, hfin = rg(row0=0, nb=NBP, S=SP, h0=jnp.zeros((NBP, 2, d_rnn), F32), out_buf=None)
            y_in, _ = rg(row0=MP, nb=NB, S=DS, h0=state_rglru[:, e], out_buf=y_in)
            y_in = _dattn(qk_p, 0, 0, qk_p, 0, nh, v, 0, 0, None, NBP, SP,
                          ev_lq[e], ev_lk[e], ev_subln_g[e], lam_init, y_in, 0, d_rnn)
            cache = (cache_dk[:, e].reshape(NB, PAST, nh), cache_dv[:, e].reshape(NB, PAST, nh))
            y_in = _dattn(qk_s, 0, 0, qk_s, 0, nh, v, MP, 0, cache, NB, DS,
                          ev_lq[e], ev_lk[e], ev_subln_g[e], lam_init, y_in, MP, d_rnn)
            w_out = ev_w_out[e].astype(BF)
            new.setdefault('rec', []).append(hfin)
            new.setdefault('dk', []).append(qk_p[:, nh:].reshape(NBP, SP, H_B, 2 * DK_B))
            new.setdefault('dv', []).append(v[:MP].reshape(NBP, SP, H_B, 2 * DK_B))
        else:
            o = l // 2
            w_in = jnp.pad(od_w_in[o].astype(BF), ((0, 0), (0, LANES - ROPE_C)))
            cqn, ckvn, kr = _mla_in(h, w_in, od_qnorm_g[o], od_kvnorm_g[o], ql, kvl, tok.tile(512))
            w_uq = od_w_uq[o].astype(BF).reshape(ql, H_C, NOPE_C + ROPE_C)
            tm2, tn2 = tok.tile(2048), _pick(H_C * NOPE_C, 2048, LANES)
            qn = _mm(cqn, w_uq[:, :, :NOPE_C].reshape(ql, H_C * NOPE_C), BF, tm=tm2, tn=tn2, name="uq_nope")
            w_uq_r = w_uq[:, :, NOPE_C:].reshape(ql, H_C * ROPE_C)
            qr_p = _mm(cqn, w_uq_r, BF, tm=tm, rows=MP, name="uq_rope")
            qr_s = _mm(cqn, w_uq_r, BF, tm=tm, row0=MP, rows=MS, rope=(cos, sin, DS), name="uq_rope_rot")
            w_ukv = od_w_ukv[o].astype(BF)
            kv = _mm(ckvn, w_ukv, BF, tm=tm2, tn=tn2, name="ukv")
            mix_shape = (M, H_C * V_C)
            y_in = _mla_attn(qn, qr_p, 0, kv, kr, 0, 0, None, NBP, SP, None, mix_shape)
            kr_s = _rope(kr, cos, sin, MP, MS, 0, LANES, DS)
            ckv_c = cache_ckv[:, o].reshape(NB * PAST, kvl)
            kv_c = _mm(ckv_c, w_ukv, BF, tm=_pick(NB * PAST, 1024, SUBLANES), name="ukv_ctx")
            kr_c = jnp.pad(cache_krope[:, o].reshape(NB * PAST, ROPE_C), ((0, 0), (0, LANES - ROPE_C)))
            y_in = _mla_attn(qn, qr_s, 0, kv, kr_s, 0, MP, (kv_c, kr_c), NB, DS, y_in, mix_shape)
            w_out = od_w_out[o].astype(BF)
            new.setdefault('ckv', []).append(ckvn[:MP].reshape(NBP, SP, kvl))
            new.setdefault('kr', []).append(kr[:MP, :ROPE_C].reshape(NBP, SP, ROPE_C))
        x = _mm_residual(y_in, w_out, (), [(x, 0)], modt, tok, l, 5, 1.0, tm=tm, tn=_pick(D, 1024, LANES),
                         name="mix_out")
        x = _ffn([(x, 0)], normt, modt, tok, l, 2, wg, wu, wd, (l, 1))
        parts = [(x, 0)]

    y_prompt = _final_norm(x, final_g, 0, MP).reshape(NBP, SP, D)
    y_sample = _final_norm(x, final_g, MP, MS).reshape(NB, DS, D)
    return (y_prompt, y_sample, jnp.stack(new['rec'], axis=1), jnp.stack(new['dk'], axis=1),
            jnp.stack(new['dv'], axis=1), jnp.stack(new['ckv'], axis=1), jnp.stack(new['kr'], axis=1))
```

```python
import functools
import math

import jax
import jax.numpy as jnp
from jax import lax
from jax.experimental import pallas as pl
from jax.experimental.pallas import tpu as pltpu

BF = jnp.bfloat16
F32 = jnp.float32

GRID_W = 64
EPS = 1e-6
ROPE_THETA = 10000.0
N_MOD = 9
H_A = 16
CONV_W = 4
LRU_C = 8.0
H_B = 16
DK_B = 64
H_C = 32
NOPE_C = 128
ROPE_C = 64
V_C = 128

LANES = 128
SUBLANES = 8
VMEM_LIMIT = 56 * 1024 * 1024
FF_ALIGN = 1024
LOG2E = math.log2(math.e)
ATT_ROWS = 256
DATT_ROWS = 128
DATT_HEADS = (4, 2)
MLA_PAIRS = (8, 1)
SCAN_UNROLL = 4
NORM_ROWS = 512
RGLRU_BLOCKS = 2
CAST_BYTES = 2 * 1024 * 1024


def _cparams(*sem):
    return pltpu.CompilerParams(dimension_semantics=sem, vmem_limit_bytes=VMEM_LIMIT)


def _pick(n, pref, mult):
    best = None
    d = mult
    while d <= min(n, pref):
        if n % d == 0:
            best = d
        d += mult
    return n if best is None else best


def _sigmoid(x):
    return 1.0 / (1.0 + jnp.exp2(x * (-LOG2E)))


def _rms(x):
    return x * lax.rsqrt(jnp.mean(x * x, axis=-1, keepdims=True) + EPS)


def _in_place(in_specs, args, out_buf):
    if out_buf is None:
        return {}
    in_specs.append(pl.BlockSpec(memory_space=pl.ANY))
    args.append(out_buf)
    return {len(args) - 1: 0}


class _Cast:
    def __init__(self, w, jobs, n_steps):
        R, C = w.shape[-2:]
        tr = 2 * SUBLANES
        ok = [t for t in range(tr, R + 1, tr) if R % t == 0 and (R // t) * len(jobs) <= n_steps]
        big = [t for t in ok if 4 * t * C >= CAST_BYTES]
        self.tr = big[0] if big else ok[-1]
        self.w, self.jobs, self.nb, self.shape = w, jobs, R // self.tr, (len(jobs), R, C)
        self.n_active = self.nb * len(jobs)

    @staticmethod
    def capacity(w, n_steps):
        R, C = w.shape[-2:]
        return min(2, n_steps // max(1, (4 * R * C) // CAST_BYTES))

    def specs(self, lin):
        nb, jobs, tr, C = self.nb, self.jobs, self.tr, self.shape[2]

        def split(*g):
            b = jnp.minimum(lin(*g), nb * len(jobs) - 1)
            return b // nb, b % nb

        def src(*g):
            j, r = split(*g)
            l, sl = jobs[0]
            for n, (l2, s2) in enumerate(jobs[1:], 1):
                l, sl = jnp.where(j == n, l2, l), jnp.where(j == n, s2, sl)
            return (l, sl, r, 0)
        return (pl.BlockSpec((None, None, tr, C), src),
                pl.BlockSpec((None, tr, C), lambda *g: split(*g) + (0,)))


def _mod_kernel(c_ref, w_ref, b_ref, o_ref):
    k = pl.program_id(2)
    c = c_ref[...]
    s = (c * _sigmoid(c)).astype(BF)
    part = jnp.dot(s, w_ref[...].astype(BF), preferred_element_type=F32)

    @pl.when(k == 0)
    def _():
        o_ref[...] = part + b_ref[...]

    @pl.when(k > 0)
    def _():
        o_ref[...] += part


def _modulation(cond, mod_w, mod_b):
    R, D = cond.shape
    L, _, N = mod_w.shape
    tn = _pick(N, 2048, LANES)
    tk = _pick(D, 1024, LANES)
    return pl.pallas_call(
        _mod_kernel,
        grid=(L, N // tn, D // tk),
        in_specs=[
            pl.BlockSpec((R, tk), lambda l, n, k: (0, k)),
            pl.BlockSpec((None, tk, tn), lambda l, n, k: (l, k, n)),
            pl.BlockSpec((None, 1, tn), lambda l, n, k: (l, 0, n)),
        ],
        out_specs=pl.BlockSpec((None, R, tn), lambda l, n, k: (l, 0, n)),
        out_shape=jax.ShapeDtypeStruct((L, R, N), F32),
        compiler_params=_cparams("parallel", "parallel", "arbitrary"),
        name="modulation",
    )(cond, mod_w, mod_b.reshape(L, 1, N))


class _Tokens:
    def __init__(self, mp, ds, nb, n_rows):
        self.mp, self.ds, self.nb, self.n_rows = mp, ds, nb, n_rows
        self.m = mp + ds * nb

    def tile(self, pref):
        return _pick(math.gcd(self.mp, self.ds), pref, SUBLANES)

    def mod_index(self, layer, chunk, tm):
        mp, ds, n_rows = self.mp, self.ds, self.n_rows

        def f(i):
            r = jnp.where(i * tm < mp, 0, 1 + (i * tm - mp) // ds)
            return (layer * n_rows + r) * N_MOD + chunk
        return f


def _prenorm_kernel(x_ref, g_ref, sh_ref, sc_ref, *rest):
    o_ref = rest[-1]
    y = _rms(x_ref[...]) * g_ref[...]
    o_ref[...] = (y * (1.0 + sc_ref[...]) + sh_ref[...]).astype(o_ref.dtype)


def _prenorm(parts, normt, modt, tok, layer, sub):
    h = None
    for x, row0 in parts:
        rows, D = x.shape
        tm = tok.tile(NORM_ROWS)
        off = row0 // tm
        sh = tok.mod_index(layer, 3 * sub, tm)
        sc = tok.mod_index(layer, 3 * sub + 1, tm)
        in_specs = [
            pl.BlockSpec((tm, D), lambda i: (i, 0)),
            pl.BlockSpec((None, 1, D), lambda i: (layer * 3 + sub, 0, 0)),
            pl.BlockSpec((None, 1, D), lambda i: (sh(off + i), 0, 0)),
            pl.BlockSpec((None, 1, D), lambda i: (sc(off + i), 0, 0)),
        ]
        args = [x, normt, modt, modt]
        aliases = _in_place(in_specs, args, h)
        h = pl.pallas_call(
            _prenorm_kernel,
            grid=(rows // tm,),
            in_specs=in_specs,
            out_specs=pl.BlockSpec((tm, D), lambda i: (off + i, 0)),
            out_shape=jax.ShapeDtypeStruct((tok.m, D), BF),
            input_output_aliases=aliases,
            compiler_params=_cparams("parallel"),
            name="prenorm",
        )(*args)
    return h


def _final_norm_kernel(x_ref, g_ref, o_ref):
    o_ref[...] = _rms(x_ref[...]) * g_ref[...]


def _final_norm(x, g, row0, rows):
    D = x.shape[1]
    tm = _pick(math.gcd(row0, rows) if row0 else rows, NORM_ROWS, SUBLANES)
    off = row0 // tm
    return pl.pallas_call(
        _final_norm_kernel,
        grid=(rows // tm,),
        in_specs=[
            pl.BlockSpec((tm, D), lambda i: (off + i, 0)),
            pl.BlockSpec((1, D), lambda i: (0, 0)),
        ],
        out_specs=pl.BlockSpec((tm, D), lambda i: (i, 0)),
        out_shape=jax.ShapeDtypeStruct((rows, D), F32),
        compiler_params=_cparams("parallel"),
        name="final_norm",
    )(x, g.reshape(1, D))


def _rope_lanes(x, cos, sin, first):
    partner = jnp.where(first, pltpu.roll(x, LANES - 16, 1), pltpu.roll(x, 16, 1))
    return x * cos + partner * sin


def _mm_kernel(*refs, nk, residual, coef, rope=False, last_rows=None, cast=0):
    a_ref, w_ref = refs[:2]
    if residual:
        r_ref, g_ref = refs[2:4]
    if rope:
        cos_ref, sin_ref = refs[2:4]
    refs, acc_ref = (refs[:-1], refs[-1]) if nk > 1 else (refs, None)
    if cast:
        step = (pl.program_id(0) * pl.num_programs(1) + pl.program_id(1)) * nk + pl.program_id(2)

        @pl.when(step < cast)
        def _():
            refs[-1][...] = refs[-3][...].astype(BF)
        refs = refs[:-1]
    o_ref = refs[-1]

    def finish(acc):
        if residual:
            o_ref[...] = r_ref[...] + (coef * g_ref[...]) * acc
        elif rope:
            cos, sin = cos_ref[...], sin_ref[...]
            first = (lax.broadcasted_iota(jnp.int32, cos.shape, 1) % 32) < 16
            for c in range(acc.shape[1] // LANES):
                cols = slice(c * LANES, (c + 1) * LANES)
                o_ref[:, cols] = _rope_lanes(acc[:, cols], cos, sin, first).astype(o_ref.dtype)
        else:
            o_ref[...] = acc.astype(o_ref.dtype)

    def part(valid_rows=None):
        w = w_ref[...]
        if valid_rows is not None:
            w = jnp.where(lax.broadcasted_iota(jnp.int32, w.shape, 0) < valid_rows, w, jnp.zeros_like(w))
        return jnp.dot(a_ref[...].astype(BF), w, preferred_element_type=F32)

    if nk == 1:
        finish(part(last_rows))
        return
    k = pl.program_id(2)

    @pl.when(k == 0)
    def _():
        acc_ref[...] = part()

    @pl.when(jnp.logical_and(k > 0, k < nk - 1))
    def _():
        acc_ref[...] += part()

    @pl.when(k == nk - 1)
    def _():
        finish(acc_ref[...] + part(last_rows))


def _mm(a, w, out_dtype, *, tm, tn=None, tk=None, row0=0, rows=None, rope=None, cast=None, name="mm"):
    K, N = w.shape
    tn = _pick(N, 1024, LANES) if tn is None else tn
    rows = a.shape[0] if rows is None else rows
    tk = K if tk is None else tk
    nk = K // tk
    off = row0 // tm
    in_specs = [
        pl.BlockSpec((tm, tk), lambda i, j, k: (off + i, k)),
        pl.BlockSpec((tk, tn), lambda i, j, k: (k, j)),
    ]
    args = [a, w]
    if rope is not None:
        cos, sin, ds = rope
        per = ds // tm
        in_specs += [pl.BlockSpec((tm, LANES), lambda i, j, k: (i % per, 0))] * 2
        args += [cos, sin]
    grid = (rows // tm, N // tn, nk)
    out_specs = [pl.BlockSpec((tm, tn), lambda i, j, k: (i, j))]
    out_shape = [jax.ShapeDtypeStruct((rows, N), out_dtype)]
    if cast is not None:
        job = _Cast(cast[0], cast[1], grid[0] * grid[1] * grid[2])
        cin, cout = job.specs(lambda i, j, k: (i * grid[1] + j) * nk + k)
        in_specs.append(cin)
        args.append(job.w)
        out_specs.append(cout)
        out_shape.append(jax.ShapeDtypeStruct(job.shape, BF))
    res = pl.pallas_call(
        functools.partial(_mm_kernel, nk=nk, residual=False, coef=None, rope=rope is not None,
                          cast=job.n_active if cast is not None else 0),
        grid=grid,
        in_specs=in_specs,
        out_specs=out_specs,
        out_shape=out_shape,
        scratch_shapes=[pltpu.VMEM((tm, tn), F32)] if nk > 1 else [],
        compiler_params=_cparams(*(("arbitrary",) * 3 if cast is not None else ("parallel", "parallel", "arbitrary"))),
        name=name,
    )(*args)
    return res if cast is not None else res[0]


def _mm_residual(a, w, w_lead, parts, modt, tok, layer, chunk, coef, *, tm, tn, tk=None, casts=None,
                 name="mm_res"):
    K, N = a.shape[1], w.shape[-1]
    tk = K if tk is None else tk
    nk = K // tk
    last_rows = w.shape[-2] - (nk - 1) * tk if w.shape[-2] < K else None
    assert last_rows is None or 0 < last_rows < tk
    gate = tok.mod_index(layer, chunk, tm)
    out = None
    cast_outs = []
    for n, (x, row0) in enumerate(parts):
        off = row0 // tm
        grid = (x.shape[0] // tm, N // tn, nk)
        in_specs = [
            pl.BlockSpec((tm, tk), lambda i, j, k: (off + i, k)),
            pl.BlockSpec((None,) * len(w_lead) + (tk, tn), lambda i, j, k: w_lead + (k, j)),
            pl.BlockSpec((tm, tn), lambda i, j, k: (i, j)),
            pl.BlockSpec((None, 1, tn), lambda i, j, k: (gate(off + i), 0, j)),
        ]
        args = [a, w, x, modt]
        aliases = _in_place(in_specs, args, out)
        out_specs = [pl.BlockSpec((tm, tn), lambda i, j, k: (off + i, j))]
        out_shape = [jax.ShapeDtypeStruct((tok.m, N), F32)]
        cast = casts[n] if casts else None
        if cast is not None:
            job = _Cast(cast[0], cast[1], grid[0] * grid[1] * grid[2])
            cin, cout = job.specs(lambda i, j, k: (i * grid[1] + j) * nk + k)
            in_specs.append(cin)
            args.append(job.w)
            out_specs.append(cout)
            out_shape.append(jax.ShapeDtypeStruct(job.shape, BF))
        res = pl.pallas_call(
            functools.partial(_mm_kernel, nk=nk, residual=True, coef=coef, last_rows=last_rows,
                              cast=job.n_active if cast is not None else 0),
            grid=grid,
            in_specs=in_specs,
            out_specs=out_specs,
            out_shape=out_shape,
            input_output_aliases=aliases,
            scratch_shapes=[pltpu.VMEM((tm, tn), F32)] if nk > 1 else [],
            compiler_params=_cparams(*(("arbitrary",) * 3 if cast is not None
                                       else ("parallel", "parallel", "arbitrary"))),
            name=name,
        )(*args)
        out = res[0]
        cast_outs.append(res[1] if cast is not None else None)
    return (out, cast_outs) if casts else out


def _ffn_up_kernel(h_ref, wg_ref, wu_ref, *rest, d_ff, cast=0):
    if cast:
        @pl.when(pl.program_id(0) * pl.num_programs(1) + pl.program_id(1) < cast)
        def _():
            rest[2][...] = rest[0][...].astype(BF)
        rest = rest[1:]
    o_ref = rest[0]
    h = h_ref[...]
    g = jnp.dot(h, wg_ref[...], preferred_element_type=F32)
    u = jnp.dot(h, wu_ref[...], preferred_element_type=F32)
    tf = o_ref.shape[1]
    col = pl.program_id(1) * tf + lax.broadcasted_iota(jnp.int32, (1, tf), 1)
    o_ref[...] = jnp.where(col < d_ff, (g * _sigmoid(g)) * u, 0.0).astype(o_ref.dtype)


def _ffn_tail_kernel(h_ref, wg_ref, wu_ref, buf_ref, o_ref, *, rem):
    h = h_ref[...]
    g = jnp.dot(h, wg_ref[...], preferred_element_type=F32)
    u = jnp.dot(h, wu_ref[...], preferred_element_type=F32)
    o_ref[:, :rem] = ((g * _sigmoid(g)) * u).astype(o_ref.dtype)
    o_ref[:, rem:] = jnp.zeros((o_ref.shape[0], o_ref.shape[1] - rem), o_ref.dtype)


def _ffn_up(h, wg, wg_lead, wu, wu_lead, fp, *, tm, tf, cast=None):
    M, D = h.shape
    d_ff = wg.shape[-1]
    nfull = d_ff // tf
    rem = d_ff - nfull * tf
    split = rem > 0 and rem % LANES == 0 and (nfull * tf) % rem == 0 and fp == (nfull + 1) * tf
    last = (d_ff - 1) // tf
    def wspec(lead, width, col):
        return pl.BlockSpec((None,) * len(lead) + (D, width), lambda i, *j: lead + (0, col(*j)))

    grid = (M // tm, nfull if split else fp // tf)
    in_specs = [pl.BlockSpec((tm, D), lambda i, j: (i, 0)),
                wspec(wg_lead, tf, lambda j: jnp.minimum(j, last)), wspec(wu_lead, tf, lambda j: jnp.minimum(j, last))]
    args = [h, wg, wu]
    out_specs = [pl.BlockSpec((tm, tf), lambda i, j: (i, j))]
    out_shape = [jax.ShapeDtypeStruct((M, fp), BF)]
    if cast is not None:
        job = _Cast(cast[0], cast[1], grid[0] * grid[1])
        cin, cout = job.specs(lambda i, j: i * grid[1] + j)
        in_specs.append(cin)
        args.append(job.w)
        out_specs.append(cout)
        out_shape.append(jax.ShapeDtypeStruct(job.shape, BF))
    res = pl.pallas_call(
        functools.partial(_ffn_up_kernel, d_ff=d_ff, cast=job.n_active if cast is not None else 0),
        grid=grid,
        in_specs=in_specs,
        out_specs=out_specs,
        out_shape=out_shape,
        compiler_params=_cparams(*(("arbitrary", "arbitrary") if cast is not None else ("parallel", "parallel"))),
        name="ffn_up",
    )(*args)
    a = res[0]
    if split:
        col = (nfull * tf) // rem
        a = pl.pallas_call(
            functools.partial(_ffn_tail_kernel, rem=rem),
            grid=(M // tm,),
            in_specs=[pl.BlockSpec((tm, D), lambda i: (i, 0)), wspec(wg_lead, rem, lambda: col),
                      wspec(wu_lead, rem, lambda: col), pl.BlockSpec(memory_space=pl.ANY)],
            out_specs=pl.BlockSpec((tm, tf), lambda i: (i, nfull)),
            out_shape=jax.ShapeDtypeStruct((M, fp), BF),
            input_output_aliases={3: 0},
            compiler_params=_cparams("parallel"),
            name="ffn_up_tail",
        )(h, wg, wu, a)
    return (a, res[1]) if cast is not None else a


class _FfnWeights:
    def __init__(self, wg, wu, wd):
        self.src = {'g': wg, 'u': wu, 'd': wd}
        self.ready = {}
        L = wg.shape[0]
        self.order = [(l, s) for l in range(L) for s in range(2)]

    def get(self, kind, ls):
        if (kind, ls) not in self.ready:
            self.ready[(kind, ls)] = (self.src[kind][ls].astype(BF), ())
        return self.ready[(kind, ls)]

    def offer(self, after, n_steps, kinds=('g', 'u')):
        todo = [(k, ls) for ls in self.order[self.order.index(after) + 1:] for k in kinds
                if (k, ls) not in self.ready]
        if not todo:
            return None
        kind = todo[0][0]
        jobs = [ls for k, ls in todo if k == kind][:_Cast.capacity(self.src[kind], n_steps)]
        return (kind, jobs) if jobs else None

    def request(self, offer):
        return None if offer is None else (self.src[offer[0]], offer[1])

    def deliver(self, offer, stacked):
        if offer is not None:
            for n, ls in enumerate(offer[1]):
                self.ready[(offer[0], ls)] = (stacked, (n,))


def _ffn(parts, normt, modt, tok, layer, sub, fw):
    ls = (layer, sub // 2)
    h = _prenorm(parts, normt, modt, tok, layer, sub)
    d_ff, D = fw.src['d'].shape[-2:]
    fp = -(-d_ff // FF_ALIGN) * FF_ALIGN
    tm = tok.tile(1024)
    (wg, wg_lead), (wu, wu_lead) = fw.get('g', ls), fw.get('u', ls)
    a, wd = _ffn_up(h, wg, wg_lead, wu, wu_lead, fp, tm=tm, tf=_pick(fp, 512, LANES), cast=(fw.src['d'], [ls]))
    tk = _pick(fp, max(fp // 4, LANES), LANES)
    tn = _pick(D, 1024, LANES)
    offers = []
    for x, _ in parts:
        offers.append(fw.offer(ls, (x.shape[0] // tm) * (D // tn) * (fp // tk)))
        fw.deliver(offers[-1], None)
    out, copies = _mm_residual(a, wd, (0,), parts, modt, tok, layer, 3 * sub + 2, 0.5, tm=tm, tn=tn, tk=tk,
                               casts=[fw.request(o) for o in offers], name="ffn_down")
    for o, c in zip(offers, copies):
        fw.deliver(o, c)
    return out


def _gelu_tanh(x):
    return x * (0.5 * (1.0 + jnp.tanh(math.sqrt(2.0 / math.pi) * (x + 0.044715 * (x * x * x)))))


def _softplus(x):
    return jnp.maximum(x, 0.0) + jnp.log1p(jnp.exp(-jnp.abs(x)))


def _rglru_kernel(*refs, S):
    xa_ref, ga_ref, cw_ref, cb_ref, w_ref, b_ref, lam_ref, h0_ref = refs[:8]
    y_ref, hfin_ref, pad_ref, af_ref, bf_ref, ab_ref, bb_ref = refs[-7:]
    C = LANES
    W = xa_ref.shape[1]
    P0 = SUBLANES
    pad_ref[0:P0, :] = jnp.zeros((P0, W), F32)
    pad_ref[P0 + S:P0 + S + P0, :] = jnp.zeros((P0, W), F32)
    pad_ref[P0:P0 + S, :] = xa_ref[...]
    cw = cw_ref[...]
    xc = jnp.broadcast_to(cb_ref[...], (S, W))
    for j in range(CONV_W):
        xc = xc + pad_ref[pl.ds(P0 - 2 + j, S), :] * cw[j:j + 1, :]

    lam = lam_ref[...]
    for blk in range(W // C):
        cols = slice(blk * C, (blk + 1) * C)
        xb = xc[:, cols]
        gates = jnp.dot(xb.astype(BF), w_ref[blk], preferred_element_type=F32) + b_ref[blk]
        for d, (a_ref, b_ref_) in enumerate(((af_ref, bf_ref), (ab_ref, bb_ref))):
            r = _sigmoid(gates[:, (2 * d) * C:(2 * d + 1) * C])
            i = _sigmoid(gates[:, (2 * d + 1) * C:(2 * d + 2) * C])
            log_a = (-LRU_C * r) * _softplus(-lam[d:d + 1, cols])
            a_ref[:, cols] = jnp.exp(log_a)
            th = jnp.tanh(log_a)
            one_minus_a2 = (-2.0 * th) / (1.0 - th)
            mult = jnp.where(one_minus_a2 > 0.0, one_minus_a2 * lax.rsqrt(one_minus_a2), 0.0)
            b_ref_[:, cols] = (mult * i) * xb

    rows = lax.broadcasted_iota(jnp.int32, (SUBLANES, W), 0)

    def tile_scan(a, b, down):
        for k in (1, 2, 4):
            if down:
                keep = rows >= k
                shift = k
            else:
                keep = rows < SUBLANES - k
                shift = SUBLANES - k
            a1 = jnp.where(keep, pltpu.roll(a, shift, 0), 1.0)
            b1 = jnp.where(keep, pltpu.roll(b, shift, 0), 0.0)
            b = a * b1 + b
            a = a * a1
        return a, b

    nt = S // SUBLANES
    unroll = max(SCAN_UNROLL * C // W, 1)
    unroll = unroll if nt % unroll == 0 else 1

    def body(tu, carry):
        hf, hb = carry
        tiles = []
        for u in range(unroll):
            t = tu * unroll + u
            r0 = pl.multiple_of(t * SUBLANES, SUBLANES)
            r1 = pl.multiple_of((nt - 1 - t) * SUBLANES, SUBLANES)
            fwd = tile_scan(af_ref[pl.ds(r0, SUBLANES), :], bf_ref[pl.ds(r0, SUBLANES), :], True)
            bwd = tile_scan(ab_ref[pl.ds(r1, SUBLANES), :], bb_ref[pl.ds(r1, SUBLANES), :], False)
            tiles.append((r0, fwd, r1, bwd))
        for r0, (a, b), r1, (a2, b2) in tiles:
            h = a * hf + b
            bf_ref[pl.ds(r0, SUBLANES), :] = h
            hf = jnp.broadcast_to(h[SUBLANES - 1:SUBLANES, :], (SUBLANES, W))
            g = a2 * hb + b2
            bb_ref[pl.ds(r1, SUBLANES), :] = g
            hb = jnp.broadcast_to(g[0:1, :], (SUBLANES, W))
        return hf, hb

    h0 = h0_ref[...]
    hf, hb = lax.fori_loop(0, nt // unroll, body, (jnp.broadcast_to(h0[0:1, :], (SUBLANES, W)),
                                                    jnp.broadcast_to(h0[1:2, :], (SUBLANES, W))))
    hfin_ref[0:1, :] = hf[0:1, :]
    hfin_ref[1:2, :] = hb[0:1, :]
    y_ref[...] = (_gelu_tanh(ga_ref[...]) * (bf_ref[...] + bb_ref[...])).astype(y_ref.dtype)


def _rglru(z, row0, nb, S, d_rnn, conv_w, conv_b, wcat, bcat, lam, h0, out_buf, out_shape):
    C = LANES
    nblk = math.gcd(d_rnn // C, RGLRU_BLOCKS)
    W = C * nblk
    ncb = d_rnn // W
    roff = row0 // S
    in_specs = [
        pl.BlockSpec((S, W), lambda b, c: (roff + b, c)),
        pl.BlockSpec((S, W), lambda b, c: (roff + b, ncb + c)),
        pl.BlockSpec((CONV_W, W), lambda b, c: (0, c)),
        pl.BlockSpec((1, W), lambda b, c: (0, c)),
        pl.BlockSpec((nblk, C, 4 * C), lambda b, c: (c, 0, 0)),
        pl.BlockSpec((nblk, 1, 4 * C), lambda b, c: (c, 0, 0)),
        pl.BlockSpec((2, W), lambda b, c: (0, c)),
        pl.BlockSpec((None, 2, W), lambda b, c: (b, 0, c)),
    ]
    args = [z, z, conv_w, conv_b.reshape(1, d_rnn), wcat, bcat, lam, h0]
    aliases = _in_place(in_specs, args, out_buf)
    return pl.pallas_call(
        functools.partial(_rglru_kernel, S=S),
        grid=(nb, ncb),
        in_specs=in_specs,
        out_specs=[
            pl.BlockSpec((S, W), lambda b, c: (roff + b, c)),
            pl.BlockSpec((None, 2, W), lambda b, c: (b, 0, c)),
        ],
        out_shape=[
            jax.ShapeDtypeStruct(out_shape, BF),
            jax.ShapeDtypeStruct((nb, 2, d_rnn), F32),
        ],
        input_output_aliases=aliases,
        scratch_shapes=[pltpu.VMEM((S + 2 * SUBLANES, W), F32)] + [pltpu.VMEM((S, W), F32)] * 4,
        compiler_params=_cparams("parallel", "parallel"),
        name="rglru",
    )(*args)


def _rope_tables(n_tok):
    t = jnp.arange(n_tok, dtype=jnp.int32)
    row, col = t // GRID_W, t % GRID_W
    n = 16
    inv = ROPE_THETA ** (-jnp.arange(n, dtype=F32) / n)
    ang_r = row.astype(F32)[:, None] * inv
    ang_c = col.astype(F32)[:, None] * inv

    def grp(ang):
        c, s = jnp.cos(ang), jnp.sin(ang)
        return jnp.concatenate([c, c], -1), jnp.concatenate([-s, s], -1)
    cr, sr = grp(ang_r)
    cc, sc = grp(ang_c)
    return jnp.concatenate([cr, cc, cr, cc], -1), jnp.concatenate([sr, sc, sr, sc], -1)


def _rope_kernel(x_ref, cos_ref, sin_ref, o_ref):
    cos = cos_ref[...]
    sin = sin_ref[...]
    first = (lax.broadcasted_iota(jnp.int32, cos.shape, 1) % 32) < 16
    for c in range(x_ref.shape[1] // LANES):
        x = x_ref[:, c * LANES:(c + 1) * LANES].astype(F32)
        partner = jnp.where(first, pltpu.roll(x, LANES - 16, 1), pltpu.roll(x, 16, 1))
        o_ref[:, c * LANES:(c + 1) * LANES] = (x * cos + partner * sin).astype(o_ref.dtype)


def _rope(x, cos, sin, row0, rows, col0, width, ds):
    tm = _pick(ds, 256, SUBLANES)
    tw = _pick(width, 1024, LANES)
    roff, coff, per = row0 // tm, col0 // tw, ds // tm
    return pl.pallas_call(
        _rope_kernel,
        grid=(rows // tm, width // tw),
        in_specs=[
            pl.BlockSpec((tm, tw), lambda i, j: (roff + i, coff + j)),
            pl.BlockSpec((tm, LANES), lambda i, j: (i % per, 0)),
            pl.BlockSpec((tm, LANES), lambda i, j: (i % per, 0)),
        ],
        out_specs=pl.BlockSpec((tm, tw), lambda i, j: (i, j)),
        out_shape=jax.ShapeDtypeStruct((rows, width), BF),
        compiler_params=_cparams("parallel", "parallel"),
        name="rope",
    )(x, cos, sin)


def _dattn_kernel(*refs, P, S, tq, G, lam_init):
    if P:
        q_ref, kc_ref, vc_ref, kl_ref, vl_ref, lq_ref, lk_ref, g_ref = refs[:8]
    else:
        q_ref, kl_ref, vl_ref, lq_ref, lk_ref, g_ref = refs[:6]
    o_ref, kk, vv = refs[-3:]
    C = 2 * DK_B

    @pl.when(pl.program_id(2) == 0)
    def _():
        if P:
            kk[0:P, :] = kc_ref[...].astype(BF)
            vv[0:P, :] = vc_ref[...].astype(BF)
        kk[P:P + S, :] = kl_ref[...].astype(BF)
        vv[P:P + S, :] = vl_ref[...].astype(BF)

    el = jnp.exp(jnp.sum(lq_ref[...] * lk_ref[...], axis=-1, keepdims=True))
    lam = el[0:1, :] - el[1:2, :] + lam_init
    sub = min(DATT_ROWS, tq)
    chains = []
    for g in range(G):
        cols = slice(g * C, (g + 1) * C)
        for r0 in range(0, tq, sub):
            q = q_ref[r0:r0 + sub, cols].astype(F32)
            lane = lax.broadcasted_iota(jnp.int32, q.shape, 1)
            qq = jnp.concatenate([jnp.where(lane < DK_B, q, 0.0), jnp.where(lane >= DK_B, q, 0.0)],
                                 axis=0).astype(BF)
            s = lax.dot_general(qq, kk[:, cols], (((1,), (1,)), ((), ())), preferred_element_type=F32)
            chains.append((cols, r0, s))
    for cols, r0, s in chains:
        e = jnp.exp2((s - jnp.max(s, axis=-1, keepdims=True)) * ((DK_B ** -0.5) * LOG2E))
        l = jnp.sum(e, axis=-1, keepdims=True)
        a = e[:sub, :] - e[sub:, :] * (lam * l[:sub, :] / l[sub:, :])
        o = jnp.dot(a.astype(BF), vv[:, cols], preferred_element_type=F32) / l[:sub, :]
        o_ref[r0:r0 + sub, cols] = ((_rms(o) * g_ref[...]) * (1.0 - lam_init)).astype(o_ref.dtype)


def _dattn(q, qrow0, qcol0, k_lat, krow0, kcol0, v_lat, vrow0, vcol0, cache, nb, S, lq, lk, g, lam_init,
           out_buf, orow0, ocol0):
    C = 2 * DK_B
    tq = _pick(S, 512, SUBLANES)
    nq = S // tq
    G = math.gcd(H_B, DATT_HEADS[0] if nq == 1 else DATT_HEADS[1])
    W = C * G
    P = cache[0].shape[1] if cache is not None else 0
    qr, kr, vr, orr = qrow0 // tq, krow0 // S, vrow0 // S, orow0 // tq
    qc, kc, vc, oc = qcol0 // W, kcol0 // W, vcol0 // W, ocol0 // W
    in_specs = [pl.BlockSpec((tq, W), lambda b, h, i: (qr + b * nq + i, qc + h))]
    args = [q]
    if P:
        in_specs += [pl.BlockSpec((None, P, W), lambda b, h, i: (b, 0, h)),
                     pl.BlockSpec((None, P, W), lambda b, h, i: (b, 0, h))]
        args += [cache[0], cache[1]]
    in_specs += [
        pl.BlockSpec((S, W), lambda b, h, i: (kr + b, kc + h)),
        pl.BlockSpec((S, W), lambda b, h, i: (vr + b, vc + h)),
        pl.BlockSpec((2, DK_B), lambda b, h, i: (0, 0)),
        pl.BlockSpec((2, DK_B), lambda b, h, i: (0, 0)),
        pl.BlockSpec((1, C), lambda b, h, i: (0, 0)),
    ]
    args += [k_lat, v_lat, lq, lk, g.reshape(1, C)]
    aliases = _in_place(in_specs, args, out_buf)
    return pl.pallas_call(
        functools.partial(_dattn_kernel, P=P, S=S, tq=tq, G=G, lam_init=lam_init),
        grid=(nb, H_B // G, nq),
        in_specs=in_specs,
        out_specs=pl.BlockSpec((tq, W), lambda b, h, i: (orr + b * nq + i, oc + h)),
        out_shape=jax.ShapeDtypeStruct(out_buf.shape, BF),
        input_output_aliases=aliases,
        scratch_shapes=[pltpu.VMEM((P + S, W), BF), pltpu.VMEM((P + S, W), BF)],
        compiler_params=_cparams("parallel", "parallel", "arbitrary"),
        name="diff_attn",
    )(*args)


def _mla_in_kernel(h_ref, w_ref, gq_ref, gkv_ref, cq_ref, ckv_ref, kr_ref, *, ql, kvl):
    z = jnp.dot(h_ref[...], w_ref[...], preferred_element_type=F32)
    cq_ref[...] = (_rms(z[:, :ql]) * gq_ref[...]).astype(cq_ref.dtype)
    ckv_ref[...] = _rms(z[:, ql:ql + kvl]) * gkv_ref[...]
    kr_ref[...] = z[:, ql + kvl:]


def _mla_in(h, w, gq, gkv, ql, kvl, tm):
    M, D = h.shape
    N = w.shape[1]
    return pl.pallas_call(
        functools.partial(_mla_in_kernel, ql=ql, kvl=kvl),
        grid=(M // tm,),
        in_specs=[
            pl.BlockSpec((tm, D), lambda i: (i, 0)),
            pl.BlockSpec((D, N), lambda i: (0, 0)),
            pl.BlockSpec((1, ql), lambda i: (0, 0)),
            pl.BlockSpec((1, kvl), lambda i: (0, 0)),
        ],
        out_specs=[
            pl.BlockSpec((tm, ql), lambda i: (i, 0)),
            pl.BlockSpec((tm, kvl), lambda i: (i, 0)),
            pl.BlockSpec((tm, LANES), lambda i: (i, 0)),
        ],
        out_shape=[
            jax.ShapeDtypeStruct((M, ql), BF),
            jax.ShapeDtypeStruct((M, kvl), F32),
            jax.ShapeDtypeStruct((M, LANES), F32),
        ],
        compiler_params=_cparams("parallel"),
        name="mla_in",
    )(h, w, gq.reshape(1, ql), gkv.reshape(1, kvl))


def _mla_kernel(*refs, P, S, tq, G):
    if P:
        qn_ref, qr_ref, kvc_ref, krc_ref, kvl_ref, krl_ref = refs[:6]
        segs = ((0, P, kvc_ref, krc_ref), (P, S, kvl_ref, krl_ref))
    else:
        qn_ref, qr_ref, kvl_ref, krl_ref = refs[:4]
        segs = ((0, S, kvl_ref, krl_ref),)
    o_ref, ks, vs = refs[-3:]
    C = LANES

    @pl.when(pl.program_id(2) == 0)
    def _():
        for r0, n, kv_ref, kr_ref in segs:
            kr = kr_ref[...].astype(F32)
            kr2 = (kr + pltpu.roll(kr, ROPE_C, 1)).astype(BF)
            for hd in range(2 * G):
                ks[hd, r0:r0 + n, 0:C] = kv_ref[:, 2 * hd * C:(2 * hd + 1) * C]
                ks[hd, r0:r0 + n, C:2 * C] = kr2
                vs[hd, r0:r0 + n, 0:C] = kv_ref[:, (2 * hd + 1) * C:(2 * hd + 2) * C]
                vs[hd, r0:r0 + n, C:2 * C] = jnp.ones((n, C), BF)

    c = ((NOPE_C + ROPE_C) ** -0.5) * LOG2E
    chains = []
    sub = min(ATT_ROWS, tq)
    for r0 in range(0, tq, sub):
        rows = slice(r0, r0 + sub)
        for p in range(G):
            qr = qr_ref[rows, p * C:(p + 1) * C].astype(F32)
            lane = lax.broadcasted_iota(jnp.int32, qr.shape, 1)
            for hh in range(2):
                hd = 2 * p + hh
                keep = (lane < ROPE_C) if hh == 0 else (lane >= ROPE_C)
                qf = jnp.concatenate([qn_ref[rows, hd * C:(hd + 1) * C], jnp.where(keep, qr, 0.0).astype(BF)],
                                     axis=1)
                s = lax.dot_general(qf, ks[hd], (((1,), (1,)), ((), ())), preferred_element_type=F32)
                chains.append((rows, hd, s))
    for rows, hd, s in chains:
        e = jnp.exp2((s - jnp.max(s, axis=-1, keepdims=True)) * c).astype(BF)
        oe = jnp.dot(e, vs[hd], preferred_element_type=F32)
        o_ref[rows, hd * C:(hd + 1) * C] = (oe[:, 0:C] / oe[:, C:2 * C]).astype(o_ref.dtype)


def _mla_attn(qn, qr, qr_row0, kv_lat, kr_lat, kr_row0, row0, cache, nb, S, out_buf, out_shape):
    C = LANES
    tq = _pick(S, 512, SUBLANES)
    nq = S // tq
    G = math.gcd(H_C // 2, MLA_PAIRS[0] if nq == 1 else MLA_PAIRS[1])
    P = cache[0].shape[0] // nb if cache is not None else 0
    r_q, r_qr, r_kv, r_kr = row0 // tq, qr_row0 // tq, row0 // S, kr_row0 // S
    in_specs = [pl.BlockSpec((tq, 2 * C * G), lambda b, h, i: (r_q + b * nq + i, h)),
                pl.BlockSpec((tq, C * G), lambda b, h, i: (r_qr + b * nq + i, h))]
    args = [qn, qr]
    if P:
        in_specs += [pl.BlockSpec((P, 4 * C * G), lambda b, h, i: (b, h)),
                     pl.BlockSpec((P, C), lambda b, h, i: (b, 0))]
        args += [cache[0], cache[1]]
    in_specs += [pl.BlockSpec((S, 4 * C * G), lambda b, h, i: (r_kv + b, h)),
                 pl.BlockSpec((S, C), lambda b, h, i: (r_kr + b, 0))]
    args += [kv_lat, kr_lat]
    aliases = _in_place(in_specs, args, out_buf)
    return pl.pallas_call(
        functools.partial(_mla_kernel, P=P, S=S, tq=tq, G=G),
        grid=(nb, H_C // (2 * G), nq),
        in_specs=in_specs,
        out_specs=pl.BlockSpec((tq, 2 * C * G), lambda b, h, i: (r_q + b * nq + i, h)),
        out_shape=jax.ShapeDtypeStruct(out_shape, BF),
        input_output_aliases=aliases,
        scratch_shapes=[pltpu.VMEM((2 * G, P + S, 2 * C), BF)] * 2,
        compiler_params=_cparams("parallel", "parallel", "arbitrary"),
        name="mla_attn",
    )(*args)


def kernel(x_prompt, x_sample, c, c_ctx, state_rglru, cache_dk, cache_dv, cache_ckv, cache_krope,
           mod_w, mod_b, norm_g, final_g, ffn_wg, ffn_wu, ffn_wd,
           ev_w_in, ev_conv_w, ev_conv_b, ev_wa, ev_ba, ev_wx, ev_bx, ev_lam, ev_lq, ev_lk,
           ev_subln_g, ev_w_out, od_w_in, od_qnorm_g, od_w_uq, od_kvnorm_g, od_w_ukv, od_w_out):
    assert NOPE_C == LANES and V_C == LANES and 2 * DK_B == LANES and 2 * ROPE_C == LANES
    NBP, SP, D = x_prompt.shape
    NB, DS, _ = x_sample.shape
    PAST = cache_dk.shape[2]
    L = mod_w.shape[0]
    MP, MS = NBP * SP, NB * DS
    tok = _Tokens(MP, DS, NB, 1 + NB)
    M = tok.m
    d_rnn = ev_lam.shape[-1]
    d_ff = ffn_wg.shape[-1]
    ql, kvl = od_qnorm_g.shape[-1], od_kvnorm_g.shape[-1]
    assert d_rnn // H_A == LANES

    n_cond = -(-(1 + NB) // SUBLANES) * SUBLANES
    cond = jnp.concatenate([c_ctx[None, :], c, jnp.zeros((n_cond - 1 - NB, D), F32)], axis=0)
    mod = _modulation(cond, mod_w, mod_b)
    modt = mod[:, :1 + NB].reshape(L * (1 + NB) * N_MOD, 1, D)
    normt = norm_g.reshape(L * 3, 1, D)

    fw = _FfnWeights(ffn_wg, ffn_wu, ffn_wd)

    cos, sin = _rope_tables(DS)
    tm = tok.tile(1024)
    new = {}
    parts = [(x_prompt.reshape(MP, D), 0), (x_sample.reshape(MS, D), MP)]

    for l in range(L):
        x = _ffn(parts, normt, modt, tok, l, 0, fw)
        h = _prenorm([(x, 0)], normt, modt, tok, l, 1)
        if l % 2 == 0:
            e = l // 2
            lam_init = 0.8 - 0.6 * math.exp(-0.3 * l)
            nh = H_B * LANES
            w_in = ev_w_in[e].astype(BF)
            w_qk = w_in[:, 2 * d_rnn:2 * d_rnn + 2 * nh]
            xg = _mm(h, w_in[:, :2 * d_rnn], F32, tm=tm, name="ev_in_rec")
            v = _mm(h, w_in[:, 2 * d_rnn + 2 * nh:], F32, tm=tm, name="ev_in_v")
            qk_p = _mm(h, w_qk, F32, tm=tm, rows=MP, name="ev_in_qk")
            qk_s = _mm(h, w_qk, BF, tm=tm, row0=MP, rows=MS, rope=(cos, sin, DS), name="ev_in_qk_rope")
            wcat = jnp.concatenate([ev_wa[e, 0], ev_wx[e, 0], ev_wa[e, 1], ev_wx[e, 1]], axis=-1).astype(BF)
            bcat = jnp.concatenate([t.reshape(H_A, 1, LANES) for t in
                                    (ev_ba[e, 0], ev_bx[e, 0], ev_ba[e, 1], ev_bx[e, 1])], axis=-1)
            mix_shape = (M, d_rnn + nh)
            rg = functools.partial(_rglru, xg, d_rnn=d_rnn, conv_w=ev_conv_w[e], conv_b=ev_conv_b[e],
                                   wcat=wcat, bcat=bcat, lam=ev_lam[e], out_shape=mix_shape)
            y_in, hfin = rg(row0=0, nb=NBP, S=SP, h0=jnp.zeros((NBP, 2, d_rnn), F32), out_buf=None)
            y_in, _ = rg(row0=MP, nb=NB, S=DS, h0=state_rglru[:, e], out_buf=y_in)
            y_in = _dattn(qk_p, 0, 0, qk_p, 0, nh, v, 0, 0, None, NBP, SP,
                          ev_lq[e], ev_lk[e], ev_subln_g[e], lam_init, y_in, 0, d_rnn)
            cache = (cache_dk[:, e].reshape(NB, PAST, nh), cache_dv[:, e].reshape(NB, PAST, nh))
            y_in = _dattn(qk_s, 0, 0, qk_s, 0, nh, v, MP, 0, cache, NB, DS,
                          ev_lq[e], ev_lk[e], ev_subln_g[e], lam_init, y_in, MP, d_rnn)
            w_out = ev_w_out[e].astype(BF)
            new.setdefault('rec', []).append(hfin)
            new.setdefault('dk', []).append(qk_p[:, nh:].reshape(NBP, SP, H_B, 2 * DK_B))
            new.setdefault('dv', []).append(v[:MP].reshape(NBP, SP, H_B, 2 * DK_B))
        else:
            o = l // 2
            w_in = jnp.pad(od_w_in[o].astype(BF), ((0, 0), (0, LANES - ROPE_C)))
            cqn, ckvn, kr = _mla_in(h, w_in, od_qnorm_g[o], od_kvnorm_g[o], ql, kvl, tok.tile(512))
            w_uq = od_w_uq[o].astype(BF).reshape(ql, H_C, NOPE_C + ROPE_C)
            tm2, tn2 = tok.tile(2048), _pick(H_C * NOPE_C, 2048, LANES)
            qn = _mm(cqn, w_uq[:, :, :NOPE_C].reshape(ql, H_C * NOPE_C), BF, tm=tm2, tn=tn2, name="uq_nope")
            w_uq_r = w_uq[:, :, NOPE_C:].reshape(ql, H_C * ROPE_C)
            qr_p = _mm(cqn, w_uq_r, BF, tm=tm, rows=MP, name="uq_rope")
            qr_s = _mm(cqn, w_uq_r, BF, tm=tm, row0=MP, rows=MS, rope=(cos, sin, DS), name="uq_rope_rot")
            w_ukv = od_w_ukv[o].astype(BF)
            kv = _mm(ckvn, w_ukv, BF, tm=tm2, tn=tn2, name="ukv")
            mix_shape = (M, H_C * V_C)
            y_in = _mla_attn(qn, qr_p, 0, kv, kr, 0, 0, None, NBP, SP, None, mix_shape)
            kr_s = _rope(kr, cos, sin, MP, MS, 0, LANES, DS)
            ckv_c = cache_ckv[:, o].reshape(NB * PAST, kvl)
            kv_c = _mm(ckv_c, w_ukv, BF, tm=_pick(NB * PAST, 1024, SUBLANES), name="ukv_ctx")
            kr_c = jnp.pad(cache_krope[:, o].reshape(NB * PAST, ROPE_C), ((0, 0), (0, LANES - ROPE_C)))
            y_in = _mla_attn(qn, qr_s, 0, kv, kr_s, 0, MP, (kv_c, kr_c), NB, DS, y_in, mix_shape)
            w_out = od_w_out[o].astype(BF)
            new.setdefault('ckv', []).append(ckvn[:MP].reshape(NBP, SP, kvl))
            new.setdefault('kr', []).append(kr[:MP, :ROPE_C].reshape(NBP, SP, ROPE_C))
        x = _mm_residual(y_in, w_out, (), [(x, 0)], modt, tok, l, 5, 1.0, tm=tm, tn=_pick(D, 1024, LANES),
                         name="mix_out")
        x = _ffn([(x, 0)], normt, modt, tok, l, 2, fw)
        parts = [(x, 0)]

    y_prompt = _final_norm(x, final_g, 0, MP).reshape(NBP, SP, D)
    y_sample = _final_norm(x, final_g, MP, MS).reshape(NB, DS, D)
    return (y_prompt, y_sample, jnp.stack(new['rec'], axis=1), jnp.stack(new['dk'], axis=1),
            jnp.stack(new['dv'], axis=1), jnp.stack(new['ckv'], axis=1), jnp.stack(new['kr'], axis=1))
```

```python
import functools
import math

import jax
import jax.numpy as jnp
from jax import lax
from jax.experimental import pallas as pl
from jax.experimental.pallas import tpu as pltpu

BF = jnp.bfloat16
F32 = jnp.float32

GRID_W = 64
EPS = 1e-6
ROPE_THETA = 10000.0
N_MOD = 9
H_A = 16
CONV_W = 4
LRU_C = 8.0
H_B = 16
DK_B = 64
H_C = 32
NOPE_C = 128
ROPE_C = 64
V_C = 128

LANES = 128
SUBLANES = 8
VMEM_LIMIT = 56 * 1024 * 1024
FF_ALIGN = 1024
LOG2E = math.log2(math.e)
ATT_ROWS = 256
DATT_ROWS = 256
DATT_HEADS = (4, 2)
MLA_PAIRS = (8, 1)
SCAN_UNROLL = 4
NORM_ROWS = 512
RGLRU_BLOCKS = 2
CAST_BYTES = 2 * 1024 * 1024


def _cparams(*sem):
    return pltpu.CompilerParams(dimension_semantics=sem, vmem_limit_bytes=VMEM_LIMIT)


def _pick(n, pref, mult):
    best = None
    d = mult
    while d <= min(n, pref):
        if n % d == 0:
            best = d
        d += mult
    return n if best is None else best


def _sigmoid(x):
    return 1.0 / (1.0 + jnp.exp2(x * (-LOG2E)))


def _rms(x):
    return x * lax.rsqrt(jnp.mean(x * x, axis=-1, keepdims=True) + EPS)


def _in_place(in_specs, args, out_buf):
    if out_buf is None:
        return {}
    in_specs.append(pl.BlockSpec(memory_space=pl.ANY))
    args.append(out_buf)
    return {len(args) - 1: 0}


class _Cast:
    def __init__(self, w, jobs, n_steps):
        R, C = w.shape[-2:]
        tr = 2 * SUBLANES
        ok = [t for t in range(tr, R + 1, tr) if R % t == 0 and (R // t) * len(jobs) <= n_steps]
        big = [t for t in ok if 4 * t * C >= CAST_BYTES]
        self.tr = big[0] if big else ok[-1]
        self.w, self.jobs, self.nb, self.shape = w, jobs, R // self.tr, (len(jobs), R, C)
        self.n_active = self.nb * len(jobs)

    @staticmethod
    def capacity(w, n_steps):
        R, C = w.shape[-2:]
        return min(2, n_steps // max(1, (4 * R * C) // CAST_BYTES))

    def specs(self, lin):
        nb, jobs, tr, C = self.nb, self.jobs, self.tr, self.shape[2]

        def split(*g):
            b = jnp.minimum(lin(*g), nb * len(jobs) - 1)
            return b // nb, b % nb

        def src(*g):
            j, r = split(*g)
            l, sl = jobs[0]
            for n, (l2, s2) in enumerate(jobs[1:], 1):
                l, sl = jnp.where(j == n, l2, l), jnp.where(j == n, s2, sl)
            return (l, sl, r, 0)
        return (pl.BlockSpec((None, None, tr, C), src),
                pl.BlockSpec((None, tr, C), lambda *g: split(*g) + (0,)))


def _mod_kernel(c_ref, w_ref, b_ref, o_ref):
    k = pl.program_id(2)
    c = c_ref[...]
    s = (c * _sigmoid(c)).astype(BF)
    part = jnp.dot(s, w_ref[...].astype(BF), preferred_element_type=F32)

    @pl.when(k == 0)
    def _():
        o_ref[...] = part + b_ref[...]

    @pl.when(k > 0)
    def _():
        o_ref[...] += part


def _modulation(cond, mod_w, mod_b):
    R, D = cond.shape
    L, _, N = mod_w.shape
    tn = _pick(N, 2048, LANES)
    tk = _pick(D, 1024, LANES)
    return pl.pallas_call(
        _mod_kernel,
        grid=(L, N // tn, D // tk),
        in_specs=[
            pl.BlockSpec((R, tk), lambda l, n, k: (0, k)),
            pl.BlockSpec((None, tk, tn), lambda l, n, k: (l, k, n)),
            pl.BlockSpec((None, 1, tn), lambda l, n, k: (l, 0, n)),
        ],
        out_specs=pl.BlockSpec((None, R, tn), lambda l, n, k: (l, 0, n)),
        out_shape=jax.ShapeDtypeStruct((L, R, N), F32),
        compiler_params=_cparams("parallel", "parallel", "arbitrary"),
        name="modulation",
    )(cond, mod_w, mod_b.reshape(L, 1, N))


class _Tokens:
    def __init__(self, mp, ds, nb, n_rows):
        self.mp, self.ds, self.nb, self.n_rows = mp, ds, nb, n_rows
        self.m = mp + ds * nb

    def tile(self, pref):
        return _pick(math.gcd(self.mp, self.ds), pref, SUBLANES)

    def mod_index(self, layer, chunk, tm):
        mp, ds, n_rows = self.mp, self.ds, self.n_rows

        def f(i):
            r = jnp.where(i * tm < mp, 0, 1 + (i * tm - mp) // ds)
            return (layer * n_rows + r) * N_MOD + chunk
        return f


def _prenorm_kernel(x_ref, g_ref, sh_ref, sc_ref, *rest):
    o_ref = rest[-1]
    y = _rms(x_ref[...]) * g_ref[...]
    o_ref[...] = (y * (1.0 + sc_ref[...]) + sh_ref[...]).astype(o_ref.dtype)


def _prenorm(parts, normt, modt, tok, layer, sub):
    h = None
    for x, row0 in parts:
        rows, D = x.shape
        tm = tok.tile(NORM_ROWS)
        off = row0 // tm
        sh = tok.mod_index(layer, 3 * sub, tm)
        sc = tok.mod_index(layer, 3 * sub + 1, tm)
        in_specs = [
            pl.BlockSpec((tm, D), lambda i: (i, 0)),
            pl.BlockSpec((None, 1, D), lambda i: (layer * 3 + sub, 0, 0)),
            pl.BlockSpec((None, 1, D), lambda i: (sh(off + i), 0, 0)),
            pl.BlockSpec((None, 1, D), lambda i: (sc(off + i), 0, 0)),
        ]
        args = [x, normt, modt, modt]
        aliases = _in_place(in_specs, args, h)
        h = pl.pallas_call(
            _prenorm_kernel,
            grid=(rows // tm,),
            in_specs=in_specs,
            out_specs=pl.BlockSpec((tm, D), lambda i: (off + i, 0)),
            out_shape=jax.ShapeDtypeStruct((tok.m, D), BF),
            input_output_aliases=aliases,
            compiler_params=_cparams("parallel"),
            name="prenorm",
        )(*args)
    return h


def _final_norm_kernel(x_ref, g_ref, o_ref):
    o_ref[...] = _rms(x_ref[...]) * g_ref[...]


def _final_norm(x, g, row0, rows):
    D = x.shape[1]
    tm = _pick(math.gcd(row0, rows) if row0 else rows, NORM_ROWS, SUBLANES)
    off = row0 // tm
    return pl.pallas_call(
        _final_norm_kernel,
        grid=(rows // tm,),
        in_specs=[
            pl.BlockSpec((tm, D), lambda i: (off + i, 0)),
            pl.BlockSpec((1, D), lambda i: (0, 0)),
        ],
        out_specs=pl.BlockSpec((tm, D), lambda i: (i, 0)),
        out_shape=jax.ShapeDtypeStruct((rows, D), F32),
        compiler_params=_cparams("parallel"),
        name="final_norm",
    )(x, g.reshape(1, D))


def _rope_lanes(x, cos, sin, first):
    partner = jnp.where(first, pltpu.roll(x, LANES - 16, 1), pltpu.roll(x, 16, 1))
    return x * cos + partner * sin


def _mm_kernel(*refs, nk, residual, coef, rope=False, last_rows=None, cast=0):
    a_ref, w_ref = refs[:2]
    if residual:
        r_ref, g_ref = refs[2:4]
    if rope:
        cos_ref, sin_ref = refs[2:4]
    refs, acc_ref = (refs[:-1], refs[-1]) if nk > 1 else (refs, None)
    if cast:
        step = (pl.program_id(0) * pl.num_programs(1) + pl.program_id(1)) * nk + pl.program_id(2)

        @pl.when(step < cast)
        def _():
            refs[-1][...] = refs[-3][...].astype(BF)
        refs = refs[:-1]
    o_ref = refs[-1]

    def finish(acc):
        if residual:
            o_ref[...] = r_ref[...] + (coef * g_ref[...]) * acc
        elif rope:
            cos, sin = cos_ref[...], sin_ref[...]
            first = (lax.broadcasted_iota(jnp.int32, cos.shape, 1) % 32) < 16
            for c in range(acc.shape[1] // LANES):
                cols = slice(c * LANES, (c + 1) * LANES)
                o_ref[:, cols] = _rope_lanes(acc[:, cols], cos, sin, first).astype(o_ref.dtype)
        else:
            o_ref[...] = acc.astype(o_ref.dtype)

    def part(valid_rows=None):
        if valid_rows is None:
            return jnp.dot(a_ref[...].astype(BF), w_ref[...], preferred_element_type=F32)
        if valid_rows % LANES == 0:
            return jnp.dot(a_ref[:, :valid_rows].astype(BF), w_ref[:valid_rows, :], preferred_element_type=F32)
        w = w_ref[...]
        w = jnp.where(lax.broadcasted_iota(jnp.int32, w.shape, 0) < valid_rows, w, jnp.zeros_like(w))
        return jnp.dot(a_ref[...].astype(BF), w, preferred_element_type=F32)

    if nk == 1:
        finish(part(last_rows))
        return
    k = pl.program_id(2)

    @pl.when(k == 0)
    def _():
        acc_ref[...] = part()

    @pl.when(jnp.logical_and(k > 0, k < nk - 1))
    def _():
        acc_ref[...] += part()

    @pl.when(k == nk - 1)
    def _():
        finish(acc_ref[...] + part(last_rows))


def _mm(a, w, out_dtype, *, tm, tn=None, tk=None, row0=0, rows=None, rope=None, cast=None, name="mm"):
    K, N = w.shape
    tn = _pick(N, 1024, LANES) if tn is None else tn
    rows = a.shape[0] if rows is None else rows
    tk = K if tk is None else tk
    nk = K // tk
    off = row0 // tm
    in_specs = [
        pl.BlockSpec((tm, tk), lambda i, j, k: (off + i, k)),
        pl.BlockSpec((tk, tn), lambda i, j, k: (k, j)),
    ]
    args = [a, w]
    if rope is not None:
        cos, sin, ds = rope
        per = ds // tm
        in_specs += [pl.BlockSpec((tm, LANES), lambda i, j, k: (i % per, 0))] * 2
        args += [cos, sin]
    grid = (rows // tm, N // tn, nk)
    out_specs = [pl.BlockSpec((tm, tn), lambda i, j, k: (i, j))]
    out_shape = [jax.ShapeDtypeStruct((rows, N), out_dtype)]
    if cast is not None:
        job = _Cast(cast[0], cast[1], grid[0] * grid[1] * grid[2])
        cin, cout = job.specs(lambda i, j, k: (i * grid[1] + j) * nk + k)
        in_specs.append(cin)
        args.append(job.w)
        out_specs.append(cout)
        out_shape.append(jax.ShapeDtypeStruct(job.shape, BF))
    res = pl.pallas_call(
        functools.partial(_mm_kernel, nk=nk, residual=False, coef=None, rope=rope is not None,
                          cast=job.n_active if cast is not None else 0),
        grid=grid,
        in_specs=in_specs,
        out_specs=out_specs,
        out_shape=out_shape,
        scratch_shapes=[pltpu.VMEM((tm, tn), F32)] if nk > 1 else [],
        compiler_params=_cparams(*(("arbitrary",) * 3 if cast is not None else ("parallel", "parallel", "arbitrary"))),
        name=name,
    )(*args)
    return res if cast is not None else res[0]


def _mm_residual(a, w, w_lead, parts, modt, tok, layer, chunk, coef, *, tm, tn, tk=None, casts=None,
                 name="mm_res"):
    K, N = a.shape[1], w.shape[-1]
    tk = K if tk is None else tk
    nk = K // tk
    last_rows = w.shape[-2] - (nk - 1) * tk if w.shape[-2] < K else None
    assert last_rows is None or 0 < last_rows < tk
    gate = tok.mod_index(layer, chunk, tm)
    out = None
    cast_outs = []
    for n, (x, row0) in enumerate(parts):
        off = row0 // tm
        grid = (x.shape[0] // tm, N // tn, nk)
        in_specs = [
            pl.BlockSpec((tm, tk), lambda i, j, k: (off + i, k)),
            pl.BlockSpec((None,) * len(w_lead) + (tk, tn), lambda i, j, k: w_lead + (k, j)),
            pl.BlockSpec((tm, tn), lambda i, j, k: (i, j)),
            pl.BlockSpec((None, 1, tn), lambda i, j, k: (gate(off + i), 0, j)),
        ]
        args = [a, w, x, modt]
        aliases = _in_place(in_specs, args, out)
        out_specs = [pl.BlockSpec((tm, tn), lambda i, j, k: (off + i, j))]
        out_shape = [jax.ShapeDtypeStruct((tok.m, N), F32)]
        cast = casts[n] if casts else None
        if cast is not None:
            job = _Cast(cast[0], cast[1], grid[0] * grid[1] * grid[2])
            cin, cout = job.specs(lambda i, j, k: (i * grid[1] + j) * nk + k)
            in_specs.append(cin)
            args.append(job.w)
            out_specs.append(cout)
            out_shape.append(jax.ShapeDtypeStruct(job.shape, BF))
        res = pl.pallas_call(
            functools.partial(_mm_kernel, nk=nk, residual=True, coef=coef, last_rows=last_rows,
                              cast=job.n_active if cast is not None else 0),
            grid=grid,
            in_specs=in_specs,
            out_specs=out_specs,
            out_shape=out_shape,
            input_output_aliases=aliases,
            scratch_shapes=[pltpu.VMEM((tm, tn), F32)] if nk > 1 else [],
            compiler_params=_cparams(*(("arbitrary",) * 3 if cast is not None
                                       else ("parallel", "parallel", "arbitrary"))),
            name=name,
        )(*args)
        out = res[0]
        cast_outs.append(res[1] if cast is not None else None)
    return (out, cast_outs) if casts else out


def _ffn_up_kernel(h_ref, wg_ref, wu_ref, *rest, d_ff, cast=0):
    if cast:
        @pl.when(pl.program_id(0) * pl.num_programs(1) + pl.program_id(1) < cast)
        def _():
            rest[2][...] = rest[0][...].astype(BF)
        rest = rest[1:]
    o_ref = rest[0]
    h = h_ref[...]
    g = jnp.dot(h, wg_ref[...], preferred_element_type=F32)
    u = jnp.dot(h, wu_ref[...], preferred_element_type=F32)
    tf = o_ref.shape[1]
    col = pl.program_id(1) * tf + lax.broadcasted_iota(jnp.int32, (1, tf), 1)
    o_ref[...] = jnp.where(col < d_ff, (g * _sigmoid(g)) * u, 0.0).astype(o_ref.dtype)


def _ffn_tail_kernel(h_ref, wg_ref, wu_ref, buf_ref, o_ref, *, rem):
    h = h_ref[...]
    g = jnp.dot(h, wg_ref[...], preferred_element_type=F32)
    u = jnp.dot(h, wu_ref[...], preferred_element_type=F32)
    o_ref[:, :rem] = ((g * _sigmoid(g)) * u).astype(o_ref.dtype)
    o_ref[:, rem:] = jnp.zeros((o_ref.shape[0], o_ref.shape[1] - rem), o_ref.dtype)


def _ffn_up(h, wg, wg_lead, wu, wu_lead, fp, *, tm, tf, cast=None):
    M, D = h.shape
    d_ff = wg.shape[-1]
    nfull = d_ff // tf
    rem = d_ff - nfull * tf
    split = rem > 0 and rem % LANES == 0 and (nfull * tf) % rem == 0 and fp == (nfull + 1) * tf
    last = (d_ff - 1) // tf
    def wspec(lead, width, col):
        return pl.BlockSpec((None,) * len(lead) + (D, width), lambda i, *j: lead + (0, col(*j)))

    grid = (M // tm, nfull if split else fp // tf)
    in_specs = [pl.BlockSpec((tm, D), lambda i, j: (i, 0)),
                wspec(wg_lead, tf, lambda j: jnp.minimum(j, last)), wspec(wu_lead, tf, lambda j: jnp.minimum(j, last))]
    args = [h, wg, wu]
    out_specs = [pl.BlockSpec((tm, tf), lambda i, j: (i, j))]
    out_shape = [jax.ShapeDtypeStruct((M, fp), BF)]
    if cast is not None:
        job = _Cast(cast[0], cast[1], grid[0] * grid[1])
        cin, cout = job.specs(lambda i, j: i * grid[1] + j)
        in_specs.append(cin)
        args.append(job.w)
        out_specs.append(cout)
        out_shape.append(jax.ShapeDtypeStruct(job.shape, BF))
    res = pl.pallas_call(
        functools.partial(_ffn_up_kernel, d_ff=d_ff, cast=job.n_active if cast is not None else 0),
        grid=grid,
        in_specs=in_specs,
        out_specs=out_specs,
        out_shape=out_shape,
        compiler_params=_cparams(*(("arbitrary", "arbitrary") if cast is not None else ("parallel", "parallel"))),
        name="ffn_up",
    )(*args)
    a = res[0]
    if split:
        col = (nfull * tf) // rem
        a = pl.pallas_call(
            functools.partial(_ffn_tail_kernel, rem=rem),
            grid=(M // tm,),
            in_specs=[pl.BlockSpec((tm, D), lambda i: (i, 0)), wspec(wg_lead, rem, lambda: col),
                      wspec(wu_lead, rem, lambda: col), pl.BlockSpec(memory_space=pl.ANY)],
            out_specs=pl.BlockSpec((tm, tf), lambda i: (i, nfull)),
            out_shape=jax.ShapeDtypeStruct((M, fp), BF),
            input_output_aliases={3: 0},
            compiler_params=_cparams("parallel"),
            name="ffn_up_tail",
        )(h, wg, wu, a)
    return (a, res[1]) if cast is not None else a


class _FfnWeights:
    def __init__(self, wg, wu, wd):
        self.src = {'g': wg, 'u': wu, 'd': wd}
        self.ready = {}
        L = wg.shape[0]
        self.order = [(l, s) for l in range(L) for s in range(2)]

    def get(self, kind, ls):
        if (kind, ls) not in self.ready:
            self.ready[(kind, ls)] = (self.src[kind][ls].astype(BF), ())
        return self.ready[(kind, ls)]

    def offer(self, after, n_steps, kinds=('g', 'u')):
        todo = [(k, ls) for ls in self.order[self.order.index(after) + 1:] for k in kinds
                if (k, ls) not in self.ready]
        if not todo:
            return None
        kind = todo[0][0]
        jobs = [ls for k, ls in todo if k == kind][:_Cast.capacity(self.src[kind], n_steps)]
        return (kind, jobs) if jobs else None

    def request(self, offer):
        return None if offer is None else (self.src[offer[0]], offer[1])

    def deliver(self, offer, stacked):
        if offer is not None:
            for n, ls in enumerate(offer[1]):
                self.ready[(offer[0], ls)] = (stacked, (n,))


def _ffn(parts, normt, modt, tok, layer, sub, fw):
    ls = (layer, sub // 2)
    h = _prenorm(parts, normt, modt, tok, layer, sub)
    d_ff, D = fw.src['d'].shape[-2:]
    fp = -(-d_ff // FF_ALIGN) * FF_ALIGN
    tm = tok.tile(1024)
    (wg, wg_lead), (wu, wu_lead) = fw.get('g', ls), fw.get('u', ls)
    a, wd = _ffn_up(h, wg, wg_lead, wu, wu_lead, fp, tm=tm, tf=_pick(fp, 512, LANES), cast=(fw.src['d'], [ls]))
    tk = _pick(fp, max(fp // 4, LANES), LANES)
    tn = _pick(D, 1024, LANES)
    offers = []
    for x, _ in parts:
        offers.append(fw.offer(ls, (x.shape[0] // tm) * (D // tn) * (fp // tk)))
        fw.deliver(offers[-1], None)
    out, copies = _mm_residual(a, wd, (0,), parts, modt, tok, layer, 3 * sub + 2, 0.5, tm=tm, tn=tn, tk=tk,
                               casts=[fw.request(o) for o in offers], name="ffn_down")
    for o, c in zip(offers, copies):
        fw.deliver(o, c)
    return out


def _gelu_tanh(x):
    return x * (0.5 * (1.0 + jnp.tanh(math.sqrt(2.0 / math.pi) * (x + 0.044715 * (x * x * x)))))


def _softplus(x):
    return jnp.maximum(x, 0.0) + jnp.log1p(jnp.exp(-jnp.abs(x)))


def _rglru_kernel(*refs, S):
    xa_ref, ga_ref, cw_ref, cb_ref, w_ref, b_ref, lam_ref, h0_ref = refs[:8]
    y_ref, hfin_ref, pad_ref, af_ref, bf_ref, ab_ref, bb_ref = refs[-7:]
    C = LANES
    W = xa_ref.shape[1]
    P0 = SUBLANES
    pad_ref[0:P0, :] = jnp.zeros((P0, W), F32)
    pad_ref[P0 + S:P0 + S + P0, :] = jnp.zeros((P0, W), F32)
    pad_ref[P0:P0 + S, :] = xa_ref[...]
    cw = cw_ref[...]
    xc = jnp.broadcast_to(cb_ref[...], (S, W))
    for j in range(CONV_W):
        xc = xc + pad_ref[pl.ds(P0 - 2 + j, S), :] * cw[j:j + 1, :]

    lam = lam_ref[...]
    for blk in range(W // C):
        cols = slice(blk * C, (blk + 1) * C)
        xb = xc[:, cols]
        gates = jnp.dot(xb.astype(BF), w_ref[blk], preferred_element_type=F32) + b_ref[blk]
        for d, (a_ref, b_ref_) in enumerate(((af_ref, bf_ref), (ab_ref, bb_ref))):
            r = _sigmoid(gates[:, (2 * d) * C:(2 * d + 1) * C])
            i = _sigmoid(gates[:, (2 * d + 1) * C:(2 * d + 2) * C])
            log_a = (-LRU_C * r) * _softplus(-lam[d:d + 1, cols])
            a_ref[:, cols] = jnp.exp(log_a)
            th = jnp.tanh(log_a)
            one_minus_a2 = (-2.0 * th) / (1.0 - th)
            mult = jnp.where(one_minus_a2 > 0.0, one_minus_a2 * lax.rsqrt(one_minus_a2), 0.0)
            b_ref_[:, cols] = (mult * i) * xb

    rows = lax.broadcasted_iota(jnp.int32, (SUBLANES, W), 0)

    def tile_scan(a, b, down):
        for k in (1, 2, 4):
            if down:
                keep = rows >= k
                shift = k
            else:
                keep = rows < SUBLANES - k
                shift = SUBLANES - k
            a1 = jnp.where(keep, pltpu.roll(a, shift, 0), 1.0)
            b1 = jnp.where(keep, pltpu.roll(b, shift, 0), 0.0)
            b = a * b1 + b
            a = a * a1
        return a, b

    nt = S // SUBLANES
    unroll = max(SCAN_UNROLL * C // W, 1)
    unroll = unroll if nt % unroll == 0 else 1

    def body(tu, carry):
        hf, hb = carry
        tiles = []
        for u in range(unroll):
            t = tu * unroll + u
            r0 = pl.multiple_of(t * SUBLANES, SUBLANES)
            r1 = pl.multiple_of((nt - 1 - t) * SUBLANES, SUBLANES)
            fwd = tile_scan(af_ref[pl.ds(r0, SUBLANES), :], bf_ref[pl.ds(r0, SUBLANES), :], True)
            bwd = tile_scan(ab_ref[pl.ds(r1, SUBLANES), :], bb_ref[pl.ds(r1, SUBLANES), :], False)
            tiles.append((r0, fwd, r1, bwd))
        for r0, (a, b), r1, (a2, b2) in tiles:
            h = a * hf + b
            bf_ref[pl.ds(r0, SUBLANES), :] = h
            hf = jnp.broadcast_to(h[SUBLANES - 1:SUBLANES, :], (SUBLANES, W))
            g = a2 * hb + b2
            bb_ref[pl.ds(r1, SUBLANES), :] = g
            hb = jnp.broadcast_to(g[0:1, :], (SUBLANES, W))
        return hf, hb

    h0 = h0_ref[...]
    hf, hb = lax.fori_loop(0, nt // unroll, body, (jnp.broadcast_to(h0[0:1, :], (SUBLANES, W)),
                                                    jnp.broadcast_to(h0[1:2, :], (SUBLANES, W))))
    hfin_ref[0:1, :] = hf[0:1, :]
    hfin_ref[1:2, :] = hb[0:1, :]
    y_ref[...] = (_gelu_tanh(ga_ref[...]) * (bf_ref[...] + bb_ref[...])).astype(y_ref.dtype)


def _rglru(z, row0, nb, S, d_rnn, conv_w, conv_b, wcat, bcat, lam, h0, out_buf, out_shape):
    C = LANES
    nblk = math.gcd(d_rnn // C, RGLRU_BLOCKS)
    W = C * nblk
    ncb = d_rnn // W
    roff = row0 // S
    in_specs = [
        pl.BlockSpec((S, W), lambda b, c: (roff + b, c)),
        pl.BlockSpec((S, W), lambda b, c: (roff + b, ncb + c)),
        pl.BlockSpec((CONV_W, W), lambda b, c: (0, c)),
        pl.BlockSpec((1, W), lambda b, c: (0, c)),
        pl.BlockSpec((nblk, C, 4 * C), lambda b, c: (c, 0, 0)),
        pl.BlockSpec((nblk, 1, 4 * C), lambda b, c: (c, 0, 0)),
        pl.BlockSpec((2, W), lambda b, c: (0, c)),
        pl.BlockSpec((None, 2, W), lambda b, c: (b, 0, c)),
    ]
    args = [z, z, conv_w, conv_b.reshape(1, d_rnn), wcat, bcat, lam, h0]
    aliases = _in_place(in_specs, args, out_buf)
    return pl.pallas_call(
        functools.partial(_rglru_kernel, S=S),
        grid=(nb, ncb),
        in_specs=in_specs,
        out_specs=[
            pl.BlockSpec((S, W), lambda b, c: (roff + b, c)),
            pl.BlockSpec((None, 2, W), lambda b, c: (b, 0, c)),
        ],
        out_shape=[
            jax.ShapeDtypeStruct(out_shape, BF),
            jax.ShapeDtypeStruct((nb, 2, d_rnn), F32),
        ],
        input_output_aliases=aliases,
        scratch_shapes=[pltpu.VMEM((S + 2 * SUBLANES, W), F32)] + [pltpu.VMEM((S, W), F32)] * 4,
        compiler_params=_cparams("parallel", "parallel"),
        name="rglru",
    )(*args)


def _rope_tables(n_tok):
    t = jnp.arange(n_tok, dtype=jnp.int32)
    row, col = t // GRID_W, t % GRID_W
    n = 16
    inv = ROPE_THETA ** (-jnp.arange(n, dtype=F32) / n)
    ang_r = row.astype(F32)[:, None] * inv
    ang_c = col.astype(F32)[:, None] * inv

    def grp(ang):
        c, s = jnp.cos(ang), jnp.sin(ang)
        return jnp.concatenate([c, c], -1), jnp.concatenate([-s, s], -1)
    cr, sr = grp(ang_r)
    cc, sc = grp(ang_c)
    return jnp.concatenate([cr, cc, cr, cc], -1), jnp.concatenate([sr, sc, sr, sc], -1)


def _rope_kernel(x_ref, cos_ref, sin_ref, o_ref):
    cos = cos_ref[...]
    sin = sin_ref[...]
    first = (lax.broadcasted_iota(jnp.int32, cos.shape, 1) % 32) < 16
    for c in range(x_ref.shape[1] // LANES):
        x = x_ref[:, c * LANES:(c + 1) * LANES].astype(F32)
        partner = jnp.where(first, pltpu.roll(x, LANES - 16, 1), pltpu.roll(x, 16, 1))
        o_ref[:, c * LANES:(c + 1) * LANES] = (x * cos + partner * sin).astype(o_ref.dtype)


def _rope(x, cos, sin, row0, rows, col0, width, ds):
    tm = _pick(ds, 256, SUBLANES)
    tw = _pick(width, 1024, LANES)
    roff, coff, per = row0 // tm, col0 // tw, ds // tm
    return pl.pallas_call(
        _rope_kernel,
        grid=(rows // tm, width // tw),
        in_specs=[
            pl.BlockSpec((tm, tw), lambda i, j: (roff + i, coff + j)),
            pl.BlockSpec((tm, LANES), lambda i, j: (i % per, 0)),
            pl.BlockSpec((tm, LANES), lambda i, j: (i % per, 0)),
        ],
        out_specs=pl.BlockSpec((tm, tw), lambda i, j: (i, j)),
        out_shape=jax.ShapeDtypeStruct((rows, width), BF),
        compiler_params=_cparams("parallel", "parallel"),
        name="rope",
    )(x, cos, sin)


def _dattn_kernel(*refs, P, S, tq, G, lam_init):
    if P:
        q_ref, kc_ref, vc_ref, kl_ref, vl_ref, lq_ref, lk_ref, g_ref = refs[:8]
    else:
        q_ref, kl_ref, vl_ref, lq_ref, lk_ref, g_ref = refs[:6]
    o_ref, kk, vv = refs[-3:]
    C = 2 * DK_B

    @pl.when(pl.program_id(2) == 0)
    def _():
        if P:
            kk[0:P, :] = kc_ref[...].astype(BF)
            vv[0:P, :] = vc_ref[...].astype(BF)
        kk[P:P + S, :] = kl_ref[...].astype(BF)
        vv[P:P + S, :] = vl_ref[...].astype(BF)

    el = jnp.exp(jnp.sum(lq_ref[...] * lk_ref[...], axis=-1, keepdims=True))
    lam = el[0:1, :] - el[1:2, :] + lam_init
    sub = min(DATT_ROWS, tq)
    chains = []
    for g in range(G):
        cols = slice(g * C, (g + 1) * C)
        for r0 in range(0, tq, sub):
            q = q_ref[r0:r0 + sub, cols].astype(F32)
            lane = lax.broadcasted_iota(jnp.int32, q.shape, 1)
            qq = jnp.concatenate([jnp.where(lane < DK_B, q, 0.0), jnp.where(lane >= DK_B, q, 0.0)],
                                 axis=0).astype(BF)
            s = lax.dot_general(qq, kk[:, cols], (((1,), (1,)), ((), ())), preferred_element_type=F32)
            chains.append((cols, r0, s))
    for cols, r0, s in chains:
        e = jnp.exp2((s - jnp.max(s, axis=-1, keepdims=True)) * ((DK_B ** -0.5) * LOG2E))
        l = jnp.sum(e, axis=-1, keepdims=True)
        a = e[:sub, :] - e[sub:, :] * (lam * l[:sub, :] / l[sub:, :])
        o = jnp.dot(a.astype(BF), vv[:, cols], preferred_element_type=F32) / l[:sub, :]
        o_ref[r0:r0 + sub, cols] = ((_rms(o) * g_ref[...]) * (1.0 - lam_init)).astype(o_ref.dtype)


def _dattn(q, qrow0, qcol0, k_lat, krow0, kcol0, v_lat, vrow0, vcol0, cache, nb, S, lq, lk, g, lam_init,
           out_buf, orow0, ocol0):
    C = 2 * DK_B
    tq = _pick(S, 512, SUBLANES)
    nq = S // tq
    G = math.gcd(H_B, DATT_HEADS[0] if nq == 1 else DATT_HEADS[1])
    W = C * G
    P = cache[0].shape[1] if cache is not None else 0
    qr, kr, vr, orr = qrow0 // tq, krow0 // S, vrow0 // S, orow0 // tq
    qc, kc, vc, oc = qcol0 // W, kcol0 // W, vcol0 // W, ocol0 // W
    in_specs = [pl.BlockSpec((tq, W), lambda b, h, i: (qr + b * nq + i, qc + h))]
    args = [q]
    if P:
        in_specs += [pl.BlockSpec((None, P, W), lambda b, h, i: (b, 0, h)),
                     pl.BlockSpec((None, P, W), lambda b, h, i: (b, 0, h))]
        args += [cache[0], cache[1]]
    in_specs += [
        pl.BlockSpec((S, W), lambda b, h, i: (kr + b, kc + h)),
        pl.BlockSpec((S, W), lambda b, h, i: (vr + b, vc + h)),
        pl.BlockSpec((2, DK_B), lambda b, h, i: (0, 0)),
        pl.BlockSpec((2, DK_B), lambda b, h, i: (0, 0)),
        pl.BlockSpec((1, C), lambda b, h, i: (0, 0)),
    ]
    args += [k_lat, v_lat, lq, lk, g.reshape(1, C)]
    aliases = _in_place(in_specs, args, out_buf)
    return pl.pallas_call(
        functools.partial(_dattn_kernel, P=P, S=S, tq=tq, G=G, lam_init=lam_init),
        grid=(nb, H_B // G, nq),
        in_specs=in_specs,
        out_specs=pl.BlockSpec((tq, W), lambda b, h, i: (orr + b * nq + i, oc + h)),
        out_shape=jax.ShapeDtypeStruct(out_buf.shape, BF),
        input_output_aliases=aliases,
        scratch_shapes=[pltpu.VMEM((P + S, W), BF), pltpu.VMEM((P + S, W), BF)],
        compiler_params=_cparams("parallel", "parallel", "arbitrary"),
        name="diff_attn",
    )(*args)


def _mla_in_kernel(h_ref, w_ref, gq_ref, gkv_ref, cq_ref, ckv_ref, kr_ref, *, ql, kvl):
    z = jnp.dot(h_ref[...], w_ref[...], preferred_element_type=F32)
    cq_ref[...] = (_rms(z[:, :ql]) * gq_ref[...]).astype(cq_ref.dtype)
    ckv_ref[...] = _rms(z[:, ql:ql + kvl]) * gkv_ref[...]
    kr_ref[...] = z[:, ql + kvl:]


def _mla_in(h, w, gq, gkv, ql, kvl, tm):
    M, D = h.shape
    N = w.shape[1]
    return pl.pallas_call(
        functools.partial(_mla_in_kernel, ql=ql, kvl=kvl),
        grid=(M // tm,),
        in_specs=[
            pl.BlockSpec((tm, D), lambda i: (i, 0)),
            pl.BlockSpec((D, N), lambda i: (0, 0)),
            pl.BlockSpec((1, ql), lambda i: (0, 0)),
            pl.BlockSpec((1, kvl), lambda i: (0, 0)),
        ],
        out_specs=[
            pl.BlockSpec((tm, ql), lambda i: (i, 0)),
            pl.BlockSpec((tm, kvl), lambda i: (i, 0)),
            pl.BlockSpec((tm, LANES), lambda i: (i, 0)),
        ],
        out_shape=[
            jax.ShapeDtypeStruct((M, ql), BF),
            jax.ShapeDtypeStruct((M, kvl), F32),
            jax.ShapeDtypeStruct((M, LANES), F32),
        ],
        compiler_params=_cparams("parallel"),
        name="mla_in",
    )(h, w, gq.reshape(1, ql), gkv.reshape(1, kvl))


def _mla_kernel(*refs, P, S, tq, G):
    if P:
        qn_ref, qr_ref, kvc_ref, krc_ref, kvl_ref, krl_ref = refs[:6]
        segs = ((0, P, kvc_ref, krc_ref), (P, S, kvl_ref, krl_ref))
    else:
        qn_ref, qr_ref, kvl_ref, krl_ref = refs[:4]
        segs = ((0, S, kvl_ref, krl_ref),)
    o_ref, ks, vs = refs[-3:]
    C = LANES

    @pl.when(pl.program_id(2) == 0)
    def _():
        for r0, n, kv_ref, kr_ref in segs:
            kr = kr_ref[...].astype(F32)
            kr2 = (kr + pltpu.roll(kr, ROPE_C, 1)).astype(BF)
            for hd in range(2 * G):
                ks[hd, r0:r0 + n, 0:C] = kv_ref[:, 2 * hd * C:(2 * hd + 1) * C]
                ks[hd, r0:r0 + n, C:2 * C] = kr2
                vs[hd, r0:r0 + n, 0:C] = kv_ref[:, (2 * hd + 1) * C:(2 * hd + 2) * C]
                vs[hd, r0:r0 + n, C:2 * C] = jnp.ones((n, C), BF)

    c = ((NOPE_C + ROPE_C) ** -0.5) * LOG2E
    chains = []
    sub = min(ATT_ROWS, tq)
    for r0 in range(0, tq, sub):
        rows = slice(r0, r0 + sub)
        for p in range(G):
            qr = qr_ref[rows, p * C:(p + 1) * C].astype(F32)
            lane = lax.broadcasted_iota(jnp.int32, qr.shape, 1)
            for hh in range(2):
                hd = 2 * p + hh
                keep = (lane < ROPE_C) if hh == 0 else (lane >= ROPE_C)
                qf = jnp.concatenate([qn_ref[rows, hd * C:(hd + 1) * C], jnp.where(keep, qr, 0.0).astype(BF)],
                                     axis=1)
                s = lax.dot_general(qf, ks[hd], (((1,), (1,)), ((), ())), preferred_element_type=F32)
                chains.append((rows, hd, s))
    for rows, hd, s in chains:
        e = jnp.exp2((s - jnp.max(s, axis=-1, keepdims=True)) * c).astype(BF)
        oe = jnp.dot(e, vs[hd], preferred_element_type=F32)
        o_ref[rows, hd * C:(hd + 1) * C] = (oe[:, 0:C] / oe[:, C:2 * C]).astype(o_ref.dtype)


def _mla_attn(qn, qr, qr_row0, kv_lat, kr_lat, kr_row0, row0, cache, nb, S, out_buf, out_shape):
    C = LANES
    tq = _pick(S, 512, SUBLANES)
    nq = S // tq
    G = math.gcd(H_C // 2, MLA_PAIRS[0] if nq == 1 else MLA_PAIRS[1])
    P = cache[0].shape[0] // nb if cache is not None else 0
    r_q, r_qr, r_kv, r_kr = row0 // tq, qr_row0 // tq, row0 // S, kr_row0 // S
    in_specs = [pl.BlockSpec((tq, 2 * C * G), lambda b, h, i: (r_q + b * nq + i, h)),
                pl.BlockSpec((tq, C * G), lambda b, h, i: (r_qr + b * nq + i, h))]
    args = [qn, qr]
    if P:
        in_specs += [pl.BlockSpec((P, 4 * C * G), lambda b, h, i: (b, h)),
                     pl.BlockSpec((P, C), lambda b, h, i: (b, 0))]
        args += [cache[0], cache[1]]
    in_specs += [pl.BlockSpec((S, 4 * C * G), lambda b, h, i: (r_kv + b, h)),
                 pl.BlockSpec((S, C), lambda b, h, i: (r_kr + b, 0))]
    args += [kv_lat, kr_lat]
    aliases = _in_place(in_specs, args, out_buf)
    return pl.pallas_call(
        functools.partial(_mla_kernel, P=P, S=S, tq=tq, G=G),
        grid=(nb, H_C // (2 * G), nq),
        in_specs=in_specs,
        out_specs=pl.BlockSpec((tq, 2 * C * G), lambda b, h, i: (r_q + b * nq + i, h)),
        out_shape=jax.ShapeDtypeStruct(out_shape, BF),
        input_output_aliases=aliases,
        scratch_shapes=[pltpu.VMEM((2 * G, P + S, 2 * C), BF)] * 2,
        compiler_params=_cparams("parallel", "parallel", "arbitrary"),
        name="mla_attn",
    )(*args)


def kernel(x_prompt, x_sample, c, c_ctx, state_rglru, cache_dk, cache_dv, cache_ckv, cache_krope,
           mod_w, mod_b, norm_g, final_g, ffn_wg, ffn_wu, ffn_wd,
           ev_w_in, ev_conv_w, ev_conv_b, ev_wa, ev_ba, ev_wx, ev_bx, ev_lam, ev_lq, ev_lk,
           ev_subln_g, ev_w_out, od_w_in, od_qnorm_g, od_w_uq, od_kvnorm_g, od_w_ukv, od_w_out):
    assert NOPE_C == LANES and V_C == LANES and 2 * DK_B == LANES and 2 * ROPE_C == LANES
    NBP, SP, D = x_prompt.shape
    NB, DS, _ = x_sample.shape
    PAST = cache_dk.shape[2]
    L = mod_w.shape[0]
    MP, MS = NBP * SP, NB * DS
    tok = _Tokens(MP, DS, NB, 1 + NB)
    M = tok.m
    d_rnn = ev_lam.shape[-1]
    d_ff = ffn_wg.shape[-1]
    ql, kvl = od_qnorm_g.shape[-1], od_kvnorm_g.shape[-1]
    assert d_rnn // H_A == LANES

    n_cond = -(-(1 + NB) // SUBLANES) * SUBLANES
    cond = jnp.concatenate([c_ctx[None, :], c, jnp.zeros((n_cond - 1 - NB, D), F32)], axis=0)
    mod = _modulation(cond, mod_w, mod_b)
    modt = mod[:, :1 + NB].reshape(L * (1 + NB) * N_MOD, 1, D)
    normt = norm_g.reshape(L * 3, 1, D)

    fw = _FfnWeights(ffn_wg, ffn_wu, ffn_wd)

    cos, sin = _rope_tables(DS)
    tm = tok.tile(1024)
    new = {}
    parts = [(x_prompt.reshape(MP, D), 0), (x_sample.reshape(MS, D), MP)]

    for l in range(L):
        x = _ffn(parts, normt, modt, tok, l, 0, fw)
        h = _prenorm([(x, 0)], normt, modt, tok, l, 1)
        if l % 2 == 0:
            e = l // 2
            lam_init = 0.8 - 0.6 * math.exp(-0.3 * l)
            nh = H_B * LANES
            w_in = ev_w_in[e].astype(BF)
            w_qk = w_in[:, 2 * d_rnn:2 * d_rnn + 2 * nh]
            xg = _mm(h, w_in[:, :2 * d_rnn], F32, tm=tm, name="ev_in_rec")
            v = _mm(h, w_in[:, 2 * d_rnn + 2 * nh:], F32, tm=tm, name="ev_in_v")
            qk_p = _mm(h, w_qk, F32, tm=tm, rows=MP, name="ev_in_qk")
            qk_s = _mm(h, w_qk, BF, tm=tm, row0=MP, rows=MS, rope=(cos, sin, DS), name="ev_in_qk_rope")
            wcat = jnp.concatenate([ev_wa[e, 0], ev_wx[e, 0], ev_wa[e, 1], ev_wx[e, 1]], axis=-1).astype(BF)
            bcat = jnp.concatenate([t.reshape(H_A, 1, LANES) for t in
                                    (ev_ba[e, 0], ev_bx[e, 0], ev_ba[e, 1], ev_bx[e, 1])], axis=-1)
            mix_shape = (M, d_rnn + nh)
            rg = functools.partial(_rglru, xg, d_rnn=d_rnn, conv_w=ev_conv_w[e], conv_b=ev_conv_b[e],
                                   wcat=wcat, bcat=bcat, lam=ev_lam[e], out_shape=mix_shape)
            y_in, hfin = rg(row0=0, nb=NBP, S=SP, h0=jnp.zeros((NBP, 2, d_rnn), F32), out_buf=None)
            y_in, _ = rg(row0=MP, nb=NB, S=DS, h0=state_rglru[:, e], out_buf=y_in)
            y_in = _dattn(qk_p, 0, 0, qk_p, 0, nh, v, 0, 0, None, NBP, SP,
                          ev_lq[e], ev_lk[e], ev_subln_g[e], lam_init, y_in, 0, d_rnn)
            cache = (cache_dk[:, e].reshape(NB, PAST, nh), cache_dv[:, e].reshape(NB, PAST, nh))
            y_in = _dattn(qk_s, 0, 0, qk_s, 0, nh, v, MP, 0, cache, NB, DS,
                          ev_lq[e], ev_lk[e], ev_subln_g[e], lam_init, y_in, MP, d_rnn)
            w_out = ev_w_out[e].astype(BF)
            new.setdefault('rec', []).append(hfin)
            new.setdefault('dk', []).append(qk_p[:, nh:].reshape(NBP, SP, H_B, 2 * DK_B))
            new.setdefault('dv', []).append(v[:MP].reshape(NBP, SP, H_B, 2 * DK_B))
        else:
            o = l // 2
            w_in = jnp.pad(od_w_in[o].astype(BF), ((0, 0), (0, LANES - ROPE_C)))
            cqn, ckvn, kr = _mla_in(h, w_in, od_qnorm_g[o], od_kvnorm_g[o], ql, kvl, tok.tile(512))
            w_uq = od_w_uq[o].astype(BF).reshape(ql, H_C, NOPE_C + ROPE_C)
            tm2, tn2 = tok.tile(2048), _pick(H_C * NOPE_C, 2048, LANES)
            qn = _mm(cqn, w_uq[:, :, :NOPE_C].reshape(ql, H_C * NOPE_C), BF, tm=tm2, tn=tn2, name="uq_nope")
            w_uq_r = w_uq[:, :, NOPE_C:].reshape(ql, H_C * ROPE_C)
            qr_p = _mm(cqn, w_uq_r, BF, tm=tm, rows=MP, name="uq_rope")
            qr_s = _mm(cqn, w_uq_r, BF, tm=tm, row0=MP, rows=MS, rope=(cos, sin, DS), name="uq_rope_rot")
            w_ukv = od_w_ukv[o].astype(BF)
            kv = _mm(ckvn, w_ukv, BF, tm=tm2, tn=tn2, name="ukv")
            mix_shape = (M, H_C * V_C)
            y_in = _mla_attn(qn, qr_p, 0, kv, kr, 0, 0, None, NBP, SP, None, mix_shape)
            kr_s = _rope(kr, cos, sin, MP, MS, 0, LANES, DS)
            ckv_c = cache_ckv[:, o].reshape(NB * PAST, kvl)
            kv_c = _mm(ckv_c, w_ukv, BF, tm=_pick(NB * PAST, 1024, SUBLANES), name="ukv_ctx")
            kr_c = jnp.pad(cache_krope[:, o].reshape(NB * PAST, ROPE_C), ((0, 0), (0, LANES - ROPE_C)))
            y_in = _mla_attn(qn, qr_s, 0, kv, kr_s, 0, MP, (kv_c, kr_c), NB, DS, y_in, mix_shape)
            w_out = od_w_out[o].astype(BF)
            new.setdefault('ckv', []).append(ckvn[:MP].reshape(NBP, SP, kvl))
            new.setdefault('kr', []).append(kr[:MP, :ROPE_C].reshape(NBP, SP, ROPE_C))
        x = _mm_residual(y_in, w_out, (), [(x, 0)], modt, tok, l, 5, 1.0, tm=tm, tn=_pick(D, 1024, LANES),
                         name="mix_out")
        x = _ffn([(x, 0)], normt, modt, tok, l, 2, fw)
        parts = [(x, 0)]

    y_prompt = _final_norm(x, final_g, 0, MP).reshape(NBP, SP, D)
    y_sample = _final_norm(x, final_g, MP, MS).reshape(NB, DS, D)
    return (y_prompt, y_sample, jnp.stack(new['rec'], axis=1), jnp.stack(new['dk'], axis=1),
            jnp.stack(new['dv'], axis=1), jnp.stack(new['ckv'], axis=1), jnp.stack(new['kr'], axis=1))
```

```python
import functools
import math

import jax
import jax.numpy as jnp
from jax import lax
from jax.experimental import pallas as pl
from jax.experimental.pallas import tpu as pltpu

BF = jnp.bfloat16
F32 = jnp.float32

GRID_W = 64
EPS = 1e-6
ROPE_THETA = 10000.0
N_MOD = 9
H_A = 16
CONV_W = 4
LRU_C = 8.0
H_B = 16
DK_B = 64
H_C = 32
NOPE_C = 128
ROPE_C = 64
V_C = 128

LANES = 128
SUBLANES = 8
VMEM_LIMIT = 56 * 1024 * 1024
FF_ALIGN = 1024
LOG2E = math.log2(math.e)
ATT_ROWS = 256
DATT_ROWS = 256
DATT_HEADS = (4, 2)
MLA_PAIRS = (8, 1)
SCAN_UNROLL = 4
NORM_ROWS = 512
RGLRU_BLOCKS = 2
CAST_BYTES = 2 * 1024 * 1024


def _cparams(*sem):
    return pltpu.CompilerParams(dimension_semantics=sem, vmem_limit_bytes=VMEM_LIMIT)


def _pick(n, pref, mult):
    best = None
    d = mult
    while d <= min(n, pref):
        if n % d == 0:
            best = d
        d += mult
    return n if best is None else best


def _sigmoid(x):
    return 1.0 / (1.0 + jnp.exp2(x * (-LOG2E)))


def _rms(x):
    return x * lax.rsqrt(jnp.mean(x * x, axis=-1, keepdims=True) + EPS)


def _in_place(in_specs, args, out_buf):
    if out_buf is None:
        return {}
    in_specs.append(pl.BlockSpec(memory_space=pl.ANY))
    args.append(out_buf)
    return {len(args) - 1: 0}


class _Cast:
    def __init__(self, w, jobs, n_steps):
        R, C = w.shape[-2:]
        tr = 2 * SUBLANES
        ok = [t for t in range(tr, R + 1, tr) if R % t == 0 and (R // t) * len(jobs) <= n_steps]
        big = [t for t in ok if 4 * t * C >= CAST_BYTES]
        self.tr = big[0] if big else ok[-1]
        self.w, self.jobs, self.nb, self.shape = w, jobs, R // self.tr, (len(jobs), R, C)
        self.n_active = self.nb * len(jobs)

    @staticmethod
    def capacity(w, n_steps):
        R, C = w.shape[-2:]
        return min(2, n_steps // max(1, (4 * R * C) // CAST_BYTES))

    def specs(self, lin):
        nb, jobs, tr, C = self.nb, self.jobs, self.tr, self.shape[2]

        def split(*g):
            b = jnp.minimum(lin(*g), nb * len(jobs) - 1)
            return b // nb, b % nb

        def src(*g):
            j, r = split(*g)
            l, sl = jobs[0]
            for n, (l2, s2) in enumerate(jobs[1:], 1):
                l, sl = jnp.where(j == n, l2, l), jnp.where(j == n, s2, sl)
            return (l, sl, r, 0)
        return (pl.BlockSpec((None, None, tr, C), src),
                pl.BlockSpec((None, tr, C), lambda *g: split(*g) + (0,)))


def _mod_kernel(c_ref, w_ref, b_ref, o_ref):
    k = pl.program_id(2)
    c = c_ref[...]
    s = (c * _sigmoid(c)).astype(BF)
    part = jnp.dot(s, w_ref[...].astype(BF), preferred_element_type=F32)

    @pl.when(k == 0)
    def _():
        o_ref[...] = part + b_ref[...]

    @pl.when(k > 0)
    def _():
        o_ref[...] += part


def _modulation(cond, mod_w, mod_b):
    R, D = cond.shape
    L, _, N = mod_w.shape
    tn = _pick(N, 2048, LANES)
    tk = _pick(D, 1024, LANES)
    return pl.pallas_call(
        _mod_kernel,
        grid=(L, N // tn, D // tk),
        in_specs=[
            pl.BlockSpec((R, tk), lambda l, n, k: (0, k)),
            pl.BlockSpec((None, tk, tn), lambda l, n, k: (l, k, n)),
            pl.BlockSpec((None, 1, tn), lambda l, n, k: (l, 0, n)),
        ],
        out_specs=pl.BlockSpec((None, R, tn), lambda l, n, k: (l, 0, n)),
        out_shape=jax.ShapeDtypeStruct((L, R, N), F32),
        compiler_params=_cparams("parallel", "parallel", "arbitrary"),
        name="modulation",
    )(cond, mod_w, mod_b.reshape(L, 1, N))


class _Tokens:
    def __init__(self, mp, ds, nb, n_rows):
        self.mp, self.ds, self.nb, self.n_rows = mp, ds, nb, n_rows
        self.m = mp + ds * nb

    def tile(self, pref):
        return _pick(math.gcd(self.mp, self.ds), pref, SUBLANES)

    def mod_index(self, layer, chunk, tm):
        mp, ds, n_rows = self.mp, self.ds, self.n_rows

        def f(i):
            r = jnp.where(i * tm < mp, 0, 1 + (i * tm - mp) // ds)
            return (layer * n_rows + r) * N_MOD + chunk
        return f


def _prenorm_kernel(x_ref, g_ref, sh_ref, sc_ref, *rest):
    o_ref = rest[-1]
    y = _rms(x_ref[...]) * g_ref[...]
    o_ref[...] = (y * (1.0 + sc_ref[...]) + sh_ref[...]).astype(o_ref.dtype)


def _prenorm(parts, normt, modt, tok, layer, sub):
    h = None
    for x, row0 in parts:
        rows, D = x.shape
        tm = tok.tile(NORM_ROWS)
        off = row0 // tm
        sh = tok.mod_index(layer, 3 * sub, tm)
        sc = tok.mod_index(layer, 3 * sub + 1, tm)
        in_specs = [
            pl.BlockSpec((tm, D), lambda i: (i, 0)),
            pl.BlockSpec((None, 1, D), lambda i: (layer * 3 + sub, 0, 0)),
            pl.BlockSpec((None, 1, D), lambda i: (sh(off + i), 0, 0)),
            pl.BlockSpec((None, 1, D), lambda i: (sc(off + i), 0, 0)),
        ]
        args = [x, normt, modt, modt]
        aliases = _in_place(in_specs, args, h)
        h = pl.pallas_call(
            _prenorm_kernel,
            grid=(rows // tm,),
            in_specs=in_specs,
            out_specs=pl.BlockSpec((tm, D), lambda i: (off + i, 0)),
            out_shape=jax.ShapeDtypeStruct((tok.m, D), BF),
            input_output_aliases=aliases,
            compiler_params=_cparams("parallel"),
            name="prenorm",
        )(*args)
    return h


def _final_norm_kernel(x_ref, g_ref, o_ref):
    o_ref[...] = _rms(x_ref[...]) * g_ref[...]


def _final_norm(x, g, row0, rows):
    D = x.shape[1]
    tm = _pick(math.gcd(row0, rows) if row0 else rows, NORM_ROWS, SUBLANES)
    off = row0 // tm
    return pl.pallas_call(
        _final_norm_kernel,
        grid=(rows // tm,),
        in_specs=[
            pl.BlockSpec((tm, D), lambda i: (off + i, 0)),
            pl.BlockSpec((1, D), lambda i: (0, 0)),
        ],
        out_specs=pl.BlockSpec((tm, D), lambda i: (i, 0)),
        out_shape=jax.ShapeDtypeStruct((rows, D), F32),
        compiler_params=_cparams("parallel"),
        name="final_norm",
    )(x, g.reshape(1, D))


def _rope_lanes(x, cos, sin, first):
    partner = jnp.where(first, pltpu.roll(x, LANES - 16, 1), pltpu.roll(x, 16, 1))
    return x * cos + partner * sin


def _mm_kernel(*refs, nk, residual, coef, rope=False, last_rows=None, cast=0):
    a_ref, w_ref = refs[:2]
    if residual:
        r_ref, g_ref = refs[2:4]
    if rope:
        cos_ref, sin_ref = refs[2:4]
    refs, acc_ref = (refs[:-1], refs[-1]) if nk > 1 else (refs, None)
    if cast:
        step = (pl.program_id(0) * pl.num_programs(1) + pl.program_id(1)) * nk + pl.program_id(2)

        @pl.when(step < cast)
        def _():
            refs[-1][...] = refs[-3][...].astype(BF)
        refs = refs[:-1]
    o_ref = refs[-1]

    def finish(acc):
        if residual:
            o_ref[...] = r_ref[...] + (coef * g_ref[...]) * acc
        elif rope:
            cos, sin = cos_ref[...], sin_ref[...]
            first = (lax.broadcasted_iota(jnp.int32, cos.shape, 1) % 32) < 16
            for c in range(acc.shape[1] // LANES):
                cols = slice(c * LANES, (c + 1) * LANES)
                o_ref[:, cols] = _rope_lanes(acc[:, cols], cos, sin, first).astype(o_ref.dtype)
        else:
            o_ref[...] = acc.astype(o_ref.dtype)

    def part(valid_rows=None):
        if valid_rows is None:
            return jnp.dot(a_ref[...].astype(BF), w_ref[...], preferred_element_type=F32)
        if valid_rows % LANES == 0:
            return jnp.dot(a_ref[:, :valid_rows].astype(BF), w_ref[:valid_rows, :], preferred_element_type=F32)
        w = w_ref[...]
        w = jnp.where(lax.broadcasted_iota(jnp.int32, w.shape, 0) < valid_rows, w, jnp.zeros_like(w))
        return jnp.dot(a_ref[...].astype(BF), w, preferred_element_type=F32)

    if nk == 1:
        finish(part(last_rows))
        return
    k = pl.program_id(2)

    @pl.when(k == 0)
    def _():
        acc_ref[...] = part()

    @pl.when(jnp.logical_and(k > 0, k < nk - 1))
    def _():
        acc_ref[...] += part()

    @pl.when(k == nk - 1)
    def _():
        finish(acc_ref[...] + part(last_rows))


def _mm(a, w, out_dtype, *, tm, tn=None, tk=None, row0=0, rows=None, rope=None, cast=None, name="mm"):
    K, N = w.shape
    tn = _pick(N, 1024, LANES) if tn is None else tn
    rows = a.shape[0] if rows is None else rows
    tk = K if tk is None else tk
    nk = K // tk
    off = row0 // tm
    in_specs = [
        pl.BlockSpec((tm, tk), lambda i, j, k: (off + i, k)),
        pl.BlockSpec((tk, tn), lambda i, j, k: (k, j)),
    ]
    args = [a, w]
    if rope is not None:
        cos, sin, ds = rope
        per = ds // tm
        in_specs += [pl.BlockSpec((tm, LANES), lambda i, j, k: (i % per, 0))] * 2
        args += [cos, sin]
    grid = (rows // tm, N // tn, nk)
    out_specs = [pl.BlockSpec((tm, tn), lambda i, j, k: (i, j))]
    out_shape = [jax.ShapeDtypeStruct((rows, N), out_dtype)]
    if cast is not None:
        job = _Cast(cast[0], cast[1], grid[0] * grid[1] * grid[2])
        cin, cout = job.specs(lambda i, j, k: (i * grid[1] + j) * nk + k)
        in_specs.append(cin)
        args.append(job.w)
        out_specs.append(cout)
        out_shape.append(jax.ShapeDtypeStruct(job.shape, BF))
    res = pl.pallas_call(
        functools.partial(_mm_kernel, nk=nk, residual=False, coef=None, rope=rope is not None,
                          cast=job.n_active if cast is not None else 0),
        grid=grid,
        in_specs=in_specs,
        out_specs=out_specs,
        out_shape=out_shape,
        scratch_shapes=[pltpu.VMEM((tm, tn), F32)] if nk > 1 else [],
        compiler_params=_cparams(*(("arbitrary",) * 3 if cast is not None else ("parallel", "parallel", "arbitrary"))),
        name=name,
    )(*args)
    return res if cast is not None else res[0]


def _mm_residual(a, w, w_lead, parts, modt, tok, layer, chunk, coef, *, tm, tn, tk=None, casts=None,
                 name="mm_res"):
    K, N = a.shape[1], w.shape[-1]
    tk = K if tk is None else tk
    nk = K // tk
    last_rows = w.shape[-2] - (nk - 1) * tk if w.shape[-2] < K else None
    assert last_rows is None or 0 < last_rows < tk
    gate = tok.mod_index(layer, chunk, tm)
    out = None
    cast_outs = []
    for n, (x, row0) in enumerate(parts):
        off = row0 // tm
        grid = (x.shape[0] // tm, N // tn, nk)
        in_specs = [
            pl.BlockSpec((tm, tk), lambda i, j, k: (off + i, k)),
            pl.BlockSpec((None,) * len(w_lead) + (tk, tn), lambda i, j, k: w_lead + (k, j)),
            pl.BlockSpec((tm, tn), lambda i, j, k: (i, j)),
            pl.BlockSpec((None, 1, tn), lambda i, j, k: (gate(off + i), 0, j)),
        ]
        args = [a, w, x, modt]
        aliases = _in_place(in_specs, args, out)
        out_specs = [pl.BlockSpec((tm, tn), lambda i, j, k: (off + i, j))]
        out_shape = [jax.ShapeDtypeStruct((tok.m, N), F32)]
        cast = casts[n] if casts else None
        if cast is not None:
            job = _Cast(cast[0], cast[1], grid[0] * grid[1] * grid[2])
            cin, cout = job.specs(lambda i, j, k: (i * grid[1] + j) * nk + k)
            in_specs.append(cin)
            args.append(job.w)
            out_specs.append(cout)
            out_shape.append(jax.ShapeDtypeStruct(job.shape, BF))
        res = pl.pallas_call(
            functools.partial(_mm_kernel, nk=nk, residual=True, coef=coef, last_rows=last_rows,
                              cast=job.n_active if cast is not None else 0),
            grid=grid,
            in_specs=in_specs,
            out_specs=out_specs,
            out_shape=out_shape,
            input_output_aliases=aliases,
            scratch_shapes=[pltpu.VMEM((tm, tn), F32)] if nk > 1 else [],
            compiler_params=_cparams(*(("arbitrary",) * 3 if cast is not None
                                       else ("parallel", "parallel", "arbitrary"))),
            name=name,
        )(*args)
        out = res[0]
        cast_outs.append(res[1] if cast is not None else None)
    return (out, cast_outs) if casts else out


def _ffn_up_kernel(h_ref, wg_ref, wu_ref, *rest, d_ff, cast=0):
    if cast:
        @pl.when(pl.program_id(0) * pl.num_programs(1) + pl.program_id(1) < cast)
        def _():
            rest[2][...] = rest[0][...].astype(BF)
        rest = rest[1:]
    o_ref = rest[0]
    h = h_ref[...]
    g = jnp.dot(h, wg_ref[...], preferred_element_type=F32)
    u = jnp.dot(h, wu_ref[...], preferred_element_type=F32)
    tf = o_ref.shape[1]
    col = pl.program_id(1) * tf + lax.broadcasted_iota(jnp.int32, (1, tf), 1)
    o_ref[...] = jnp.where(col < d_ff, (g * _sigmoid(g)) * u, 0.0).astype(o_ref.dtype)


def _ffn_tail_kernel(h_ref, wg_ref, wu_ref, buf_ref, o_ref, *, rem):
    h = h_ref[...]
    g = jnp.dot(h, wg_ref[...], preferred_element_type=F32)
    u = jnp.dot(h, wu_ref[...], preferred_element_type=F32)
    o_ref[:, :rem] = ((g * _sigmoid(g)) * u).astype(o_ref.dtype)
    o_ref[:, rem:] = jnp.zeros((o_ref.shape[0], o_ref.shape[1] - rem), o_ref.dtype)


def _ffn_up(h, wg, wg_lead, wu, wu_lead, fp, *, tm, tf, cast=None):
    M, D = h.shape
    d_ff = wg.shape[-1]
    nfull = d_ff // tf
    rem = d_ff - nfull * tf
    split = rem > 0 and rem % LANES == 0 and (nfull * tf) % rem == 0 and fp == (nfull + 1) * tf
    last = (d_ff - 1) // tf
    def wspec(lead, width, col):
        return pl.BlockSpec((None,) * len(lead) + (D, width), lambda i, *j: lead + (0, col(*j)))

    grid = (M // tm, nfull if split else fp // tf)
    in_specs = [pl.BlockSpec((tm, D), lambda i, j: (i, 0)),
                wspec(wg_lead, tf, lambda j: jnp.minimum(j, last)), wspec(wu_lead, tf, lambda j: jnp.minimum(j, last))]
    args = [h, wg, wu]
    out_specs = [pl.BlockSpec((tm, tf), lambda i, j: (i, j))]
    out_shape = [jax.ShapeDtypeStruct((M, fp), BF)]
    if cast is not None:
        job = _Cast(cast[0], cast[1], grid[0] * grid[1])
        cin, cout = job.specs(lambda i, j: i * grid[1] + j)
        in_specs.append(cin)
        args.append(job.w)
        out_specs.append(cout)
        out_shape.append(jax.ShapeDtypeStruct(job.shape, BF))
    res = pl.pallas_call(
        functools.partial(_ffn_up_kernel, d_ff=d_ff, cast=job.n_active if cast is not None else 0),
        grid=grid,
        in_specs=in_specs,
        out_specs=out_specs,
        out_shape=out_shape,
        compiler_params=_cparams(*(("arbitrary", "arbitrary") if cast is not None else ("parallel", "parallel"))),
        name="ffn_up",
    )(*args)
    a = res[0]
    if split:
        col = (nfull * tf) // rem
        a = pl.pallas_call(
            functools.partial(_ffn_tail_kernel, rem=rem),
            grid=(M // tm,),
            in_specs=[pl.BlockSpec((tm, D), lambda i: (i, 0)), wspec(wg_lead, rem, lambda: col),
                      wspec(wu_lead, rem, lambda: col), pl.BlockSpec(memory_space=pl.ANY)],
            out_specs=pl.BlockSpec((tm, tf), lambda i: (i, nfull)),
            out_shape=jax.ShapeDtypeStruct((M, fp), BF),
            input_output_aliases={3: 0},
            compiler_params=_cparams("parallel"),
            name="ffn_up_tail",
        )(h, wg, wu, a)
    return (a, res[1]) if cast is not None else a


class _FfnWeights:
    def __init__(self, wg, wu, wd):
        self.src = {'g': wg, 'u': wu, 'd': wd}
        self.ready = {}
        L = wg.shape[0]
        self.order = [(l, s) for l in range(L) for s in range(2)]

    def get(self, kind, ls):
        if (kind, ls) not in self.ready:
            self.ready[(kind, ls)] = (self.src[kind][ls].astype(BF), ())
        return self.ready[(kind, ls)]

    def offer(self, after, n_steps, kinds=('g', 'u')):
        todo = [(k, ls) for ls in self.order[self.order.index(after) + 1:] for k in kinds
                if (k, ls) not in self.ready]
        if not todo:
            return None
        kind = todo[0][0]
        jobs = [ls for k, ls in todo if k == kind][:_Cast.capacity(self.src[kind], n_steps)]
        return (kind, jobs) if jobs else None

    def request(self, offer):
        return None if offer is None else (self.src[offer[0]], offer[1])

    def deliver(self, offer, stacked):
        if offer is not None:
            for n, ls in enumerate(offer[1]):
                self.ready[(offer[0], ls)] = (stacked, (n,))


def _ffn(parts, normt, modt, tok, layer, sub, fw):
    ls = (layer, sub // 2)
    h = _prenorm(parts, normt, modt, tok, layer, sub)
    d_ff, D = fw.src['d'].shape[-2:]
    fp = -(-d_ff // FF_ALIGN) * FF_ALIGN
    tm = tok.tile(1024)
    (wg, wg_lead), (wu, wu_lead) = fw.get('g', ls), fw.get('u', ls)
    a, wd = _ffn_up(h, wg, wg_lead, wu, wu_lead, fp, tm=tm, tf=_pick(fp, 512, LANES), cast=(fw.src['d'], [ls]))
    tk = _pick(fp, max(fp // 4, LANES), LANES)
    tn = _pick(D, 1024, LANES)
    offers = []
    for x, _ in parts:
        offers.append(fw.offer(ls, (x.shape[0] // tm) * (D // tn) * (fp // tk)))
        fw.deliver(offers[-1], None)
    out, copies = _mm_residual(a, wd, (0,), parts, modt, tok, layer, 3 * sub + 2, 0.5, tm=tm, tn=tn, tk=tk,
                               casts=[fw.request(o) for o in offers], name="ffn_down")
    for o, c in zip(offers, copies):
        fw.deliver(o, c)
    return out


def _gelu_tanh(x):
    return x * (0.5 * (1.0 + jnp.tanh(math.sqrt(2.0 / math.pi) * (x + 0.044715 * (x * x * x)))))


def _softplus(x):
    return jnp.maximum(x, 0.0) + jnp.log1p(jnp.exp(-jnp.abs(x)))


def _rglru_kernel(*refs, S):
    xa_ref, ga_ref, cw_ref, cb_ref, w_ref, b_ref, lam_ref, h0_ref = refs[:8]
    y_ref, hfin_ref, pad_ref, af_ref, bf_ref, ab_ref, bb_ref = refs[-7:]
    C = LANES
    W = xa_ref.shape[1]
    P0 = SUBLANES
    pad_ref[0:P0, :] = jnp.zeros((P0, W), F32)
    pad_ref[P0 + S:P0 + S + P0, :] = jnp.zeros((P0, W), F32)
    pad_ref[P0:P0 + S, :] = xa_ref[...]
    cw = cw_ref[...]
    xc = jnp.broadcast_to(cb_ref[...], (S, W))
    for j in range(CONV_W):
        xc = xc + pad_ref[pl.ds(P0 - 2 + j, S), :] * cw[j:j + 1, :]

    lam = lam_ref[...]
    for blk in range(W // C):
        cols = slice(blk * C, (blk + 1) * C)
        xb = xc[:, cols]
        gates = jnp.dot(xb.astype(BF), w_ref[blk], preferred_element_type=F32) + b_ref[blk]
        for d, (a_ref, b_ref_) in enumerate(((af_ref, bf_ref), (ab_ref, bb_ref))):
            r = _sigmoid(gates[:, (2 * d) * C:(2 * d + 1) * C])
            i = _sigmoid(gates[:, (2 * d + 1) * C:(2 * d + 2) * C])
            log_a = (-LRU_C * r) * _softplus(-lam[d:d + 1, cols])
            a_ref[:, cols] = jnp.exp(log_a)
            th = jnp.tanh(log_a)
            one_minus_a2 = (-2.0 * th) / (1.0 - th)
            mult = jnp.where(one_minus_a2 > 0.0, one_minus_a2 * lax.rsqrt(one_minus_a2), 0.0)
            b_ref_[:, cols] = (mult * i) * xb

    rows = lax.broadcasted_iota(jnp.int32, (SUBLANES, W), 0)

    def tile_scan(a, b, down):
        for k in (1, 2, 4):
            if down:
                keep = rows >= k
                shift = k
            else:
                keep = rows < SUBLANES - k
                shift = SUBLANES - k
            a1 = jnp.where(keep, pltpu.roll(a, shift, 0), 1.0)
            b1 = jnp.where(keep, pltpu.roll(b, shift, 0), 0.0)
            b = a * b1 + b
            a = a * a1
        return a, b

    nt = S // SUBLANES
    unroll = max(SCAN_UNROLL * C // W, 1)
    unroll = unroll if nt % unroll == 0 else 1

    def body(tu, carry):
        hf, hb = carry
        tiles = []
        for u in range(unroll):
            t = tu * unroll + u
            r0 = pl.multiple_of(t * SUBLANES, SUBLANES)
            r1 = pl.multiple_of((nt - 1 - t) * SUBLANES, SUBLANES)
            fwd = tile_scan(af_ref[pl.ds(r0, SUBLANES), :], bf_ref[pl.ds(r0, SUBLANES), :], True)
            bwd = tile_scan(ab_ref[pl.ds(r1, SUBLANES), :], bb_ref[pl.ds(r1, SUBLANES), :], False)
            tiles.append((r0, fwd, r1, bwd))
        for r0, (a, b), r1, (a2, b2) in tiles:
            h = a * hf + b
            bf_ref[pl.ds(r0, SUBLANES), :] = h
            hf = jnp.broadcast_to(h[SUBLANES - 1:SUBLANES, :], (SUBLANES, W))
            g = a2 * hb + b2
            bb_ref[pl.ds(r1, SUBLANES), :] = g
            hb = jnp.broadcast_to(g[0:1, :], (SUBLANES, W))
        return hf, hb

    h0 = h0_ref[...]
    hf, hb = lax.fori_loop(0, nt // unroll, body, (jnp.broadcast_to(h0[0:1, :], (SUBLANES, W)),
                                                    jnp.broadcast_to(h0[1:2, :], (SUBLANES, W))))
    hfin_ref[0:1, :] = hf[0:1, :]
    hfin_ref[1:2, :] = hb[0:1, :]
    y_ref[...] = (_gelu_tanh(ga_ref[...]) * (bf_ref[...] + bb_ref[...])).astype(y_ref.dtype)


def _rglru(z, row0, nb, S, d_rnn, conv_w, conv_b, wcat, bcat, lam, h0, out_buf, out_shape):
    C = LANES
    nblk = math.gcd(d_rnn // C, RGLRU_BLOCKS)
    W = C * nblk
    ncb = d_rnn // W
    roff = row0 // S
    in_specs = [
        pl.BlockSpec((S, W), lambda b, c: (roff + b, c)),
        pl.BlockSpec((S, W), lambda b, c: (roff + b, ncb + c)),
        pl.BlockSpec((CONV_W, W), lambda b, c: (0, c)),
        pl.BlockSpec((1, W), lambda b, c: (0, c)),
        pl.BlockSpec((nblk, C, 4 * C), lambda b, c: (c, 0, 0)),
        pl.BlockSpec((nblk, 1, 4 * C), lambda b, c: (c, 0, 0)),
        pl.BlockSpec((2, W), lambda b, c: (0, c)),
        pl.BlockSpec((None, 2, W), lambda b, c: (b, 0, c)),
    ]
    args = [z, z, conv_w, conv_b.reshape(1, d_rnn), wcat, bcat, lam, h0]
    aliases = _in_place(in_specs, args, out_buf)
    return pl.pallas_call(
        functools.partial(_rglru_kernel, S=S),
        grid=(nb, ncb),
        in_specs=in_specs,
        out_specs=[
            pl.BlockSpec((S, W), lambda b, c: (roff + b, c)),
            pl.BlockSpec((None, 2, W), lambda b, c: (b, 0, c)),
        ],
        out_shape=[
            jax.ShapeDtypeStruct(out_shape, BF),
            jax.ShapeDtypeStruct((nb, 2, d_rnn), F32),
        ],
        input_output_aliases=aliases,
        scratch_shapes=[pltpu.VMEM((S + 2 * SUBLANES, W), F32)] + [pltpu.VMEM((S, W), F32)] * 4,
        compiler_params=_cparams("parallel", "parallel"),
        name="rglru",
    )(*args)


def _rope_tables(n_tok):
    t = jnp.arange(n_tok, dtype=jnp.int32)
    row, col = t // GRID_W, t % GRID_W
    n = 16
    inv = ROPE_THETA ** (-jnp.arange(n, dtype=F32) / n)
    ang_r = row.astype(F32)[:, None] * inv
    ang_c = col.astype(F32)[:, None] * inv

    def grp(ang):
        c, s = jnp.cos(ang), jnp.sin(ang)
        return jnp.concatenate([c, c], -1), jnp.concatenate([-s, s], -1)
    cr, sr = grp(ang_r)
    cc, sc = grp(ang_c)
    return jnp.concatenate([cr, cc, cr, cc], -1), jnp.concatenate([sr, sc, sr, sc], -1)


def _rope_kernel(x_ref, cos_ref, sin_ref, o_ref):
    cos = cos_ref[...]
    sin = sin_ref[...]
    first = (lax.broadcasted_iota(jnp.int32, cos.shape, 1) % 32) < 16
    for c in range(x_ref.shape[1] // LANES):
        x = x_ref[:, c * LANES:(c + 1) * LANES].astype(F32)
        partner = jnp.where(first, pltpu.roll(x, LANES - 16, 1), pltpu.roll(x, 16, 1))
        o_ref[:, c * LANES:(c + 1) * LANES] = (x * cos + partner * sin).astype(o_ref.dtype)


def _rope(x, cos, sin, row0, rows, col0, width, ds):
    tm = _pick(ds, 256, SUBLANES)
    tw = _pick(width, 1024, LANES)
    roff, coff, per = row0 // tm, col0 // tw, ds // tm
    return pl.pallas_call(
        _rope_kernel,
        grid=(rows // tm, width // tw),
        in_specs=[
            pl.BlockSpec((tm, tw), lambda i, j: (roff + i, coff + j)),
            pl.BlockSpec((tm, LANES), lambda i, j: (i % per, 0)),
            pl.BlockSpec((tm, LANES), lambda i, j: (i % per, 0)),
        ],
        out_specs=pl.BlockSpec((tm, tw), lambda i, j: (i, j)),
        out_shape=jax.ShapeDtypeStruct((rows, width), BF),
        compiler_params=_cparams("parallel", "parallel"),
        name="rope",
    )(x, cos, sin)


def _dattn_kernel(*refs, P, S, tq, G, lam_init):
    if P:
        q_ref, kc_ref, vc_ref, kl_ref, vl_ref, lq_ref, lk_ref, g_ref = refs[:8]
    else:
        q_ref, kl_ref, vl_ref, lq_ref, lk_ref, g_ref = refs[:6]
    o_ref, kk, vv = refs[-3:]
    C = 2 * DK_B

    @pl.when(pl.program_id(2) == 0)
    def _():
        if P:
            kk[0:P, :] = kc_ref[...].astype(BF)
            vv[0:P, :] = vc_ref[...].astype(BF)
        kk[P:P + S, :] = kl_ref[...].astype(BF)
        vv[P:P + S, :] = vl_ref[...].astype(BF)

    el = jnp.exp(jnp.sum(lq_ref[...] * lk_ref[...], axis=-1, keepdims=True))
    lam = el[0:1, :] - el[1:2, :] + lam_init
    sub = min(DATT_ROWS, tq)
    chains = []
    for g in range(G):
        cols = slice(g * C, (g + 1) * C)
        for r0 in range(0, tq, sub):
            q = q_ref[r0:r0 + sub, cols].astype(F32)
            lane = lax.broadcasted_iota(jnp.int32, q.shape, 1)
            qq = jnp.concatenate([jnp.where(lane < DK_B, q, 0.0), jnp.where(lane >= DK_B, q, 0.0)],
                                 axis=0).astype(BF)
            s = lax.dot_general(qq, kk[:, cols], (((1,), (1,)), ((), ())), preferred_element_type=F32)
            chains.append((cols, r0, s))
    for cols, r0, s in chains:
        e = jnp.exp2((s - jnp.max(s, axis=-1, keepdims=True)) * ((DK_B ** -0.5) * LOG2E))
        l = jnp.sum(e, axis=-1, keepdims=True)
        a = e[:sub, :] - e[sub:, :] * (lam * l[:sub, :] / l[sub:, :])
        o = jnp.dot(a.astype(BF), vv[:, cols], preferred_element_type=F32) / l[:sub, :]
        o_ref[r0:r0 + sub, cols] = ((_rms(o) * g_ref[...]) * (1.0 - lam_init)).astype(o_ref.dtype)


def _dattn(q, qrow0, qcol0, k_lat, krow0, kcol0, v_lat, vrow0, vcol0, cache, nb, S, lq, lk, g, lam_init,
           out_buf, orow0, ocol0):
    C = 2 * DK_B
    tq = _pick(S, 512, SUBLANES)
    nq = S // tq
    G = math.gcd(H_B, DATT_HEADS[0] if nq == 1 else DATT_HEADS[1])
    W = C * G
    P = cache[0].shape[1] if cache is not None else 0
    qr, kr, vr, orr = qrow0 // tq, krow0 // S, vrow0 // S, orow0 // tq
    qc, kc, vc, oc = qcol0 // W, kcol0 // W, vcol0 // W, ocol0 // W
    in_specs = [pl.BlockSpec((tq, W), lambda b, h, i: (qr + b * nq + i, qc + h))]
    args = [q]
    if P:
        in_specs += [pl.BlockSpec((None, P, W), lambda b, h, i: (b, 0, h)),
                     pl.BlockSpec((None, P, W), lambda b, h, i: (b, 0, h))]
        args += [cache[0], cache[1]]
    in_specs += [
        pl.BlockSpec((S, W), lambda b, h, i: (kr + b, kc + h)),
        pl.BlockSpec((S, W), lambda b, h, i: (vr + b, vc + h)),
        pl.BlockSpec((2, DK_B), lambda b, h, i: (0, 0)),
        pl.BlockSpec((2, DK_B), lambda b, h, i: (0, 0)),
        pl.BlockSpec((1, C), lambda b, h, i: (0, 0)),
    ]
    args += [k_lat, v_lat, lq, lk, g.reshape(1, C)]
    aliases = _in_place(in_specs, args, out_buf)
    return pl.pallas_call(
        functools.partial(_dattn_kernel, P=P, S=S, tq=tq, G=G, lam_init=lam_init),
        grid=(nb, H_B // G, nq),
        in_specs=in_specs,
        out_specs=pl.BlockSpec((tq, W), lambda b, h, i: (orr + b * nq + i, oc + h)),
        out_shape=jax.ShapeDtypeStruct(out_buf.shape, BF),
        input_output_aliases=aliases,
        scratch_shapes=[pltpu.VMEM((P + S, W), BF), pltpu.VMEM((P + S, W), BF)],
        compiler_params=_cparams("parallel", "parallel", "arbitrary"),
        name="diff_attn",
    )(*args)


def _mla_in_kernel(h_ref, w_ref, gq_ref, gkv_ref, cq_ref, ckv_ref, kr_ref, *, ql, kvl):
    z = jnp.dot(h_ref[...], w_ref[...], preferred_element_type=F32)
    cq_ref[...] = (_rms(z[:, :ql]) * gq_ref[...]).astype(cq_ref.dtype)
    ckv_ref[...] = _rms(z[:, ql:ql + kvl]) * gkv_ref[...]
    kr_ref[...] = z[:, ql + kvl:]


def _mla_in(h, w, gq, gkv, ql, kvl, tm):
    M, D = h.shape
    N = w.shape[1]
    return pl.pallas_call(
        functools.partial(_mla_in_kernel, ql=ql, kvl=kvl),
        grid=(M // tm,),
        in_specs=[
            pl.BlockSpec((tm, D), lambda i: (i, 0)),
            pl.BlockSpec((D, N), lambda i: (0, 0), pipeline_mode=pl.Buffered(1)),
            pl.BlockSpec((1, ql), lambda i: (0, 0)),
            pl.BlockSpec((1, kvl), lambda i: (0, 0)),
        ],
        out_specs=[
            pl.BlockSpec((tm, ql), lambda i: (i, 0)),
            pl.BlockSpec((tm, kvl), lambda i: (i, 0)),
            pl.BlockSpec((tm, LANES), lambda i: (i, 0)),
        ],
        out_shape=[
            jax.ShapeDtypeStruct((M, ql), BF),
            jax.ShapeDtypeStruct((M, kvl), F32),
            jax.ShapeDtypeStruct((M, LANES), F32),
        ],
        compiler_params=_cparams("parallel"),
        name="mla_in",
    )(h, w, gq.reshape(1, ql), gkv.reshape(1, kvl))


def _mla_kernel(*refs, P, S, tq, G):
    if P:
        qn_ref, qr_ref, kvc_ref, krc_ref, kvl_ref, krl_ref = refs[:6]
        segs = ((0, P, kvc_ref, krc_ref), (P, S, kvl_ref, krl_ref))
    else:
        qn_ref, qr_ref, kvl_ref, krl_ref = refs[:4]
        segs = ((0, S, kvl_ref, krl_ref),)
    o_ref, ks, vs = refs[-3:]
    C = LANES

    @pl.when(pl.program_id(2) == 0)
    def _():
        for r0, n, kv_ref, kr_ref in segs:
            kr = kr_ref[...].astype(F32)
            kr2 = (kr + pltpu.roll(kr, ROPE_C, 1)).astype(BF)
            for hd in range(2 * G):
                ks[hd, r0:r0 + n, 0:C] = kv_ref[:, 2 * hd * C:(2 * hd + 1) * C]
                ks[hd, r0:r0 + n, C:2 * C] = kr2
                vs[hd, r0:r0 + n, 0:C] = kv_ref[:, (2 * hd + 1) * C:(2 * hd + 2) * C]
                vs[hd, r0:r0 + n, C:2 * C] = jnp.ones((n, C), BF)

    c = ((NOPE_C + ROPE_C) ** -0.5) * LOG2E
    chains = []
    sub = min(ATT_ROWS, tq)
    for r0 in range(0, tq, sub):
        rows = slice(r0, r0 + sub)
        for p in range(G):
            qr = qr_ref[rows, p * C:(p + 1) * C].astype(F32)
            lane = lax.broadcasted_iota(jnp.int32, qr.shape, 1)
            for hh in range(2):
                hd = 2 * p + hh
                keep = (lane < ROPE_C) if hh == 0 else (lane >= ROPE_C)
                qf = jnp.concatenate([qn_ref[rows, hd * C:(hd + 1) * C], jnp.where(keep, qr, 0.0).astype(BF)],
                                     axis=1)
                s = lax.dot_general(qf, ks[hd], (((1,), (1,)), ((), ())), preferred_element_type=F32)
                chains.append((rows, hd, s))
    for rows, hd, s in chains:
        e = jnp.exp2((s - jnp.max(s, axis=-1, keepdims=True)) * c).astype(BF)
        oe = jnp.dot(e, vs[hd], preferred_element_type=F32)
        o_ref[rows, hd * C:(hd + 1) * C] = (oe[:, 0:C] / oe[:, C:2 * C]).astype(o_ref.dtype)


def _mla_attn(qn, qr, qr_row0, kv_lat, kr_lat, kr_row0, row0, cache, nb, S, out_buf, out_shape):
    C = LANES
    tq = _pick(S, 512, SUBLANES)
    nq = S // tq
    G = math.gcd(H_C // 2, MLA_PAIRS[0] if nq == 1 else MLA_PAIRS[1])
    P = cache[0].shape[0] // nb if cache is not None else 0
    r_q, r_qr, r_kv, r_kr = row0 // tq, qr_row0 // tq, row0 // S, kr_row0 // S
    in_specs = [pl.BlockSpec((tq, 2 * C * G), lambda b, h, i: (r_q + b * nq + i, h)),
                pl.BlockSpec((tq, C * G), lambda b, h, i: (r_qr + b * nq + i, h))]
    args = [qn, qr]
    if P:
        in_specs += [pl.BlockSpec((P, 4 * C * G), lambda b, h, i: (b, h)),
                     pl.BlockSpec((P, C), lambda b, h, i: (b, 0))]
        args += [cache[0], cache[1]]
    in_specs += [pl.BlockSpec((S, 4 * C * G), lambda b, h, i: (r_kv + b, h)),
                 pl.BlockSpec((S, C), lambda b, h, i: (r_kr + b, 0))]
    args += [kv_lat, kr_lat]
    aliases = _in_place(in_specs, args, out_buf)
    return pl.pallas_call(
        functools.partial(_mla_kernel, P=P, S=S, tq=tq, G=G),
        grid=(nb, H_C // (2 * G), nq),
        in_specs=in_specs,
        out_specs=pl.BlockSpec((tq, 2 * C * G), lambda b, h, i: (r_q + b * nq + i, h)),
        out_shape=jax.ShapeDtypeStruct(out_shape, BF),
        input_output_aliases=aliases,
        scratch_shapes=[pltpu.VMEM((2 * G, P + S, 2 * C), BF)] * 2,
        compiler_params=_cparams("parallel", "parallel", "arbitrary"),
        name="mla_attn",
    )(*args)


def kernel(x_prompt, x_sample, c, c_ctx, state_rglru, cache_dk, cache_dv, cache_ckv, cache_krope,
           mod_w, mod_b, norm_g, final_g, ffn_wg, ffn_wu, ffn_wd,
           ev_w_in, ev_conv_w, ev_conv_b, ev_wa, ev_ba, ev_wx, ev_bx, ev_lam, ev_lq, ev_lk,
           ev_subln_g, ev_w_out, od_w_in, od_qnorm_g, od_w_uq, od_kvnorm_g, od_w_ukv, od_w_out):
    assert NOPE_C == LANES and V_C == LANES and 2 * DK_B == LANES and 2 * ROPE_C == LANES
    NBP, SP, D = x_prompt.shape
    NB, DS, _ = x_sample.shape
    PAST = cache_dk.shape[2]
    L = mod_w.shape[0]
    MP, MS = NBP * SP, NB * DS
    tok = _Tokens(MP, DS, NB, 1 + NB)
    M = tok.m
    d_rnn = ev_lam.shape[-1]
    d_ff = ffn_wg.shape[-1]
    ql, kvl = od_qnorm_g.shape[-1], od_kvnorm_g.shape[-1]
    assert d_rnn // H_A == LANES

    n_cond = -(-(1 + NB) // SUBLANES) * SUBLANES
    cond = jnp.concatenate([c_ctx[None, :], c, jnp.zeros((n_cond - 1 - NB, D), F32)], axis=0)
    mod = _modulation(cond, mod_w, mod_b)
    modt = mod[:, :1 + NB].reshape(L * (1 + NB) * N_MOD, 1, D)
    normt = norm_g.reshape(L * 3, 1, D)

    fw = _FfnWeights(ffn_wg, ffn_wu, ffn_wd)

    cos, sin = _rope_tables(DS)
    tm = tok.tile(1024)
    new = {}
    parts = [(x_prompt.reshape(MP, D), 0), (x_sample.reshape(MS, D), MP)]

    for l in range(L):
        x = _ffn(parts, normt, modt, tok, l, 0, fw)
        h = _prenorm([(x, 0)], normt, modt, tok, l, 1)
        if l % 2 == 0:
            e = l // 2
            lam_init = 0.8 - 0.6 * math.exp(-0.3 * l)
            nh = H_B * LANES
            w_in = ev_w_in[e].astype(BF)
            w_qk = w_in[:, 2 * d_rnn:2 * d_rnn + 2 * nh]
            xg = _mm(h, w_in[:, :2 * d_rnn], F32, tm=tm, name="ev_in_rec")
            v = _mm(h, w_in[:, 2 * d_rnn + 2 * nh:], F32, tm=tm, name="ev_in_v")
            qk_p = _mm(h, w_qk, F32, tm=tm, rows=MP, name="ev_in_qk")
            qk_s = _mm(h, w_qk, BF, tm=tm, row0=MP, rows=MS, rope=(cos, sin, DS), name="ev_in_qk_rope")
            wcat = jnp.concatenate([ev_wa[e, 0], ev_wx[e, 0], ev_wa[e, 1], ev_wx[e, 1]], axis=-1).astype(BF)
            bcat = jnp.concatenate([t.reshape(H_A, 1, LANES) for t in
                                    (ev_ba[e, 0], ev_bx[e, 0], ev_ba[e, 1], ev_bx[e, 1])], axis=-1)
            mix_shape = (M, d_rnn + nh)
            rg = functools.partial(_rglru, xg, d_rnn=d_rnn, conv_w=ev_conv_w[e], conv_b=ev_conv_b[e],
                                   wcat=wcat, bcat=bcat, lam=ev_lam[e], out_shape=mix_shape)
            y_in, hfin = rg(row0=0, nb=NBP, S=SP, h0=jnp.zeros((NBP, 2, d_rnn), F32), out_buf=None)
            y_in, _ = rg(row0=MP, nb=NB, S=DS, h0=state_rglru[:, e], out_buf=y_in)
            y_in = _dattn(qk_p, 0, 0, qk_p, 0, nh, v, 0, 0, None, NBP, SP,
                          ev_lq[e], ev_lk[e], ev_subln_g[e], lam_init, y_in, 0, d_rnn)
            cache = (cache_dk[:, e].reshape(NB, PAST, nh), cache_dv[:, e].reshape(NB, PAST, nh))
            y_in = _dattn(qk_s, 0, 0, qk_s, 0, nh, v, MP, 0, cache, NB, DS,
                          ev_lq[e], ev_lk[e], ev_subln_g[e], lam_init, y_in, MP, d_rnn)
            w_out = ev_w_out[e].astype(BF)
            new.setdefault('rec', []).append(hfin)
            new.setdefault('dk', []).append(qk_p[:, nh:].reshape(NBP, SP, H_B, 2 * DK_B))
            new.setdefault('dv', []).append(v[:MP].reshape(NBP, SP, H_B, 2 * DK_B))
        else:
            o = l // 2
            w_in = jnp.pad(od_w_in[o].astype(BF), ((0, 0), (0, LANES - ROPE_C)))
            cqn, ckvn, kr = _mla_in(h, w_in, od_qnorm_g[o], od_kvnorm_g[o], ql, kvl, tok.tile(1024))
            w_uq = od_w_uq[o].astype(BF).reshape(ql, H_C, NOPE_C + ROPE_C)
            tm2, tn2 = tok.tile(2048), _pick(H_C * NOPE_C, 2048, LANES)
            qn = _mm(cqn, w_uq[:, :, :NOPE_C].reshape(ql, H_C * NOPE_C), BF, tm=tm2, tn=tn2, name="uq_nope")
            w_uq_r = w_uq[:, :, NOPE_C:].reshape(ql, H_C * ROPE_C)
            qr_p = _mm(cqn, w_uq_r, BF, tm=tm, rows=MP, name="uq_rope")
            qr_s = _mm(cqn, w_uq_r, BF, tm=tm, row0=MP, rows=MS, rope=(cos, sin, DS), name="uq_rope_rot")
            w_ukv = od_w_ukv[o].astype(BF)
            kv = _mm(ckvn, w_ukv, BF, tm=tm2, tn=tn2, name="ukv")
            mix_shape = (M, H_C * V_C)
            y_in = _mla_attn(qn, qr_p, 0, kv, kr, 0, 0, None, NBP, SP, None, mix_shape)
            kr_s = _rope(kr, cos, sin, MP, MS, 0, LANES, DS)
            ckv_c = cache_ckv[:, o].reshape(NB * PAST, kvl)
            kv_c = _mm(ckv_c, w_ukv, BF, tm=_pick(NB * PAST, 1024, SUBLANES), name="ukv_ctx")
            kr_c = jnp.pad(cache_krope[:, o].reshape(NB * PAST, ROPE_C), ((0, 0), (0, LANES - ROPE_C)))
            y_in = _mla_attn(qn, qr_s, 0, kv, kr_s, 0, MP, (kv_c, kr_c), NB, DS, y_in, mix_shape)
            w_out = od_w_out[o].astype(BF)
            new.setdefault('ckv', []).append(ckvn[:MP].reshape(NBP, SP, kvl))
            new.setdefault('kr', []).append(kr[:MP, :ROPE_C].reshape(NBP, SP, ROPE_C))
        x = _mm_residual(y_in, w_out, (), [(x, 0)], modt, tok, l, 5, 1.0, tm=tm, tn=_pick(D, 1024, LANES),
                         name="mix_out")
        x = _ffn([(x, 0)], normt, modt, tok, l, 2, fw)
        parts = [(x, 0)]

    y_prompt = _final_norm(x, final_g, 0, MP).reshape(NBP, SP, D)
    y_sample = _final_norm(x, final_g, MP, MS).reshape(NB, DS, D)
    return (y_prompt, y_sample, jnp.stack(new['rec'], axis=1), jnp.stack(new['dk'], axis=1),
            jnp.stack(new['dv'], axis=1), jnp.stack(new['ckv'], axis=1), jnp.stack(new['kr'], axis=1))
```

```python
import functools
import math

import jax
import jax.numpy as jnp
from jax import lax
from jax.experimental import pallas as pl
from jax.experimental.pallas import tpu as pltpu

BF = jnp.bfloat16
F32 = jnp.float32

GRID_W = 64
EPS = 1e-6
ROPE_THETA = 10000.0
N_MOD = 9
H_A = 16
CONV_W = 4
LRU_C = 8.0
H_B = 16
DK_B = 64
H_C = 32
NOPE_C = 128
ROPE_C = 64
V_C = 128

LANES = 128
SUBLANES = 8
VMEM_LIMIT = 56 * 1024 * 1024
FF_ALIGN = 1024
LOG2E = math.log2(math.e)
ATT_ROWS = 256
DATT_ROWS = 256
DATT_HEADS = (4, 2)
MLA_PAIRS = (8, 1)
SCAN_UNROLL = 4
NORM_ROWS = 512
RGLRU_BLOCKS = 2
CAST_BYTES = 2 * 1024 * 1024


def _cparams(*sem):
    return pltpu.CompilerParams(dimension_semantics=sem, vmem_limit_bytes=VMEM_LIMIT)


def _pick(n, pref, mult):
    best = None
    d = mult
    while d <= min(n, pref):
        if n % d == 0:
            best = d
        d += mult
    return n if best is None else best


def _sigmoid(x):
    return 1.0 / (1.0 + jnp.exp2(x * (-LOG2E)))


def _rms(x):
    return x * lax.rsqrt(jnp.mean(x * x, axis=-1, keepdims=True) + EPS)


def _in_place(in_specs, args, out_buf):
    if out_buf is None:
        return {}
    in_specs.append(pl.BlockSpec(memory_space=pl.ANY))
    args.append(out_buf)
    return {len(args) - 1: 0}


class _Cast:
    def __init__(self, w, jobs, n_steps):
        R, C = w.shape[-2:]
        tr = 2 * SUBLANES
        ok = [t for t in range(tr, R + 1, tr) if R % t == 0 and (R // t) * len(jobs) <= n_steps]
        big = [t for t in ok if 4 * t * C >= CAST_BYTES]
        self.tr = big[0] if big else ok[-1]
        self.w, self.jobs, self.nb, self.shape = w, jobs, R // self.tr, (len(jobs), R, C)
        self.n_active = self.nb * len(jobs)

    @staticmethod
    def capacity(w, n_steps):
        R, C = w.shape[-2:]
        return min(2, n_steps // max(1, (4 * R * C) // CAST_BYTES))

    def specs(self, lin):
        nb, jobs, tr, C = self.nb, self.jobs, self.tr, self.shape[2]

        def split(*g):
            b = jnp.minimum(lin(*g), nb * len(jobs) - 1)
            return b // nb, b % nb

        def src(*g):
            j, r = split(*g)
            l, sl = jobs[0]
            for n, (l2, s2) in enumerate(jobs[1:], 1):
                l, sl = jnp.where(j == n, l2, l), jnp.where(j == n, s2, sl)
            return (l, sl, r, 0)
        return (pl.BlockSpec((None, None, tr, C), src),
                pl.BlockSpec((None, tr, C), lambda *g: split(*g) + (0,)))


def _mod_kernel(c_ref, w_ref, b_ref, o_ref):
    k = pl.program_id(2)
    c = c_ref[...]
    s = (c * _sigmoid(c)).astype(BF)
    part = jnp.dot(s, w_ref[...].astype(BF), preferred_element_type=F32)

    @pl.when(k == 0)
    def _():
        o_ref[...] = part + b_ref[...]

    @pl.when(k > 0)
    def _():
        o_ref[...] += part


def _modulation(cond, mod_w, mod_b):
    R, D = cond.shape
    L, _, N = mod_w.shape
    tn = _pick(N, 2048, LANES)
    tk = _pick(D, 1024, LANES)
    return pl.pallas_call(
        _mod_kernel,
        grid=(L, N // tn, D // tk),
        in_specs=[
            pl.BlockSpec((R, tk), lambda l, n, k: (0, k)),
            pl.BlockSpec((None, tk, tn), lambda l, n, k: (l, k, n)),
            pl.BlockSpec((None, 1, tn), lambda l, n, k: (l, 0, n)),
        ],
        out_specs=pl.BlockSpec((None, R, tn), lambda l, n, k: (l, 0, n)),
        out_shape=jax.ShapeDtypeStruct((L, R, N), F32),
        compiler_params=_cparams("parallel", "parallel", "arbitrary"),
        name="modulation",
    )(cond, mod_w, mod_b.reshape(L, 1, N))


class _Tokens:
    def __init__(self, mp, ds, nb, n_rows):
        self.mp, self.ds, self.nb, self.n_rows = mp, ds, nb, n_rows
        self.m = mp + ds * nb

    def tile(self, pref):
        return _pick(math.gcd(self.mp, self.ds), pref, SUBLANES)

    def mod_index(self, layer, chunk, tm):
        mp, ds, n_rows = self.mp, self.ds, self.n_rows

        def f(i):
            r = jnp.where(i * tm < mp, 0, 1 + (i * tm - mp) // ds)
            return (layer * n_rows + r) * N_MOD + chunk
        return f


def _prenorm_kernel(x_ref, g_ref, sh_ref, sc_ref, *rest):
    o_ref = rest[-1]
    y = _rms(x_ref[...]) * g_ref[...]
    o_ref[...] = (y * (1.0 + sc_ref[...]) + sh_ref[...]).astype(o_ref.dtype)


def _prenorm(parts, normt, modt, tok, layer, sub):
    h = None
    for x, row0 in parts:
        rows, D = x.shape
        tm = tok.tile(NORM_ROWS)
        off = row0 // tm
        sh = tok.mod_index(layer, 3 * sub, tm)
        sc = tok.mod_index(layer, 3 * sub + 1, tm)
        in_specs = [
            pl.BlockSpec((tm, D), lambda i: (i, 0)),
            pl.BlockSpec((None, 1, D), lambda i: (layer * 3 + sub, 0, 0)),
            pl.BlockSpec((None, 1, D), lambda i: (sh(off + i), 0, 0)),
            pl.BlockSpec((None, 1, D), lambda i: (sc(off + i), 0, 0)),
        ]
        args = [x, normt, modt, modt]
        aliases = _in_place(in_specs, args, h)
        h = pl.pallas_call(
            _prenorm_kernel,
            grid=(rows // tm,),
            in_specs=in_specs,
            out_specs=pl.BlockSpec((tm, D), lambda i: (off + i, 0)),
            out_shape=jax.ShapeDtypeStruct((tok.m, D), BF),
            input_output_aliases=aliases,
            compiler_params=_cparams("parallel"),
            name="prenorm",
        )(*args)
    return h


def _final_norm_kernel(x_ref, g_ref, o_ref):
    o_ref[...] = _rms(x_ref[...]) * g_ref[...]


def _final_norm(x, g, row0, rows):
    D = x.shape[1]
    tm = _pick(math.gcd(row0, rows) if row0 else rows, NORM_ROWS, SUBLANES)
    off = row0 // tm
    return pl.pallas_call(
        _final_norm_kernel,
        grid=(rows // tm,),
        in_specs=[
            pl.BlockSpec((tm, D), lambda i: (off + i, 0)),
            pl.BlockSpec((1, D), lambda i: (0, 0)),
        ],
        out_specs=pl.BlockSpec((tm, D), lambda i: (i, 0)),
        out_shape=jax.ShapeDtypeStruct((rows, D), F32),
        compiler_params=_cparams("parallel"),
        name="final_norm",
    )(x, g.reshape(1, D))


def _rope_lanes(x, cos, sin, first):
    partner = jnp.where(first, pltpu.roll(x, LANES - 16, 1), pltpu.roll(x, 16, 1))
    return x * cos + partner * sin


def _mm_kernel(*refs, nk, residual, coef, rope=False, last_rows=None, cast=0):
    a_ref, w_ref = refs[:2]
    if residual:
        r_ref, g_ref = refs[2:4]
    if rope:
        cos_ref, sin_ref = refs[2:4]
    refs, acc_ref = (refs[:-1], refs[-1]) if nk > 1 else (refs, None)
    if cast:
        step = (pl.program_id(0) * pl.num_programs(1) + pl.program_id(1)) * nk + pl.program_id(2)

        @pl.when(step < cast)
        def _():
            refs[-1][...] = refs[-3][...].astype(BF)
        refs = refs[:-1]
    o_ref = refs[-1]

    def finish(acc):
        if residual:
            o_ref[...] = r_ref[...] + (coef * g_ref[...]) * acc
        elif rope:
            cos, sin = cos_ref[...], sin_ref[...]
            first = (lax.broadcasted_iota(jnp.int32, cos.shape, 1) % 32) < 16
            for c in range(acc.shape[1] // LANES):
                cols = slice(c * LANES, (c + 1) * LANES)
                o_ref[:, cols] = _rope_lanes(acc[:, cols], cos, sin, first).astype(o_ref.dtype)
        else:
            o_ref[...] = acc.astype(o_ref.dtype)

    def part(valid_rows=None):
        if valid_rows is None:
            return jnp.dot(a_ref[...].astype(BF), w_ref[...], preferred_element_type=F32)
        if valid_rows % LANES == 0:
            return jnp.dot(a_ref[:, :valid_rows].astype(BF), w_ref[:valid_rows, :], preferred_element_type=F32)
        w = w_ref[...]
        w = jnp.where(lax.broadcasted_iota(jnp.int32, w.shape, 0) < valid_rows, w, jnp.zeros_like(w))
        return jnp.dot(a_ref[...].astype(BF), w, preferred_element_type=F32)

    if nk == 1:
        finish(part(last_rows))
        return
    k = pl.program_id(2)

    @pl.when(k == 0)
    def _():
        acc_ref[...] = part()

    @pl.when(jnp.logical_and(k > 0, k < nk - 1))
    def _():
        acc_ref[...] += part()

    @pl.when(k == nk - 1)
    def _():
        finish(acc_ref[...] + part(last_rows))


def _mm(a, w, out_dtype, *, tm, tn=None, tk=None, row0=0, rows=None, rope=None, cast=None, name="mm"):
    K, N = w.shape
    tn = _pick(N, 1024, LANES) if tn is None else tn
    rows = a.shape[0] if rows is None else rows
    tk = K if tk is None else tk
    nk = K // tk
    off = row0 // tm
    in_specs = [
        pl.BlockSpec((tm, tk), lambda i, j, k: (off + i, k)),
        pl.BlockSpec((tk, tn), lambda i, j, k: (k, j)),
    ]
    args = [a, w]
    if rope is not None:
        cos, sin, ds = rope
        per = ds // tm
        in_specs += [pl.BlockSpec((tm, LANES), lambda i, j, k: (i % per, 0))] * 2
        args += [cos, sin]
    grid = (rows // tm, N // tn, nk)
    out_specs = [pl.BlockSpec((tm, tn), lambda i, j, k: (i, j))]
    out_shape = [jax.ShapeDtypeStruct((rows, N), out_dtype)]
    if cast is not None:
        job = _Cast(cast[0], cast[1], grid[0] * grid[1] * grid[2])
        cin, cout = job.specs(lambda i, j, k: (i * grid[1] + j) * nk + k)
        in_specs.append(cin)
        args.append(job.w)
        out_specs.append(cout)
        out_shape.append(jax.ShapeDtypeStruct(job.shape, BF))
    res = pl.pallas_call(
        functools.partial(_mm_kernel, nk=nk, residual=False, coef=None, rope=rope is not None,
                          cast=job.n_active if cast is not None else 0),
        grid=grid,
        in_specs=in_specs,
        out_specs=out_specs,
        out_shape=out_shape,
        scratch_shapes=[pltpu.VMEM((tm, tn), F32)] if nk > 1 else [],
        compiler_params=_cparams(*(("arbitrary",) * 3 if cast is not None else ("parallel", "parallel", "arbitrary"))),
        name=name,
    )(*args)
    return res if cast is not None else res[0]


def _mm_residual(a, w, w_lead, parts, modt, tok, layer, chunk, coef, *, tm, tn, tk=None, casts=None,
                 name="mm_res"):
    K, N = a.shape[1], w.shape[-1]
    tk = K if tk is None else tk
    nk = K // tk
    last_rows = w.shape[-2] - (nk - 1) * tk if w.shape[-2] < K else None
    assert last_rows is None or 0 < last_rows < tk
    gate = tok.mod_index(layer, chunk, tm)
    out = None
    cast_outs = []
    for n, (x, row0) in enumerate(parts):
        off = row0 // tm
        grid = (x.shape[0] // tm, N // tn, nk)
        in_specs = [
            pl.BlockSpec((tm, tk), lambda i, j, k: (off + i, k)),
            pl.BlockSpec((None,) * len(w_lead) + (tk, tn), lambda i, j, k: w_lead + (k, j)),
            pl.BlockSpec((tm, tn), lambda i, j, k: (i, j)),
            pl.BlockSpec((None, 1, tn), lambda i, j, k: (gate(off + i), 0, j)),
        ]
        args = [a, w, x, modt]
        aliases = _in_place(in_specs, args, out)
        out_specs = [pl.BlockSpec((tm, tn), lambda i, j, k: (off + i, j))]
        out_shape = [jax.ShapeDtypeStruct((tok.m, N), F32)]
        cast = casts[n] if casts else None
        if cast is not None:
            job = _Cast(cast[0], cast[1], grid[0] * grid[1] * grid[2])
            cin, cout = job.specs(lambda i, j, k: (i * grid[1] + j) * nk + k)
            in_specs.append(cin)
            args.append(job.w)
            out_specs.append(cout)
            out_shape.append(jax.ShapeDtypeStruct(job.shape, BF))
        res = pl.pallas_call(
            functools.partial(_mm_kernel, nk=nk, residual=True, coef=coef, last_rows=last_rows,
                              cast=job.n_active if cast is not None else 0),
            grid=grid,
            in_specs=in_specs,
            out_specs=out_specs,
            out_shape=out_shape,
            input_output_aliases=aliases,
            scratch_shapes=[pltpu.VMEM((tm, tn), F32)] if nk > 1 else [],
            compiler_params=_cparams(*(("arbitrary",) * 3 if cast is not None
                                       else ("parallel", "parallel", "arbitrary"))),
            name=name,
        )(*args)
        out = res[0]
        cast_outs.append(res[1] if cast is not None else None)
    return (out, cast_outs) if casts else out


def _ffn_up_kernel(h_ref, wg_ref, wu_ref, *rest, d_ff, cast=0):
    if cast:
        @pl.when(pl.program_id(0) * pl.num_programs(1) + pl.program_id(1) < cast)
        def _():
            rest[2][...] = rest[0][...].astype(BF)
        rest = rest[1:]
    o_ref = rest[0]
    h = h_ref[...]
    tf = o_ref.shape[1]
    hw = tf // 2
    for c in range(2):
        cols = slice(c * hw, (c + 1) * hw)
        g = jnp.dot(h, wg_ref[:, cols], preferred_element_type=F32)
        u = jnp.dot(h, wu_ref[:, cols], preferred_element_type=F32)
        col = pl.program_id(1) * tf + c * hw + lax.broadcasted_iota(jnp.int32, (1, hw), 1)
        o_ref[:, cols] = jnp.where(col < d_ff, (g * _sigmoid(g)) * u, 0.0).astype(o_ref.dtype)


def _ffn_tail_kernel(h_ref, wg_ref, wu_ref, buf_ref, o_ref, *, rem):
    h = h_ref[...]
    g = jnp.dot(h, wg_ref[...], preferred_element_type=F32)
    u = jnp.dot(h, wu_ref[...], preferred_element_type=F32)
    o_ref[:, :rem] = ((g * _sigmoid(g)) * u).astype(o_ref.dtype)
    o_ref[:, rem:] = jnp.zeros((o_ref.shape[0], o_ref.shape[1] - rem), o_ref.dtype)


def _ffn_up(h, wg, wg_lead, wu, wu_lead, fp, *, tm, tf, cast=None):
    M, D = h.shape
    d_ff = wg.shape[-1]
    nfull = d_ff // tf
    rem = d_ff - nfull * tf
    split = rem > 0 and rem % LANES == 0 and (nfull * tf) % rem == 0 and fp == (nfull + 1) * tf
    last = (d_ff - 1) // tf
    def wspec(lead, width, col):
        return pl.BlockSpec((None,) * len(lead) + (D, width), lambda i, *j: lead + (0, col(*j)))

    grid = (M // tm, nfull if split else fp // tf)
    in_specs = [pl.BlockSpec((tm, D), lambda i, j: (i, 0)),
                wspec(wg_lead, tf, lambda j: jnp.minimum(j, last)), wspec(wu_lead, tf, lambda j: jnp.minimum(j, last))]
    args = [h, wg, wu]
    out_specs = [pl.BlockSpec((tm, tf), lambda i, j: (i, j))]
    out_shape = [jax.ShapeDtypeStruct((M, fp), BF)]
    if cast is not None:
        job = _Cast(cast[0], cast[1], grid[0] * grid[1])
        cin, cout = job.specs(lambda i, j: i * grid[1] + j)
        in_specs.append(cin)
        args.append(job.w)
        out_specs.append(cout)
        out_shape.append(jax.ShapeDtypeStruct(job.shape, BF))
    res = pl.pallas_call(
        functools.partial(_ffn_up_kernel, d_ff=d_ff, cast=job.n_active if cast is not None else 0),
        grid=grid,
        in_specs=in_specs,
        out_specs=out_specs,
        out_shape=out_shape,
        compiler_params=_cparams(*(("arbitrary", "arbitrary") if cast is not None else ("parallel", "parallel"))),
        name="ffn_up",
    )(*args)
    a = res[0]
    if split:
        col = (nfull * tf) // rem
        a = pl.pallas_call(
            functools.partial(_ffn_tail_kernel, rem=rem),
            grid=(M // tm,),
            in_specs=[pl.BlockSpec((tm, D), lambda i: (i, 0)), wspec(wg_lead, rem, lambda: col),
                      wspec(wu_lead, rem, lambda: col), pl.BlockSpec(memory_space=pl.ANY)],
            out_specs=pl.BlockSpec((tm, tf), lambda i: (i, nfull)),
            out_shape=jax.ShapeDtypeStruct((M, fp), BF),
            input_output_aliases={3: 0},
            compiler_params=_cparams("parallel"),
            name="ffn_up_tail",
        )(h, wg, wu, a)
    return (a, res[1]) if cast is not None else a


class _FfnWeights:
    def __init__(self, wg, wu, wd):
        self.src = {'g': wg, 'u': wu, 'd': wd}
        self.ready = {}
        L = wg.shape[0]
        self.order = [(l, s) for l in range(L) for s in range(2)]

    def get(self, kind, ls):
        if (kind, ls) not in self.ready:
            self.ready[(kind, ls)] = (self.src[kind][ls].astype(BF), ())
        return self.ready[(kind, ls)]

    def offer(self, after, n_steps, kinds=('g', 'u')):
        todo = [(k, ls) for ls in self.order[self.order.index(after) + 1:] for k in kinds
                if (k, ls) not in self.ready]
        if not todo:
            return None
        kind = todo[0][0]
        jobs = [ls for k, ls in todo if k == kind][:_Cast.capacity(self.src[kind], n_steps)]
        return (kind, jobs) if jobs else None

    def request(self, offer):
        return None if offer is None else (self.src[offer[0]], offer[1])

    def deliver(self, offer, stacked):
        if offer is not None:
            for n, ls in enumerate(offer[1]):
                self.ready[(offer[0], ls)] = (stacked, (n,))


def _ffn(parts, normt, modt, tok, layer, sub, fw):
    ls = (layer, sub // 2)
    h = _prenorm(parts, normt, modt, tok, layer, sub)
    d_ff, D = fw.src['d'].shape[-2:]
    fp = -(-d_ff // FF_ALIGN) * FF_ALIGN
    tm = tok.tile(1024)
    (wg, wg_lead), (wu, wu_lead) = fw.get('g', ls), fw.get('u', ls)
    a, wd = _ffn_up(h, wg, wg_lead, wu, wu_lead, fp, tm=tm, tf=_pick(fp, 512, LANES), cast=(fw.src['d'], [ls]))
    tk = _pick(fp, max(fp // 4, LANES), LANES)
    tn = _pick(D, 1024, LANES)
    offers = []
    for x, _ in parts:
        offers.append(fw.offer(ls, (x.shape[0] // tm) * (D // tn) * (fp // tk)))
        fw.deliver(offers[-1], None)
    out, copies = _mm_residual(a, wd, (0,), parts, modt, tok, layer, 3 * sub + 2, 0.5, tm=tm, tn=tn, tk=tk,
                               casts=[fw.request(o) for o in offers], name="ffn_down")
    for o, c in zip(offers, copies):
        fw.deliver(o, c)
    return out


def _gelu_tanh(x):
    return x * (0.5 * (1.0 + jnp.tanh(math.sqrt(2.0 / math.pi) * (x + 0.044715 * (x * x * x)))))


def _softplus(x):
    return jnp.maximum(x, 0.0) + jnp.log1p(jnp.exp(-jnp.abs(x)))


def _rglru_kernel(*refs, S):
    xa_ref, ga_ref, cw_ref, cb_ref, w_ref, b_ref, lam_ref, h0_ref = refs[:8]
    y_ref, hfin_ref, pad_ref, af_ref, bf_ref, ab_ref, bb_ref = refs[-7:]
    C = LANES
    W = xa_ref.shape[1]
    P0 = SUBLANES
    pad_ref[0:P0, :] = jnp.zeros((P0, W), F32)
    pad_ref[P0 + S:P0 + S + P0, :] = jnp.zeros((P0, W), F32)
    pad_ref[P0:P0 + S, :] = xa_ref[...]
    cw = cw_ref[...]
    xc = jnp.broadcast_to(cb_ref[...], (S, W))
    for j in range(CONV_W):
        xc = xc + pad_ref[pl.ds(P0 - 2 + j, S), :] * cw[j:j + 1, :]

    lam = lam_ref[...]
    for blk in range(W // C):
        cols = slice(blk * C, (blk + 1) * C)
        xb = xc[:, cols]
        gates = jnp.dot(xb.astype(BF), w_ref[blk], preferred_element_type=F32) + b_ref[blk]
        for d, (a_ref, b_ref_) in enumerate(((af_ref, bf_ref), (ab_ref, bb_ref))):
            r = _sigmoid(gates[:, (2 * d) * C:(2 * d + 1) * C])
            i = _sigmoid(gates[:, (2 * d + 1) * C:(2 * d + 2) * C])
            log_a = (-LRU_C * r) * _softplus(-lam[d:d + 1, cols])
            a_ref[:, cols] = jnp.exp(log_a)
            th = jnp.tanh(log_a)
            one_minus_a2 = (-2.0 * th) / (1.0 - th)
            mult = jnp.where(one_minus_a2 > 0.0, one_minus_a2 * lax.rsqrt(one_minus_a2), 0.0)
            b_ref_[:, cols] = (mult * i) * xb

    rows = lax.broadcasted_iota(jnp.int32, (SUBLANES, W), 0)

    def tile_scan(a, b, down):
        for k in (1, 2, 4):
            if down:
                keep = rows >= k
                shift = k
            else:
                keep = rows < SUBLANES - k
                shift = SUBLANES - k
            a1 = jnp.where(keep, pltpu.roll(a, shift, 0), 1.0)
            b1 = jnp.where(keep, pltpu.roll(b, shift, 0), 0.0)
            b = a * b1 + b
            a = a * a1
        return a, b

    nt = S // SUBLANES
    unroll = max(SCAN_UNROLL * C // W, 1)
    unroll = unroll if nt % unroll == 0 else 1

    def body(tu, carry):
        hf, hb = carry
        tiles = []
        for u in range(unroll):
            t = tu * unroll + u
            r0 = pl.multiple_of(t * SUBLANES, SUBLANES)
            r1 = pl.multiple_of((nt - 1 - t) * SUBLANES, SUBLANES)
            fwd = tile_scan(af_ref[pl.ds(r0, SUBLANES), :], bf_ref[pl.ds(r0, SUBLANES), :], True)
            bwd = tile_scan(ab_ref[pl.ds(r1, SUBLANES), :], bb_ref[pl.ds(r1, SUBLANES), :], False)
            tiles.append((r0, fwd, r1, bwd))
        for r0, (a, b), r1, (a2, b2) in tiles:
            h = a * hf + b
            bf_ref[pl.ds(r0, SUBLANES), :] = h
            hf = jnp.broadcast_to(h[SUBLANES - 1:SUBLANES, :], (SUBLANES, W))
            g = a2 * hb + b2
            bb_ref[pl.ds(r1, SUBLANES), :] = g
            hb = jnp.broadcast_to(g[0:1, :], (SUBLANES, W))
        return hf, hb

    h0 = h0_ref[...]
    hf, hb = lax.fori_loop(0, nt // unroll, body, (jnp.broadcast_to(h0[0:1, :], (SUBLANES, W)),
                                                    jnp.broadcast_to(h0[1:2, :], (SUBLANES, W))))
    hfin_ref[0:1, :] = hf[0:1, :]
    hfin_ref[1:2, :] = hb[0:1, :]
    y_ref[...] = (_gelu_tanh(ga_ref[...]) * (bf_ref[...] + bb_ref[...])).astype(y_ref.dtype)


def _rglru(z, row0, nb, S, d_rnn, conv_w, conv_b, wcat, bcat, lam, h0, out_buf, out_shape):
    C = LANES
    nblk = math.gcd(d_rnn // C, RGLRU_BLOCKS)
    W = C * nblk
    ncb = d_rnn // W
    roff = row0 // S
    in_specs = [
        pl.BlockSpec((S, W), lambda b, c: (roff + b, c)),
        pl.BlockSpec((S, W), lambda b, c: (roff + b, ncb + c)),
        pl.BlockSpec((CONV_W, W), lambda b, c: (0, c)),
        pl.BlockSpec((1, W), lambda b, c: (0, c)),
        pl.BlockSpec((nblk, C, 4 * C), lambda b, c: (c, 0, 0)),
        pl.BlockSpec((nblk, 1, 4 * C), lambda b, c: (c, 0, 0)),
        pl.BlockSpec((2, W), lambda b, c: (0, c)),
        pl.BlockSpec((None, 2, W), lambda b, c: (b, 0, c)),
    ]
    args = [z, z, conv_w, conv_b.reshape(1, d_rnn), wcat, bcat, lam, h0]
    aliases = _in_place(in_specs, args, out_buf)
    return pl.pallas_call(
        functools.partial(_rglru_kernel, S=S),
        grid=(nb, ncb),
        in_specs=in_specs,
        out_specs=[
            pl.BlockSpec((S, W), lambda b, c: (roff + b, c)),
            pl.BlockSpec((None, 2, W), lambda b, c: (b, 0, c)),
        ],
        out_shape=[
            jax.ShapeDtypeStruct(out_shape, BF),
            jax.ShapeDtypeStruct((nb, 2, d_rnn), F32),
        ],
        input_output_aliases=aliases,
        scratch_shapes=[pltpu.VMEM((S + 2 * SUBLANES, W), F32)] + [pltpu.VMEM((S, W), F32)] * 4,
        compiler_params=_cparams("parallel", "parallel"),
        name="rglru",
    )(*args)


def _rope_tables(n_tok):
    t = jnp.arange(n_tok, dtype=jnp.int32)
    row, col = t // GRID_W, t % GRID_W
    n = 16
    inv = ROPE_THETA ** (-jnp.arange(n, dtype=F32) / n)
    ang_r = row.astype(F32)[:, None] * inv
    ang_c = col.astype(F32)[:, None] * inv

    def grp(ang):
        c, s = jnp.cos(ang), jnp.sin(ang)
        return jnp.concatenate([c, c], -1), jnp.concatenate([-s, s], -1)
    cr, sr = grp(ang_r)
    cc, sc = grp(ang_c)
    return jnp.concatenate([cr, cc, cr, cc], -1), jnp.concatenate([sr, sc, sr, sc], -1)


def _rope_kernel(x_ref, cos_ref, sin_ref, o_ref):
    cos = cos_ref[...]
    sin = sin_ref[...]
    first = (lax.broadcasted_iota(jnp.int32, cos.shape, 1) % 32) < 16
    for c in range(x_ref.shape[1] // LANES):
        x = x_ref[:, c * LANES:(c + 1) * LANES].astype(F32)
        partner = jnp.where(first, pltpu.roll(x, LANES - 16, 1), pltpu.roll(x, 16, 1))
        o_ref[:, c * LANES:(c + 1) * LANES] = (x * cos + partner * sin).astype(o_ref.dtype)


def _rope(x, cos, sin, row0, rows, col0, width, ds):
    tm = _pick(ds, 256, SUBLANES)
    tw = _pick(width, 1024, LANES)
    roff, coff, per = row0 // tm, col0 // tw, ds // tm
    return pl.pallas_call(
        _rope_kernel,
        grid=(rows // tm, width // tw),
        in_specs=[
            pl.BlockSpec((tm, tw), lambda i, j: (roff + i, coff + j)),
            pl.BlockSpec((tm, LANES), lambda i, j: (i % per, 0)),
            pl.BlockSpec((tm, LANES), lambda i, j: (i % per, 0)),
        ],
        out_specs=pl.BlockSpec((tm, tw), lambda i, j: (i, j)),
        out_shape=jax.ShapeDtypeStruct((rows, width), BF),
        compiler_params=_cparams("parallel", "parallel"),
        name="rope",
    )(x, cos, sin)


def _dattn_kernel(*refs, P, S, tq, G, lam_init):
    if P:
        q_ref, kc_ref, vc_ref, kl_ref, vl_ref, lq_ref, lk_ref, g_ref = refs[:8]
    else:
        q_ref, kl_ref, vl_ref, lq_ref, lk_ref, g_ref = refs[:6]
    o_ref, kk, vv = refs[-3:]
    C = 2 * DK_B

    @pl.when(pl.program_id(2) == 0)
    def _():
        if P:
            kk[0:P, :] = kc_ref[...].astype(BF)
            vv[0:P, :] = vc_ref[...].astype(BF)
        kk[P:P + S, :] = kl_ref[...].astype(BF)
        vv[P:P + S, :] = vl_ref[...].astype(BF)

    el = jnp.exp(jnp.sum(lq_ref[...] * lk_ref[...], axis=-1, keepdims=True))
    lam = el[0:1, :] - el[1:2, :] + lam_init
    sub = min(DATT_ROWS, tq)
    chains = []
    for g in range(G):
        cols = slice(g * C, (g + 1) * C)
        for r0 in range(0, tq, sub):
            q = q_ref[r0:r0 + sub, cols].astype(F32)
            lane = lax.broadcasted_iota(jnp.int32, q.shape, 1)
            qq = jnp.concatenate([jnp.where(lane < DK_B, q, 0.0), jnp.where(lane >= DK_B, q, 0.0)],
                                 axis=0).astype(BF)
            s = lax.dot_general(qq, kk[:, cols], (((1,), (1,)), ((), ())), preferred_element_type=F32)
            chains.append((cols, r0, s))
    for cols, r0, s in chains:
        e = jnp.exp2((s - jnp.max(s, axis=-1, keepdims=True)) * ((DK_B ** -0.5) * LOG2E))
        l = jnp.sum(e, axis=-1, keepdims=True)
        a = e[:sub, :] - e[sub:, :] * (lam * l[:sub, :] / l[sub:, :])
        o = jnp.dot(a.astype(BF), vv[:, cols], preferred_element_type=F32) / l[:sub, :]
        o_ref[r0:r0 + sub, cols] = ((_rms(o) * g_ref[...]) * (1.0 - lam_init)).astype(o_ref.dtype)


def _dattn(q, qrow0, qcol0, k_lat, krow0, kcol0, v_lat, vrow0, vcol0, cache, nb, S, lq, lk, g, lam_init,
           out_buf, orow0, ocol0):
    C = 2 * DK_B
    tq = _pick(S, 512, SUBLANES)
    nq = S // tq
    G = math.gcd(H_B, DATT_HEADS[0] if nq == 1 else DATT_HEADS[1])
    W = C * G
    P = cache[0].shape[1] if cache is not None else 0
    qr, kr, vr, orr = qrow0 // tq, krow0 // S, vrow0 // S, orow0 // tq
    qc, kc, vc, oc = qcol0 // W, kcol0 // W, vcol0 // W, ocol0 // W
    in_specs = [pl.BlockSpec((tq, W), lambda b, h, i: (qr + b * nq + i, qc + h))]
    args = [q]
    if P:
        in_specs += [pl.BlockSpec((None, P, W), lambda b, h, i: (b, 0, h)),
                     pl.BlockSpec((None, P, W), lambda b, h, i: (b, 0, h))]
        args += [cache[0], cache[1]]
    in_specs += [
        pl.BlockSpec((S, W), lambda b, h, i: (kr + b, kc + h)),
        pl.BlockSpec((S, W), lambda b, h, i: (vr + b, vc + h)),
        pl.BlockSpec((2, DK_B), lambda b, h, i: (0, 0)),
        pl.BlockSpec((2, DK_B), lambda b, h, i: (0, 0)),
        pl.BlockSpec((1, C), lambda b, h, i: (0, 0)),
    ]
    args += [k_lat, v_lat, lq, lk, g.reshape(1, C)]
    aliases = _in_place(in_specs, args, out_buf)
    return pl.pallas_call(
        functools.partial(_dattn_kernel, P=P, S=S, tq=tq, G=G, lam_init=lam_init),
        grid=(nb, H_B // G, nq),
        in_specs=in_specs,
        out_specs=pl.BlockSpec((tq, W), lambda b, h, i: (orr + b * nq + i, oc + h)),
        out_shape=jax.ShapeDtypeStruct(out_buf.shape, BF),
        input_output_aliases=aliases,
        scratch_shapes=[pltpu.VMEM((P + S, W), BF), pltpu.VMEM((P + S, W), BF)],
        compiler_params=_cparams("parallel", "parallel", "arbitrary"),
        name="diff_attn",
    )(*args)


def _mla_in_kernel(h_ref, w_ref, gq_ref, gkv_ref, cq_ref, ckv_ref, kr_ref, *, ql, kvl):
    z = jnp.dot(h_ref[...], w_ref[...], preferred_element_type=F32)
    cq_ref[...] = (_rms(z[:, :ql]) * gq_ref[...]).astype(cq_ref.dtype)
    ckv_ref[...] = _rms(z[:, ql:ql + kvl]) * gkv_ref[...]
    kr_ref[...] = z[:, ql + kvl:]


def _mla_in(h, w, gq, gkv, ql, kvl, tm):
    M, D = h.shape
    N = w.shape[1]
    return pl.pallas_call(
        functools.partial(_mla_in_kernel, ql=ql, kvl=kvl),
        grid=(M // tm,),
        in_specs=[
            pl.BlockSpec((tm, D), lambda i: (i, 0)),
            pl.BlockSpec((D, N), lambda i: (0, 0)),
            pl.BlockSpec((1, ql), lambda i: (0, 0)),
            pl.BlockSpec((1, kvl), lambda i: (0, 0)),
        ],
        out_specs=[
            pl.BlockSpec((tm, ql), lambda i: (i, 0)),
            pl.BlockSpec((tm, kvl), lambda i: (i, 0)),
            pl.BlockSpec((tm, LANES), lambda i: (i, 0)),
        ],
        out_shape=[
            jax.ShapeDtypeStruct((M, ql), BF),
            jax.ShapeDtypeStruct((M, kvl), F32),
            jax.ShapeDtypeStruct((M, LANES), F32),
        ],
        compiler_params=_cparams("parallel"),
        name="mla_in",
    )(h, w, gq.reshape(1, ql), gkv.reshape(1, kvl))


def _mla_kernel(*refs, P, S, tq, G):
    if P:
        qn_ref, qr_ref, kvc_ref, krc_ref, kvl_ref, krl_ref = refs[:6]
        segs = ((0, P, kvc_ref, krc_ref), (P, S, kvl_ref, krl_ref))
    else:
        qn_ref, qr_ref, kvl_ref, krl_ref = refs[:4]
        segs = ((0, S, kvl_ref, krl_ref),)
    o_ref, ks, vs = refs[-3:]
    C = LANES

    @pl.when(pl.program_id(2) == 0)
    def _():
        for r0, n, kv_ref, kr_ref in segs:
            kr = kr_ref[...].astype(F32)
            kr2 = (kr + pltpu.roll(kr, ROPE_C, 1)).astype(BF)
            for hd in range(2 * G):
                ks[hd, r0:r0 + n, 0:C] = kv_ref[:, 2 * hd * C:(2 * hd + 1) * C]
                ks[hd, r0:r0 + n, C:2 * C] = kr2
                vs[hd, r0:r0 + n, 0:C] = kv_ref[:, (2 * hd + 1) * C:(2 * hd + 2) * C]
                vs[hd, r0:r0 + n, C:2 * C] = jnp.ones((n, C), BF)

    c = ((NOPE_C + ROPE_C) ** -0.5) * LOG2E
    chains = []
    sub = min(ATT_ROWS, tq)
    for r0 in range(0, tq, sub):
        rows = slice(r0, r0 + sub)
        for p in range(G):
            qr = qr_ref[rows, p * C:(p + 1) * C].astype(F32)
            lane = lax.broadcasted_iota(jnp.int32, qr.shape, 1)
            for hh in range(2):
                hd = 2 * p + hh
                keep = (lane < ROPE_C) if hh == 0 else (lane >= ROPE_C)
                qf = jnp.concatenate([qn_ref[rows, hd * C:(hd + 1) * C], jnp.where(keep, qr, 0.0).astype(BF)],
                                     axis=1)
                s = lax.dot_general(qf, ks[hd], (((1,), (1,)), ((), ())), preferred_element_type=F32)
                chains.append((rows, hd, s))
    for rows, hd, s in chains:
        e = jnp.exp2((s - jnp.max(s, axis=-1, keepdims=True)) * c).astype(BF)
        oe = jnp.dot(e, vs[hd], preferred_element_type=F32)
        o_ref[rows, hd * C:(hd + 1) * C] = (oe[:, 0:C] / oe[:, C:2 * C]).astype(o_ref.dtype)


def _mla_attn(qn, qr, qr_row0, kv_lat, kr_lat, kr_row0, row0, cache, nb, S, out_buf, out_shape):
    C = LANES
    tq = _pick(S, 512, SUBLANES)
    nq = S // tq
    G = math.gcd(H_C // 2, MLA_PAIRS[0] if nq == 1 else MLA_PAIRS[1])
    P = cache[0].shape[0] // nb if cache is not None else 0
    r_q, r_qr, r_kv, r_kr = row0 // tq, qr_row0 // tq, row0 // S, kr_row0 // S
    in_specs = [pl.BlockSpec((tq, 2 * C * G), lambda b, h, i: (r_q + b * nq + i, h)),
                pl.BlockSpec((tq, C * G), lambda b, h, i: (r_qr + b * nq + i, h))]
    args = [qn, qr]
    if P:
        in_specs += [pl.BlockSpec((P, 4 * C * G), lambda b, h, i: (b, h)),
                     pl.BlockSpec((P, C), lambda b, h, i: (b, 0))]
        args += [cache[0], cache[1]]
    in_specs += [pl.BlockSpec((S, 4 * C * G), lambda b, h, i: (r_kv + b, h)),
                 pl.BlockSpec((S, C), lambda b, h, i: (r_kr + b, 0))]
    args += [kv_lat, kr_lat]
    aliases = _in_place(in_specs, args, out_buf)
    return pl.pallas_call(
        functools.partial(_mla_kernel, P=P, S=S, tq=tq, G=G),
        grid=(nb, H_C // (2 * G), nq),
        in_specs=in_specs,
        out_specs=pl.BlockSpec((tq, 2 * C * G), lambda b, h, i: (r_q + b * nq + i, h)),
        out_shape=jax.ShapeDtypeStruct(out_shape, BF),
        input_output_aliases=aliases,
        scratch_shapes=[pltpu.VMEM((2 * G, P + S, 2 * C), BF)] * 2,
        compiler_params=_cparams("parallel", "parallel", "arbitrary"),
        name="mla_attn",
    )(*args)


def kernel(x_prompt, x_sample, c, c_ctx, state_rglru, cache_dk, cache_dv, cache_ckv, cache_krope,
           mod_w, mod_b, norm_g, final_g, ffn_wg, ffn_wu, ffn_wd,
           ev_w_in, ev_conv_w, ev_conv_b, ev_wa, ev_ba, ev_wx, ev_bx, ev_lam, ev_lq, ev_lk,
           ev_subln_g, ev_w_out, od_w_in, od_qnorm_g, od_w_uq, od_kvnorm_g, od_w_ukv, od_w_out):
    assert NOPE_C == LANES and V_C == LANES and 2 * DK_B == LANES and 2 * ROPE_C == LANES
    NBP, SP, D = x_prompt.shape
    NB, DS, _ = x_sample.shape
    PAST = cache_dk.shape[2]
    L = mod_w.shape[0]
    MP, MS = NBP * SP, NB * DS
    tok = _Tokens(MP, DS, NB, 1 + NB)
    M = tok.m
    d_rnn = ev_lam.shape[-1]
    d_ff = ffn_wg.shape[-1]
    ql, kvl = od_qnorm_g.shape[-1], od_kvnorm_g.shape[-1]
    assert d_rnn // H_A == LANES

    n_cond = -(-(1 + NB) // SUBLANES) * SUBLANES
    cond = jnp.concatenate([c_ctx[None, :], c, jnp.zeros((n_cond - 1 - NB, D), F32)], axis=0)
    mod = _modulation(cond, mod_w, mod_b)
    modt = mod[:, :1 + NB].reshape(L * (1 + NB) * N_MOD, 1, D)
    normt = norm_g.reshape(L * 3, 1, D)

    fw = _FfnWeights(ffn_wg, ffn_wu, ffn_wd)

    cos, sin = _rope_tables(DS)
    tm = tok.tile(1024)
    new = {}
    parts = [(x_prompt.reshape(MP, D), 0), (x_sample.reshape(MS, D), MP)]

    for l in range(L):
        x = _ffn(parts, normt, modt, tok, l, 0, fw)
        h = _prenorm([(x, 0)], normt, modt, tok, l, 1)
        if l % 2 == 0:
            e = l // 2
            lam_init = 0.8 - 0.6 * math.exp(-0.3 * l)
            nh = H_B * LANES
            w_in = ev_w_in[e].astype(BF)
            w_qk = w_in[:, 2 * d_rnn:2 * d_rnn + 2 * nh]
            xg = _mm(h, w_in[:, :2 * d_rnn], F32, tm=tm, name="ev_in_rec")
            v = _mm(h, w_in[:, 2 * d_rnn + 2 * nh:], F32, tm=tm, name="ev_in_v")
            qk_p = _mm(h, w_qk, F32, tm=tm, rows=MP, name="ev_in_qk")
            qk_s = _mm(h, w_qk, BF, tm=tm, row0=MP, rows=MS, rope=(cos, sin, DS), name="ev_in_qk_rope")
            wcat = jnp.concatenate([ev_wa[e, 0], ev_wx[e, 0], ev_wa[e, 1], ev_wx[e, 1]], axis=-1).astype(BF)
            bcat = jnp.concatenate([t.reshape(H_A, 1, LANES) for t in
                                    (ev_ba[e, 0], ev_bx[e, 0], ev_ba[e, 1], ev_bx[e, 1])], axis=-1)
            mix_shape = (M, d_rnn + nh)
            rg = functools.partial(_rglru, xg, d_rnn=d_rnn, conv_w=ev_conv_w[e], conv_b=ev_conv_b[e],
                                   wcat=wcat, bcat=bcat, lam=ev_lam[e], out_shape=mix_shape)
            y_in, hfin = rg(row0=0, nb=NBP, S=SP, h0=jnp.zeros((NBP, 2, d_rnn), F32), out_buf=None)
            y_in, _ = rg(row0=MP, nb=NB, S=DS, h0=state_rglru[:, e], out_buf=y_in)
            y_in = _dattn(qk_p, 0, 0, qk_p, 0, nh, v, 0, 0, None, NBP, SP,
                          ev_lq[e], ev_lk[e], ev_subln_g[e], lam_init, y_in, 0, d_rnn)
            cache = (cache_dk[:, e].reshape(NB, PAST, nh), cache_dv[:, e].reshape(NB, PAST, nh))
            y_in = _dattn(qk_s, 0, 0, qk_s, 0, nh, v, MP, 0, cache, NB, DS,
                          ev_lq[e], ev_lk[e], ev_subln_g[e], lam_init, y_in, MP, d_rnn)
            w_out = ev_w_out[e].astype(BF)
            new.setdefault('rec', []).append(hfin)
            new.setdefault('dk', []).append(qk_p[:, nh:].reshape(NBP, SP, H_B, 2 * DK_B))
            new.setdefault('dv', []).append(v[:MP].reshape(NBP, SP, H_B, 2 * DK_B))
        else:
            o = l // 2
            w_in = jnp.pad(od_w_in[o].astype(BF), ((0, 0), (0, LANES - ROPE_C)))
            cqn, ckvn, kr = _mla_in(h, w_in, od_qnorm_g[o], od_kvnorm_g[o], ql, kvl, tok.tile(512))
            w_uq = od_w_uq[o].astype(BF).reshape(ql, H_C, NOPE_C + ROPE_C)
            tm2, tn2 = tok.tile(2048), _pick(H_C * NOPE_C, 2048, LANES)
            qn = _mm(cqn, w_uq[:, :, :NOPE_C].reshape(ql, H_C * NOPE_C), BF, tm=tm2, tn=tn2, name="uq_nope")
            w_uq_r = w_uq[:, :, NOPE_C:].reshape(ql, H_C * ROPE_C)
            qr_p = _mm(cqn, w_uq_r, BF, tm=tm, rows=MP, name="uq_rope")
            qr_s = _mm(cqn, w_uq_r, BF, tm=tm, row0=MP, rows=MS, rope=(cos, sin, DS), name="uq_rope_rot")
            w_ukv = od_w_ukv[o].astype(BF)
            kv = _mm(ckvn, w_ukv, BF, tm=tm2, tn=tn2, name="ukv")
            mix_shape = (M, H_C * V_C)
            y_in = _mla_attn(qn, qr_p, 0, kv, kr, 0, 0, None, NBP, SP, None, mix_shape)
            kr_s = _rope(kr, cos, sin, MP, MS, 0, LANES, DS)
            ckv_c = cache_ckv[:, o].reshape(NB * PAST, kvl)
            kv_c = _mm(ckv_c, w_ukv, BF, tm=_pick(NB * PAST, 1024, SUBLANES), name="ukv_ctx")
            kr_c = jnp.pad(cache_krope[:, o].reshape(NB * PAST, ROPE_C), ((0, 0), (0, LANES - ROPE_C)))
            y_in = _mla_attn(qn, qr_s, 0, kv, kr_s, 0, MP, (kv_c, kr_c), NB, DS, y_in, mix_shape)
            w_out = od_w_out[o].astype(BF)
            new.setdefault('ckv', []).append(ckvn[:MP].reshape(NBP, SP, kvl))
            new.setdefault('kr', []).append(kr[:MP, :ROPE_C].reshape(NBP, SP, ROPE_C))
        x = _mm_residual(y_in, w_out, (), [(x, 0)], modt, tok, l, 5, 1.0, tm=tm, tn=_pick(D, 1024, LANES),
                         name="mix_out")
        x = _ffn([(x, 0)], normt, modt, tok, l, 2, fw)
        parts = [(x, 0)]

    y_prompt = _final_norm(x, final_g, 0, MP).reshape(NBP, SP, D)
    y_sample = _final_norm(x, final_g, MP, MS).reshape(NB, DS, D)
    return (y_prompt, y_sample, jnp.stack(new['rec'], axis=1), jnp.stack(new['dk'], axis=1),
            jnp.stack(new['dv'], axis=1), jnp.stack(new['ckv'], axis=1), jnp.stack(new['kr'], axis=1))
```
